```python
import math
import jax, jax.numpy as jnp
from jax import lax
import numpy as np

D_MODEL = 2048
BATCH = 1
SEQ = 16384
DEPTH = 2

N_MIXERS = 2
N_GLA_LAYERS = (DEPTH + 1) // 2
N_CONV_LAYERS = DEPTH // 2
EPS = 1e-6

GLA_HEADS = 4
GLA_KD = D_MODEL // 2
GLA_VD = D_MODEL
GLA_DK = GLA_KD // GLA_HEADS
GLA_DV = GLA_VD // GLA_HEADS
GATE_RANK = 16
GATE_NORMALIZER = 16.0
GLA_CHUNK = 64
GLA_IN = 2 * GLA_KD + 2 * GLA_VD + GATE_RANK

CONV_WIDTH = 3

N_GROUPS = 4
EXPERTS_PER_GROUP = 8
N_EXPERTS = N_GROUPS * EXPERTS_PER_GROUP
TOP_K = 2
D_FF_EXPERT = D_MODEL // 4
ROW_BLOCK = 128

kernel_name = "hybrid_gla_shortconv_hmoe"


def rmsnorm(x, w):
    xf = x.astype(jnp.float32)
    xf = xf * lax.rsqrt(jnp.mean(xf * xf, axis=-1, keepdims=True) + EPS)
    return (xf * w.astype(jnp.float32)).astype(x.dtype)


def gla_mixer(xn, w_in, w_gk2, b_gk, head_norm, w_out):
    B, T, _ = xn.shape
    nc = T // GLA_CHUNK
    proj = xn @ w_in
    q, k, v, g, gl = jnp.split(
        proj, [GLA_KD, 2 * GLA_KD, 2 * GLA_KD + GLA_VD, 2 * GLA_KD + 2 * GLA_VD], axis=-1)
    log_a = jax.nn.log_sigmoid((gl @ w_gk2 + b_gk).astype(jnp.float32)) / GATE_NORMALIZER

    def to_chunks(t, d):
        return t.astype(jnp.float32).reshape(B, nc, GLA_CHUNK, GLA_HEADS, d).transpose(1, 0, 3, 2, 4)

    qc = to_chunks(q, GLA_DK) * (GLA_DK ** -0.5)
    kc = to_chunks(k, GLA_DK)
    vc = to_chunks(v, GLA_DV)
    gc = to_chunks(log_a, GLA_DK)
    causal = jnp.tril(jnp.ones((GLA_CHUNK, GLA_CHUNK), dtype=bool))

    def step(S, inp):
        qb, kb, vb, gb = inp
        b = jnp.cumsum(gb, axis=-2)
        o_inter = jnp.einsum('bhcd,bhdv->bhcv', qb * jnp.exp(b), S)
        diff = b[:, :, :, None, :] - b[:, :, None, :, :]
        decay = jnp.exp(jnp.where(causal[None, None, :, :, None], diff, -jnp.inf))
        A = jnp.einsum('bhid,bhjd,bhijd->bhij', qb, kb, decay)
        o = o_inter + jnp.einsum('bhij,bhjv->bhiv', A, vb)
        b_last = b[:, :, -1:, :]
        S = jnp.exp(b_last[:, :, 0, :])[..., None] * S + jnp.einsum(
            'bhjd,bhjv->bhdv', kb * jnp.exp(b_last - b), vb)
        return S, o

    S0 = jnp.zeros((B, GLA_HEADS, GLA_DK, GLA_DV), jnp.float32)
    _, o = lax.scan(step, S0, (qc, kc, vc, gc))
    o = o.transpose(1, 0, 3, 2, 4).reshape(B, T, GLA_HEADS, GLA_DV)
    o = o * lax.rsqrt(jnp.mean(o * o, axis=-1, keepdims=True) + EPS) * head_norm.astype(jnp.float32)
    o = o.reshape(B, T, GLA_VD).astype(xn.dtype) * jax.nn.silu(g)
    return o @ w_out


def conv_mixer(xn, w_in, conv_w, w_out):
    bg, cg, h = jnp.split(xn @ w_in, 3, axis=-1)
    u = cg * h
    up = jnp.pad(u, ((0, 0), (CONV_WIDTH - 1, 0), (0, 0)))
    T = u.shape[1]
    conv = conv_w[0] * up[:, 0:T] + conv_w[1] * up[:, 1:T + 1] + conv_w[2] * up[:, 2:T + 2]
    return (bg * conv) @ w_out


def hier_moe(xn, w_group, b_group, w_router, b_router, w_gate_up, w_down):
    N, D = xn.shape
    group_probs = jax.nn.softmax((xn @ w_group + b_group).astype(jnp.float32), axis=-1)
    top_gp, top_g = lax.top_k(group_probs, 1)
    exp_logits = (xn @ w_router + b_router).astype(jnp.float32).reshape(N, N_GROUPS, EXPERTS_PER_GROUP)
    sel = jnp.take_along_axis(exp_logits, top_g[:, :, None], axis=1)[:, 0]
    top_ep, top_e = lax.top_k(jax.nn.softmax(sel, axis=-1), TOP_K)
    top_ep = top_ep / jnp.sum(top_ep, axis=-1, keepdims=True)
    gates = (top_gp * top_ep).reshape(-1)
    ids = (top_g * EXPERTS_PER_GROUP + top_e).reshape(-1).astype(jnp.int32)
    tok = jnp.repeat(jnp.arange(N, dtype=jnp.int32), TOP_K)

    NK = N * TOP_K
    n_blocks = -(-(NK + N_EXPERTS * (ROW_BLOCK - 1)) // ROW_BLOCK)
    P = n_blocks * ROW_BLOCK
    order = jnp.argsort(ids)
    sorted_ids = ids[order]
    counts = jnp.bincount(ids, length=N_EXPERTS)
    padded = ((counts + ROW_BLOCK - 1) // ROW_BLOCK) * ROW_BLOCK
    starts = jnp.cumsum(counts) - counts
    pends = jnp.cumsum(padded)
    pstarts = pends - padded
    dest = pstarts[sorted_ids] + (jnp.arange(NK, dtype=jnp.int32) - starts[sorted_ids])
    row_tok = jnp.zeros((P,), jnp.int32).at[dest].set(tok[order])
    row_w = jnp.zeros((P,), jnp.float32).at[dest].set(gates[order])
    block_exp = jnp.minimum(
        jnp.searchsorted(pends, jnp.arange(n_blocks, dtype=jnp.int32) * ROW_BLOCK, side='right'),
        N_EXPERTS - 1)
    xs = xn[row_tok].reshape(n_blocks, ROW_BLOCK, D)

    def expert_block(args):
        xb, e = args
        gt, up = jnp.split(xb @ w_gate_up[e], 2, axis=-1)
        return (jax.nn.silu(gt) * up) @ w_down[e]

    yb = lax.map(expert_block, (xs, block_exp)).reshape(P, D)
    return jax.ops.segment_sum(yb * row_w[:, None].astype(yb.dtype), row_tok, num_segments=N)


def setup_inputs(seed: int = 0) -> dict:
    key = jax.random.key(seed)
    ks = jax.random.split(key, 20)
    n = jax.random.normal
    D, F = D_MODEL, D_FF_EXPERT
    return {
        "x": n(ks[0], (BATCH, SEQ, D), jnp.float32),
        "mix_norm": 1.0 + 0.02 * n(ks[1], (DEPTH, D), jnp.float32),
        "gla_w_in": n(ks[2], (N_GLA_LAYERS, D, GLA_IN), jnp.float32) * D ** -0.5,
        "gla_w_gk2": n(ks[3], (N_GLA_LAYERS, GATE_RANK, GLA_KD), jnp.float32) * GATE_RANK ** -0.5,
        "gla_b_gk": 0.1 * n(ks[4], (N_GLA_LAYERS, GLA_KD), jnp.float32),
        "gla_head_norm": 1.0 + 0.02 * n(ks[5], (N_GLA_LAYERS, GLA_DV), jnp.float32),
        "gla_w_out": n(ks[6], (N_GLA_LAYERS, GLA_VD, D), jnp.float32) * GLA_VD ** -0.5,
        "conv_w_in": n(ks[7], (N_CONV_LAYERS, D, 3 * D), jnp.float32) * D ** -0.5,
        "conv_w": n(ks[8], (N_CONV_LAYERS, CONV_WIDTH, D), jnp.float32) * CONV_WIDTH ** -0.5,
        "conv_w_out": n(ks[9], (N_CONV_LAYERS, D, D), jnp.float32) * D ** -0.5,
        "ffn_norm": 1.0 + 0.02 * n(ks[10], (DEPTH, D), jnp.float32),
        "w_group": n(ks[11], (DEPTH, D, N_GROUPS), jnp.float32) * D ** -0.5,
        "b_group": 0.01 * n(ks[12], (DEPTH, N_GROUPS), jnp.float32),
        "w_router": n(ks[13], (DEPTH, D, N_EXPERTS), jnp.float32) * D ** -0.5,
        "b_router": 0.01 * n(ks[14], (DEPTH, N_EXPERTS), jnp.float32),
        "w_gate_up": n(ks[15], (DEPTH, N_EXPERTS, D, 2 * F), jnp.float32) * D ** -0.5,
        "w_down": n(ks[16], (DEPTH, N_EXPERTS, F, D), jnp.float32) * F ** -0.5,
        "final_norm": 1.0 + 0.02 * n(ks[17], (D,), jnp.float32),
    }


def reference(x, mix_norm, gla_w_in, gla_w_gk2, gla_b_gk, gla_head_norm, gla_w_out,
              conv_w_in, conv_w, conv_w_out, ffn_norm, w_group, b_group, w_router, b_router,
              w_gate_up, w_down, final_norm):
    B, T, D = x.shape
    h = x
    for i in range(DEPTH):
        j = i // N_MIXERS
        hn = rmsnorm(h, mix_norm[i])
        if i % N_MIXERS == 0:
            m = gla_mixer(hn, gla_w_in[j], gla_w_gk2[j], gla_b_gk[j], gla_head_norm[j], gla_w_out[j])
        else:
            m = conv_mixer(hn, conv_w_in[j], conv_w[j], conv_w_out[j])
        h = h + m
        hn = rmsnorm(h, ffn_norm[i]).reshape(B * T, D)
        h = h + hier_moe(hn, w_group[i], b_group[i], w_router[i], b_router[i],
                         w_gate_up[i], w_down[i]).reshape(B, T, D)
    return rmsnorm(h, final_norm)
```

```python
import functools

import jax
import jax.numpy as jnp
from jax import lax
from jax.experimental import pallas as pl
from jax.experimental.pallas import tpu as pltpu

F32 = jnp.float32
BF16 = jnp.bfloat16
I32 = jnp.int32

EPS = 1e-6
GLA_HEADS = 4
GATE_RANK = 16
GATE_NORMALIZER = 16.0
CONV_WIDTH = 3
N_GROUPS = 4
EXPERTS_PER_GROUP = 8
N_EXPERTS = N_GROUPS * EXPERTS_PER_GROUP
TOP_K = 2

LANES = 128
SUBLANES = 8
BF16_ROWS = 16
VMEM_LIMIT = 56 * 1024 * 1024

GLA_CHUNK = 64
GLA_SUB = 16
GLA_GROUP = 128
ROW_BLOCK = 256


def _tile(n, pref):
    t = min(n, pref)
    assert n % t == 0, (n, t)
    return t


def _params(*sem):
    return pltpu.CompilerParams(dimension_semantics=sem, vmem_limit_bytes=VMEM_LIMIT)


def _dot(a, b):
    return jnp.dot(a, b, preferred_element_type=F32)


def _dot_nt(a, b):
    return lax.dot_general(a, b, (((1,), (1,)), ((), ())), preferred_element_type=F32)


def _dot_tn(a, b):
    return lax.dot_general(a, b, (((0,), (0,)), ((), ())), preferred_element_type=F32)


def _rms(x, w):
    return x * lax.rsqrt(jnp.mean(x * x, axis=-1, keepdims=True) + EPS) * w


def _norm_proj_kernel(*refs, with_extra):
    if with_extra:
        x_ref, nw_ref, w_ref, we_ref, o_ref, e_ref, xn_ref = refs
    else:
        x_ref, nw_ref, w_ref, o_ref, xn_ref = refs

    @pl.when(pl.program_id(1) == 0)
    def _():
        xn_ref[...] = _rms(x_ref[...], nw_ref[...]).astype(BF16)
        if with_extra:
            e_ref[...] = _dot(xn_ref[...], we_ref[...])

    o_ref[...] = _dot(xn_ref[...], w_ref[...]).astype(o_ref.dtype)


def norm_proj(x, norm_w, w, w_extra=None):
    T, D = x.shape
    N = w.shape[1]
    tm, tn = _tile(T, 1024), _tile(N, 1024)
    in_specs = [
        pl.BlockSpec((tm, D), lambda i, j: (i, 0)),
        pl.BlockSpec((1, D), lambda i, j: (0, 0)),
        pl.BlockSpec((D, tn), lambda i, j: (0, j)),
    ]
    out_shape = [jax.ShapeDtypeStruct((T, N), BF16)]
    out_specs = [pl.BlockSpec((tm, tn), lambda i, j: (i, j))]
    args = [x, norm_w.reshape(1, D), w]
    if w_extra is not None:
        in_specs.append(pl.BlockSpec((D, LANES), lambda i, j: (0, 0)))
        out_shape.append(jax.ShapeDtypeStruct((T, LANES), F32))
        out_specs.append(pl.BlockSpec((tm, LANES), lambda i, j: (i, 0)))
        args.append(w_extra)
    out = pl.pallas_call(
        functools.partial(_norm_proj_kernel, with_extra=w_extra is not None),
        grid=(T // tm, N // tn),
        in_specs=in_specs,
        out_specs=out_specs,
        out_shape=out_shape,
        scratch_shapes=[pltpu.VMEM((tm, D), BF16)],
        compiler_params=_params("arbitrary", "arbitrary"),
        name="norm_proj",
    )(*args)
    return out if w_extra is not None else out[0]


def _gla_kernel(q_ref, k_ref, v_ref, g_ref, gl_ref, wgk_ref, bgk_ref, hnorm_ref, o_ref,
                st_ref, bs_ref, qt_ref, kh_ref, bref_ref, *, dk, tb):
    C, SUB, GRP = GLA_CHUNK, GLA_SUB, GLA_GROUP
    NS = C // SUB
    CPG = GRP // C
    scale = dk ** -0.5

    @pl.when(pl.program_id(1) == 0)
    def _():
        st_ref[...] = jnp.zeros_like(st_ref)

    r = lax.broadcasted_iota(I32, (GRP, GRP), 0)
    c = lax.broadcasted_iota(I32, (GRP, GRP), 1)
    same = (r // SUB) == (c // SUB)
    low = jnp.where(same & (c <= r), 1.0, 0.0).astype(BF16)
    upp = jnp.where(same & (c > r), 1.0, 0.0).astype(BF16)
    rr = lax.broadcasted_iota(I32, (CPG * SUBLANES, GRP), 0)
    cc = lax.broadcasted_iota(I32, (CPG * SUBLANES, GRP), 1)
    blk = rr % SUBLANES
    mref = jnp.where(((cc // C) == (rr // SUBLANES)) & ((cc % C) < SUB * blk) & (blk <= NS),
                     1.0, 0.0).astype(BF16)

    for gi in range(tb // GRP):
        sl = pl.ds(gi * GRP, GRP)
        z = _dot(gl_ref[sl, :].astype(BF16), wgk_ref[...]) + bgk_ref[...]
        la = (jnp.minimum(z, 0.0) - jnp.log(1.0 + jnp.exp(-jnp.abs(z)))) * (1.0 / GATE_NORMALIZER)
        hi = la.astype(BF16)
        lo = (la - hi.astype(F32)).astype(BF16)
        bs = _dot(low, hi) + _dot(low, lo)
        ru = _dot(upp, hi) + _dot(upp, lo)
        br = _dot(mref, hi) + _dot(mref, lo)
        bs_ref[sl, :] = bs
        qt_ref[sl, :] = q_ref[sl, :].astype(F32) * jnp.exp(bs) * scale
        kh_ref[sl, :] = k_ref[sl, :].astype(F32) * jnp.exp(ru)
        bref_ref[pl.ds(gi * CPG, CPG)] = br.reshape(CPG, SUBLANES, dk)

    lane = lax.broadcasted_iota(I32, (SUB, C), 1)
    subrow = lax.broadcasted_iota(I32, (SUB, dk), 0)

    def chunk(n, carry):
        r0 = pl.multiple_of(n * C, C)
        rows = pl.ds(r0, C)
        bn = bref_ref[n]
        eb = jnp.exp(bn)
        st = st_ref[...]
        qt = qt_ref[rows, :]
        kh = kh_ref[rows, :]
        bs = bs_ref[rows, :]
        v = v_ref[rows, :]
        qraw = q_ref[rows, :].astype(F32) * scale
        kraw = k_ref[rows, :].astype(F32)

        qin = jnp.concatenate([qt[I * SUB:(I + 1) * SUB] * eb[I:I + 1] for I in range(NS)], axis=0)
        o = _dot_nt(qin.astype(BF16), st.astype(BF16))

        a_rows = []
        for I in range(NS):
            blk_rows = slice(I * SUB, (I + 1) * SUB)
            q_blk, k_blk, bs_blk = qraw[blk_rows], kraw[blk_rows], bs[blk_rows]
            acc = jnp.zeros((SUB, C), F32)
            for j in range(SUB):
                e = jnp.where(subrow >= j, jnp.exp(bs_blk - bs_blk[j:j + 1]), 0.0)
                p = (q_blk * k_blk[j:j + 1]) * e
                acc = jnp.where(lane == I * SUB + j, jnp.sum(p, axis=-1, keepdims=True), acc)
            if I > 0:
                parts = []
                for J in range(NS):
                    if J < I:
                        parts.append(kh[J * SUB:(J + 1) * SUB] * jnp.exp(bn[I:I + 1] - bn[J + 1:J + 2]))
                    else:
                        parts.append(jnp.zeros((SUB, dk), F32))
                k_dec = jnp.concatenate(parts, axis=0)
                acc = acc + _dot_nt(qt[blk_rows].astype(BF16), k_dec.astype(BF16))
            a_rows.append(acc)
        a = jnp.concatenate(a_rows, axis=0)
        o = o + _dot(a.astype(BF16), v)

        k_st = jnp.concatenate(
            [kh[J * SUB:(J + 1) * SUB] * jnp.exp(bn[NS:NS + 1] - bn[J + 1:J + 2]) for J in range(NS)],
            axis=0)
        st_ref[...] = st * eb[NS:NS + 1] + _dot_tn(v, k_st.astype(BF16))

        g = g_ref[rows, :].astype(F32)
        o_ref[rows, :] = (_rms(o, hnorm_ref[...]) * (g * jax.nn.sigmoid(g))).astype(o_ref.dtype)
        return carry

    lax.fori_loop(0, tb // C, chunk, 0)


def gla(proj, gl, w_gk2, b_gk, head_norm, *, dk, dv):
    T = proj.shape[0]
    H = GLA_HEADS
    tb = _tile(T, 512)
    assert tb % GLA_GROUP == 0 and dk % LANES == 0 and dv % LANES == 0
    kd, vd = H * dk, H * dv
    k_off, v_off, g_off = kd // dk, (2 * kd) // dv, (2 * kd + vd) // dv
    return pl.pallas_call(
        functools.partial(_gla_kernel, dk=dk, tb=tb),
        grid=(H, T // tb),
        in_specs=[
            pl.BlockSpec((tb, dk), lambda h, t: (t, h)),
            pl.BlockSpec((tb, dk), lambda h, t: (t, k_off + h)),
            pl.BlockSpec((tb, dv), lambda h, t: (t, v_off + h)),
            pl.BlockSpec((tb, dv), lambda h, t: (t, g_off + h)),
            pl.BlockSpec((tb, LANES), lambda h, t: (t, 0)),
            pl.BlockSpec((LANES, dk), lambda h, t: (0, h)),
            pl.BlockSpec((1, dk), lambda h, t: (0, h)),
            pl.BlockSpec((1, dv), lambda h, t: (0, 0)),
        ],
        out_specs=pl.BlockSpec((tb, dv), lambda h, t: (t, h)),
        out_shape=jax.ShapeDtypeStruct((T, vd), BF16),
        scratch_shapes=[
            pltpu.VMEM((dv, dk), F32),
            pltpu.VMEM((tb, dk), F32),
            pltpu.VMEM((tb, dk), F32),
            pltpu.VMEM((tb, dk), F32),
            pltpu.VMEM((tb // GLA_CHUNK, SUBLANES, dk), F32),
        ],
        compiler_params=_params("arbitrary", "arbitrary"),
        name="gla",
    )(proj, proj, proj, proj, gl, w_gk2, b_gk, head_norm)


def _route(hn, rhi_ref, rlo_ref, rb_ref):
    tm = hn.shape[0]
    hi = hn.astype(BF16)
    lo = (hn - hi.astype(F32)).astype(BF16)
    logits = _dot(hi, rhi_ref[...]) + _dot(hi, rlo_ref[...]) + _dot(lo, rhi_ref[...]) + rb_ref[...]
    lane = lax.broadcasted_iota(I32, (tm, LANES), 1)
    lane_f = lane.astype(F32)
    neg = -jnp.inf
    far = float(LANES)

    def first_max(vals):
        m = jnp.max(vals, axis=-1, keepdims=True)
        idx = jnp.min(jnp.where(vals == m, lane_f, far), axis=-1, keepdims=True)
        return m, idx

    lg = jnp.where((lane >= N_EXPERTS) & (lane < N_EXPERTS + N_GROUPS), logits, neg)
    mg, gidx = first_max(lg)
    top_gp = 1.0 / jnp.sum(jnp.exp(lg - mg), axis=-1, keepdims=True)
    grp = gidx.astype(I32) - N_EXPERTS
    le = jnp.where((lane < N_EXPERTS) & ((lane // EXPERTS_PER_GROUP) == grp), logits, neg)
    m1, i1 = first_max(le)
    m2, i2 = first_max(jnp.where(lane_f == i1, neg, le))
    e2 = jnp.exp(m2 - m1)
    w1 = 1.0 / (1.0 + e2)
    return i1.astype(I32), i2.astype(I32), top_gp * w1, top_gp * (e2 * w1)


def _out_router_kernel(*refs, conv):
    if conv:
        (pb_ref, pc_ref, ph_ref, hc_ref, hh_ref, cw_ref, w_ref, hin_ref, fnw_ref, rhi_ref, rlo_ref,
         rb_ref, h_ref, hn_ref, id0_ref, id1_ref, g0_ref, g1_ref, cnt_ref) = refs
    else:
        (y_ref, w_ref, hin_ref, fnw_ref, rhi_ref, rlo_ref,
         rb_ref, h_ref, hn_ref, id0_ref, id1_ref, g0_ref, g1_ref, cnt_ref) = refs
    i = pl.program_id(0)

    if conv:
        u = pc_ref[...].astype(F32) * ph_ref[...].astype(F32)
        prev = hc_ref[...].astype(F32) * hh_ref[...].astype(F32)
        prev = jnp.where(i > 0, prev, 0.0)
        row = lax.broadcasted_iota(I32, u.shape, 0)
        last, last2 = prev[BF16_ROWS - 1:BF16_ROWS], prev[BF16_ROWS - 2:BF16_ROWS - 1]
        u1 = jnp.where(row == 0, last, pltpu.roll(u, 1, 0))
        u2 = jnp.where(row == 0, last2, jnp.where(row == 1, last, pltpu.roll(u, 2, 0)))
        cw = cw_ref[...]
        y = (pb_ref[...].astype(F32) * (cw[0:1] * u2 + cw[1:2] * u1 + cw[2:3] * u)).astype(BF16)
    else:
        y = y_ref[...]

    h = hin_ref[...] + _dot(y, w_ref[...])
    h_ref[...] = h
    hn = _rms(h, fnw_ref[...])
    hn_ref[...] = hn
    i1, i2, g1, g2 = _route(hn, rhi_ref, rlo_ref, rb_ref)
    shape = id0_ref.shape
    id0_ref[...] = jnp.broadcast_to(i1, shape)
    id1_ref[...] = jnp.broadcast_to(i2, shape)
    g0_ref[...] = jnp.broadcast_to(g1, shape)
    g1_ref[...] = jnp.broadcast_to(g2, shape)

    @pl.when(i == 0)
    def _():
        cnt_ref[...] = jnp.zeros_like(cnt_ref)

    lane = lax.broadcasted_iota(I32, shape, 1)
    hit = jnp.where((lane == i1) | (lane == i2), 1.0, 0.0)
    cnt_ref[...] += jnp.sum(hit, axis=0, keepdims=True)


def out_router(y, w_out, h_in, ffn_norm_w, r_hi, r_lo, r_b, conv_w=None):
    T, D = h_in.shape
    K = w_out.shape[0]
    tm = _tile(T, 256)
    conv = conv_w is not None
    row = lambda i: (i, 0)
    fixed = lambda i: (0, 0)
    if conv:
        hb = tm // BF16_ROWS
        halo = lambda col: (lambda i: (jnp.maximum(i * hb - 1, 0), col))
        in_specs = [
            pl.BlockSpec((tm, K), lambda i: (i, 0)),
            pl.BlockSpec((tm, K), lambda i: (i, 1)),
            pl.BlockSpec((tm, K), lambda i: (i, 2)),
            pl.BlockSpec((BF16_ROWS, K), halo(1)),
            pl.BlockSpec((BF16_ROWS, K), halo(2)),
            pl.BlockSpec((SUBLANES, K), fixed),
        ]
        cw = jnp.zeros((SUBLANES, K), F32).at[:CONV_WIDTH].set(conv_w)
        args = [y, y, y, y, y, cw]
    else:
        in_specs = [pl.BlockSpec((tm, K), row)]
        args = [y]
    in_specs += [
        pl.BlockSpec((K, D), fixed),
        pl.BlockSpec((tm, D), row),
        pl.BlockSpec((1, D), fixed),
        pl.BlockSpec((D, LANES), fixed),
        pl.BlockSpec((D, LANES), fixed),
        pl.BlockSpec((1, LANES), fixed),
    ]
    args += [w_out, h_in, ffn_norm_w.reshape(1, D), r_hi, r_lo, r_b]
    wide = lambda dt: jax.ShapeDtypeStruct((T, LANES), dt)
    return pl.pallas_call(
        functools.partial(_out_router_kernel, conv=conv),
        grid=(T // tm,),
        in_specs=in_specs,
        out_specs=[pl.BlockSpec((tm, D), row), pl.BlockSpec((tm, D), row)]
        + [pl.BlockSpec((tm, LANES), row)] * 4 + [pl.BlockSpec((1, LANES), fixed)],
        out_shape=[jax.ShapeDtypeStruct((T, D), F32), jax.ShapeDtypeStruct((T, D), F32),
                   wide(I32), wide(I32), wide(F32), wide(F32), jax.ShapeDtypeStruct((1, LANES), F32)],
        compiler_params=_params("arbitrary"),
        name="out_router_conv" if conv else "out_router",
    )(*args)


def _rank_kernel(id0_ref, id1_ref, pst_ref, d0_ref, d1_ref, carry_ref):
    @pl.when(pl.program_id(0) == 0)
    def _():
        carry_ref[...] = jnp.zeros_like(carry_ref)

    tb = id0_ref.shape[0]
    lane = lax.broadcasted_iota(I32, (tb, LANES), 1)
    oh0 = lane == id0_ref[...]
    oh1 = lane == id1_ref[...]
    hit = jnp.where(oh0 | oh1, 1.0, 0.0)
    r = lax.broadcasted_iota(I32, (tb, tb), 0)
    c = lax.broadcasted_iota(I32, (tb, tb), 1)
    before = jnp.where(c < r, 1.0, 0.0).astype(BF16)
    base = _dot(before, hit.astype(BF16)) + carry_ref[...] + pst_ref[...]
    d0 = jnp.sum(jnp.where(oh0, base, 0.0), axis=-1, keepdims=True)
    d1 = jnp.sum(jnp.where(oh1, base, 0.0), axis=-1, keepdims=True)
    d0_ref[...] = jnp.broadcast_to(d0.astype(I32), (tb, LANES))
    d1_ref[...] = jnp.broadcast_to(d1.astype(I32), (tb, LANES))
    carry_ref[...] += jnp.sum(hit, axis=0, keepdims=True)


def rank(id0, id1, pstart_row):
    T = id0.shape[0]
    tb = _tile(T, 256)
    row = lambda i: (i, 0)
    return pl.pallas_call(
        _rank_kernel,
        grid=(T // tb,),
        in_specs=[pl.BlockSpec((tb, LANES), row), pl.BlockSpec((tb, LANES), row),
                  pl.BlockSpec((1, LANES), lambda i: (0, 0))],
        out_specs=[pl.BlockSpec((tb, LANES), row)] * 2,
        out_shape=[jax.ShapeDtypeStruct((T, LANES), I32)] * 2,
        scratch_shapes=[pltpu.VMEM((1, LANES), F32)],
        compiler_params=_params("arbitrary"),
        name="rank",
    )(id0, id1, pstart_row)


def _invert_kernel(d0_ref, d1_ref, rt_ref):
    def clear(p, carry):
        rt_ref[p] = 0
        return carry

    lax.fori_loop(0, rt_ref.shape[0], clear, 0, unroll=8)

    def place(t, carry):
        rt_ref[d0_ref[t]] = t
        rt_ref[d1_ref[t]] = t
        return carry

    lax.fori_loop(0, d0_ref.shape[0], place, 0, unroll=8)


def invert(d0, d1, n_rows):
    smem = pl.BlockSpec(memory_space=pltpu.SMEM)
    return pl.pallas_call(
        _invert_kernel,
        in_specs=[smem, smem],
        out_specs=smem,
        out_shape=jax.ShapeDtypeStruct((n_rows,), I32),
        name="invert",
    )(d0, d1)


def _row_copy(src_hbm, dst_ref, sem, src_row, dst_row):
    return pltpu.make_async_copy(src_hbm.at[pl.ds(src_row, 1), :], dst_ref.at[pl.ds(dst_row, 1), :], sem)


def _experts_kernel(bexp_ref, rtok_ref, nused_ref, hn_hbm, wgu_ref, wd_ref, y_ref, xbuf, sem, *, ff):
    i = pl.program_id(0)
    rb = y_ref.shape[0]
    slot = i % 2
    n_used = nused_ref[0]

    def gather(blk, slot, start):
        def body(r, carry):
            cp = _row_copy(hn_hbm, xbuf.at[slot], sem.at[slot], rtok_ref[blk * rb + r], r)
            if start:
                cp.start()
            else:
                cp.wait()
            return carry

        lax.fori_loop(0, rb, body, 0, unroll=8)

    @pl.when(i == 0)
    def _():
        gather(0, 0, True)

    @pl.when(i + 1 < n_used)
    def _():
        gather(i + 1, 1 - slot, True)

    @pl.when(i < n_used)
    def _():
        gather(i, slot, False)
        x = xbuf[slot].astype(BF16)
        gu = _dot(x, wgu_ref[0])
        gt, up = gu[:, :ff], gu[:, ff:]
        act = (gt * jax.nn.sigmoid(gt) * up).astype(BF16)
        y_ref[...] = _dot(act, wd_ref[0])

    @pl.when(i >= n_used)
    def _():
        y_ref[...] = jnp.zeros_like(y_ref)


def experts(block_exp, row_tok, n_used, hn, w_gate_up, w_down):
    T, D = hn.shape
    E, _, ff2 = w_gate_up.shape
    ff = ff2 // 2
    P = row_tok.shape[0]
    rb = ROW_BLOCK
    grid_spec = pltpu.PrefetchScalarGridSpec(
        num_scalar_prefetch=3,
        grid=(P // rb,),
        in_specs=[
            pl.BlockSpec(memory_space=pl.ANY),
            pl.BlockSpec((1, D, ff2), lambda i, be, rt, nu: (be[i], 0, 0)),
            pl.BlockSpec((1, ff, D), lambda i, be, rt, nu: (be[i], 0, 0)),
        ],
        out_specs=pl.BlockSpec((rb, D), lambda i, be, rt, nu: (i, 0)),
        scratch_shapes=[pltpu.VMEM((2, rb, D), F32), pltpu.SemaphoreType.DMA((2,))],
    )
    return pl.pallas_call(
        functools.partial(_experts_kernel, ff=ff),
        grid_spec=grid_spec,
        out_shape=jax.ShapeDtypeStruct((P, D), F32),
        compiler_params=_params("arbitrary"),
        name="experts",
    )(block_exp, row_tok, n_used, hn, w_gate_up, w_down)


def _combine_kernel(d0_ref, d1_ref, yb_hbm, h_ref, g0_ref, g1_ref, nw_ref, o_ref, buf, sem, *, final_norm):
    i = pl.program_id(0)
    n = pl.num_programs(0)
    tc = o_ref.shape[0]
    slot = i % 2

    def gather(blk, slot, start):
        def body(r, carry):
            t = blk * tc + r
            for k, d_ref in enumerate((d0_ref, d1_ref)):
                cp = _row_copy(yb_hbm, buf.at[slot, k], sem.at[slot], d_ref[t], r)
                if start:
                    cp.start()
                else:
                    cp.wait()
            return carry

        lax.fori_loop(0, tc, body, 0, unroll=8)

    @pl.when(i == 0)
    def _():
        gather(0, 0, True)

    @pl.when(i + 1 < n)
    def _():
        gather(i + 1, 1 - slot, True)

    gather(i, slot, False)
    reps = o_ref.shape[1] // LANES
    wide = lambda g_ref: jnp.concatenate([g_ref[...]] * reps, axis=1)
    out = h_ref[...] + wide(g0_ref) * buf[slot, 0] + wide(g1_ref) * buf[slot, 1]
    if final_norm:
        out = _rms(out, nw_ref[...])
    o_ref[...] = out


def combine(d0, d1, yb, h, g0, g1, norm_w=None):
    T, D = h.shape
    tc = _tile(T, 256)
    final_norm = norm_w is not None
    nw = (norm_w if final_norm else jnp.ones((D,), F32)).reshape(1, D)
    row = lambda i, a, b: (i, 0)
    grid_spec = pltpu.PrefetchScalarGridSpec(
        num_scalar_prefetch=2,
        grid=(T // tc,),
        in_specs=[
            pl.BlockSpec(memory_space=pl.ANY),
            pl.BlockSpec((tc, D), row),
            pl.BlockSpec((tc, LANES), row),
            pl.BlockSpec((tc, LANES), row),
            pl.BlockSpec((1, D), lambda i, a, b: (0, 0)),
        ],
        out_specs=pl.BlockSpec((tc, D), row),
        scratch_shapes=[pltpu.VMEM((2, TOP_K, tc, D), F32), pltpu.SemaphoreType.DMA((2,))],
    )
    return pl.pallas_call(
        functools.partial(_combine_kernel, final_norm=final_norm),
        grid_spec=grid_spec,
        out_shape=jax.ShapeDtypeStruct((T, D), F32),
        compiler_params=_params("arbitrary"),
        name="combine_norm" if final_norm else "combine",
    )(d0, d1, yb, h, g0, g1, nw)


def _pad_cols(w, n):
    return jnp.pad(w, ((0, 0), (0, n - w.shape[1])))


def moe(h, hn, id0, id1, g0, g1, counts, w_gate_up, w_down, final_norm_w=None):
    T, D = h.shape
    rb = ROW_BLOCK
    n_blocks = -(-(T * TOP_K + N_EXPERTS * (rb - 1)) // rb)
    cnt = counts[0, :N_EXPERTS].astype(I32)
    padded = ((cnt + rb - 1) // rb) * rb
    pends = jnp.cumsum(padded)
    pstart_row = _pad_cols((pends - padded).astype(F32).reshape(1, N_EXPERTS), LANES)
    d0b, d1b = rank(id0, id1, pstart_row)
    d0, d1 = d0b[:, 0], d1b[:, 0]
    row_tok = invert(d0, d1, n_blocks * rb)
    block_exp = jnp.minimum(
        jnp.searchsorted(pends, jnp.arange(n_blocks, dtype=I32) * rb, side="right"), N_EXPERTS - 1
    ).astype(I32)
    n_used = (pends[-1:] // rb).astype(I32)
    yb = experts(block_exp, row_tok, n_used, hn, w_gate_up.astype(BF16), w_down.astype(BF16))
    return combine(d0, d1, yb, h, g0, g1, final_norm_w)


def _router_weights(w_group, b_group, w_router, b_router):
    w = _pad_cols(jnp.concatenate([w_router, w_group], axis=1), LANES)
    b = _pad_cols(jnp.concatenate([b_router, b_group]).reshape(1, -1), LANES)
    hi = w.astype(BF16)
    lo = (w - hi.astype(F32)).astype(BF16)
    return hi, lo, b


def kernel(x, mix_norm, gla_w_in, gla_w_gk2, gla_b_gk, gla_head_norm, gla_w_out, conv_w_in, conv_w,
           conv_w_out, ffn_norm, w_group, b_group, w_router, b_router, w_gate_up, w_down, final_norm):
    B, T, D = x.shape
    assert B == 1, "the recurrence state is carried across the whole row axis"
    h = x.reshape(T, D)
    kd, vd = D // 2, D
    dk, dv = kd // GLA_HEADS, vd // GLA_HEADS

    w_in = gla_w_in[0]
    n_main = 2 * kd + 2 * vd
    proj, gl = norm_proj(h, mix_norm[0], w_in[:, :n_main].astype(BF16),
                         _pad_cols(w_in[:, n_main:], LANES).astype(BF16))
    w_gk2 = jnp.pad(gla_w_gk2[0], ((0, LANES - GATE_RANK), (0, 0))).astype(BF16)
    o = gla(proj, gl, w_gk2, gla_b_gk[0].reshape(1, kd), gla_head_norm[0].reshape(1, dv), dk=dk, dv=dv)
    routed = out_router(o, gla_w_out[0].astype(BF16), h, ffn_norm[0],
                        *_router_weights(w_group[0], b_group[0], w_router[0], b_router[0]))
    h = moe(*routed, w_gate_up[0], w_down[0])

    proj = norm_proj(h, mix_norm[1], conv_w_in[0].astype(BF16))
    routed = out_router(proj, conv_w_out[0].astype(BF16), h, ffn_norm[1],
                        *_router_weights(w_group[1], b_group[1], w_router[1], b_router[1]),
                        conv_w=conv_w[0])
    out = moe(*routed, w_gate_up[1], w_down[1], final_norm_w=final_norm)
    return out.reshape(B, T, D)
```

```python
import functools

import jax
import jax.numpy as jnp
from jax import lax
from jax.experimental import pallas as pl
from jax.experimental.pallas import tpu as pltpu

F32 = jnp.float32
BF16 = jnp.bfloat16
I32 = jnp.int32

EPS = 1e-6
GLA_HEADS = 4
GATE_RANK = 16
GATE_NORMALIZER = 16.0
CONV_WIDTH = 3
N_GROUPS = 4
EXPERTS_PER_GROUP = 8
N_EXPERTS = N_GROUPS * EXPERTS_PER_GROUP
TOP_K = 2

LANES = 128
SUBLANES = 8
BF16_ROWS = 16
VMEM_LIMIT = 56 * 1024 * 1024

GLA_CHUNK = 64
GLA_SUB = 16
GLA_GROUP = 128
ROW_BLOCK = 256


def _tile(n, pref):
    t = min(n, pref)
    assert n % t == 0, (n, t)
    return t


def _params(*sem):
    return pltpu.CompilerParams(dimension_semantics=sem, vmem_limit_bytes=VMEM_LIMIT)


def _dot(a, b):
    return jnp.dot(a, b, preferred_element_type=F32)


def _dot_nt(a, b):
    return lax.dot_general(a, b, (((1,), (1,)), ((), ())), preferred_element_type=F32)


def _dot_tn(a, b):
    return lax.dot_general(a, b, (((0,), (0,)), ((), ())), preferred_element_type=F32)


def _rms(x, w):
    return x * lax.rsqrt(jnp.mean(x * x, axis=-1, keepdims=True) + EPS) * w


def _norm_proj_kernel(*refs, with_extra):
    if with_extra:
        x_ref, nw_ref, w_ref, we_ref, o_ref, e_ref, xn_ref = refs
    else:
        x_ref, nw_ref, w_ref, o_ref, xn_ref = refs

    @pl.when(pl.program_id(1) == 0)
    def _():
        xn_ref[...] = _rms(x_ref[...], nw_ref[...]).astype(BF16)
        if with_extra:
            e_ref[...] = _dot(xn_ref[...], we_ref[...])

    o_ref[...] = _dot(xn_ref[...], w_ref[...]).astype(o_ref.dtype)


def norm_proj(x, norm_w, w, w_extra=None):
    T, D = x.shape
    N = w.shape[1]
    tm, tn = _tile(T, 1024), _tile(N, 1024)
    in_specs = [
        pl.BlockSpec((tm, D), lambda i, j: (i, 0)),
        pl.BlockSpec((1, D), lambda i, j: (0, 0)),
        pl.BlockSpec((D, tn), lambda i, j: (0, j)),
    ]
    out_shape = [jax.ShapeDtypeStruct((T, N), BF16)]
    out_specs = [pl.BlockSpec((tm, tn), lambda i, j: (i, j))]
    args = [x, norm_w.reshape(1, D), w]
    if w_extra is not None:
        in_specs.append(pl.BlockSpec((D, LANES), lambda i, j: (0, 0)))
        out_shape.append(jax.ShapeDtypeStruct((T, LANES), F32))
        out_specs.append(pl.BlockSpec((tm, LANES), lambda i, j: (i, 0)))
        args.append(w_extra)
    out = pl.pallas_call(
        functools.partial(_norm_proj_kernel, with_extra=w_extra is not None),
        grid=(T // tm, N // tn),
        in_specs=in_specs,
        out_specs=out_specs,
        out_shape=out_shape,
        scratch_shapes=[pltpu.VMEM((tm, D), BF16)],
        compiler_params=_params("arbitrary", "arbitrary"),
        name="norm_proj",
    )(*args)
    return out if w_extra is not None else out[0]


def _gla_kernel(q_ref, k_ref, v_ref, g_ref, gl_ref, wgk_ref, bgk_ref, hnorm_ref, o_ref,
                st_ref, bs_ref, qt_ref, kh_ref, bref_ref, *, dk, tb):
    C, SUB, GRP = GLA_CHUNK, GLA_SUB, GLA_GROUP
    NS = C // SUB
    CPG = GRP // C
    scale = dk ** -0.5

    @pl.when(pl.program_id(1) == 0)
    def _():
        st_ref[...] = jnp.zeros_like(st_ref)

    r = lax.broadcasted_iota(I32, (GRP, GRP), 0)
    c = lax.broadcasted_iota(I32, (GRP, GRP), 1)
    same = (r // SUB) == (c // SUB)
    low = jnp.where(same & (c <= r), 1.0, 0.0).astype(BF16)
    upp = jnp.where(same & (c > r), 1.0, 0.0).astype(BF16)
    rr = lax.broadcasted_iota(I32, (CPG * SUBLANES, GRP), 0)
    cc = lax.broadcasted_iota(I32, (CPG * SUBLANES, GRP), 1)
    blk = rr % SUBLANES
    mref = jnp.where(((cc // C) == (rr // SUBLANES)) & ((cc % C) < SUB * blk) & (blk <= NS),
                     1.0, 0.0).astype(BF16)

    for gi in range(tb // GRP):
        sl = pl.ds(gi * GRP, GRP)
        z = _dot(gl_ref[sl, :].astype(BF16), wgk_ref[...]) + bgk_ref[...]
        la = (jnp.minimum(z, 0.0) - jnp.log(1.0 + jnp.exp(-jnp.abs(z)))) * (1.0 / GATE_NORMALIZER)
        hi = la.astype(BF16)
        lo = (la - hi.astype(F32)).astype(BF16)
        bs = _dot(low, hi) + _dot(low, lo)
        ru = _dot(upp, hi) + _dot(upp, lo)
        br = _dot(mref, hi) + _dot(mref, lo)
        bs_ref[sl, :] = bs
        qt_ref[sl, :] = q_ref[sl, :].astype(F32) * jnp.exp(bs) * scale
        kh_ref[sl, :] = k_ref[sl, :].astype(F32) * jnp.exp(ru)
        bref_ref[pl.ds(gi * CPG, CPG)] = br.reshape(CPG, SUBLANES, dk)

    lane = lax.broadcasted_iota(I32, (SUB, C), 1)
    subrow = lax.broadcasted_iota(I32, (SUB, dk), 0)

    def chunk(n, carry):
        r0 = pl.multiple_of(n * C, C)
        rows = pl.ds(r0, C)
        bn = bref_ref[n]
        eb = jnp.exp(bn)
        st = st_ref[...]
        qt = qt_ref[rows, :]
        kh = kh_ref[rows, :]
        bs = bs_ref[rows, :]
        v = v_ref[rows, :]
        qraw = q_ref[rows, :].astype(F32) * scale
        kraw = k_ref[rows, :].astype(F32)

        qin = jnp.concatenate([qt[I * SUB:(I + 1) * SUB] * eb[I:I + 1] for I in range(NS)], axis=0)
        o = _dot_nt(qin.astype(BF16), st.astype(BF16))

        a_rows = []
        for I in range(NS):
            blk_rows = slice(I * SUB, (I + 1) * SUB)
            q_blk, k_blk, bs_blk = qraw[blk_rows], kraw[blk_rows], bs[blk_rows]
            acc = jnp.zeros((SUB, C), F32)
            for j in range(SUB):
                e = jnp.where(subrow >= j, jnp.exp(bs_blk - bs_blk[j:j + 1]), 0.0)
                p = (q_blk * k_blk[j:j + 1]) * e
                acc = jnp.where(lane == I * SUB + j, jnp.sum(p, axis=-1, keepdims=True), acc)
            if I > 0:
                parts = []
                for J in range(NS):
                    if J < I:
                        parts.append(kh[J * SUB:(J + 1) * SUB] * jnp.exp(bn[I:I + 1] - bn[J + 1:J + 2]))
                    else:
                        parts.append(jnp.zeros((SUB, dk), F32))
                k_dec = jnp.concatenate(parts, axis=0)
                acc = acc + _dot_nt(qt[blk_rows].astype(BF16), k_dec.astype(BF16))
            a_rows.append(acc)
        a = jnp.concatenate(a_rows, axis=0)
        o = o + _dot(a.astype(BF16), v)

        k_st = jnp.concatenate(
            [kh[J * SUB:(J + 1) * SUB] * jnp.exp(bn[NS:NS + 1] - bn[J + 1:J + 2]) for J in range(NS)],
            axis=0)
        st_ref[...] = st * eb[NS:NS + 1] + _dot_tn(v, k_st.astype(BF16))

        g = g_ref[rows, :].astype(F32)
        o_ref[rows, :] = (_rms(o, hnorm_ref[...]) * (g * jax.nn.sigmoid(g))).astype(o_ref.dtype)
        return carry

    lax.fori_loop(0, tb // C, chunk, 0)


def gla(proj, gl, w_gk2, b_gk, head_norm, *, dk, dv):
    T = proj.shape[0]
    H = GLA_HEADS
    tb = _tile(T, 512)
    assert tb % GLA_GROUP == 0 and dk % LANES == 0 and dv % LANES == 0
    kd, vd = H * dk, H * dv
    k_off, v_off, g_off = kd // dk, (2 * kd) // dv, (2 * kd + vd) // dv
    return pl.pallas_call(
        functools.partial(_gla_kernel, dk=dk, tb=tb),
        grid=(H, T // tb),
        in_specs=[
            pl.BlockSpec((tb, dk), lambda h, t: (t, h)),
            pl.BlockSpec((tb, dk), lambda h, t: (t, k_off + h)),
            pl.BlockSpec((tb, dv), lambda h, t: (t, v_off + h)),
            pl.BlockSpec((tb, dv), lambda h, t: (t, g_off + h)),
            pl.BlockSpec((tb, LANES), lambda h, t: (t, 0)),
            pl.BlockSpec((LANES, dk), lambda h, t: (0, h)),
            pl.BlockSpec((1, dk), lambda h, t: (0, h)),
            pl.BlockSpec((1, dv), lambda h, t: (0, 0)),
        ],
        out_specs=pl.BlockSpec((tb, dv), lambda h, t: (t, h)),
        out_shape=jax.ShapeDtypeStruct((T, vd), BF16),
        scratch_shapes=[
            pltpu.VMEM((dv, dk), F32),
            pltpu.VMEM((tb, dk), F32),
            pltpu.VMEM((tb, dk), F32),
            pltpu.VMEM((tb, dk), F32),
            pltpu.VMEM((tb // GLA_CHUNK, SUBLANES, dk), F32),
        ],
        compiler_params=_params("arbitrary", "arbitrary"),
        name="gla",
    )(proj, proj, proj, proj, gl, w_gk2, b_gk, head_norm)


def _route(hn, rhi_ref, rlo_ref, rb_ref):
    tm = hn.shape[0]
    hi = hn.astype(BF16)
    lo = (hn - hi.astype(F32)).astype(BF16)
    logits = _dot(hi, rhi_ref[...]) + _dot(hi, rlo_ref[...]) + _dot(lo, rhi_ref[...]) + rb_ref[...]
    lane = lax.broadcasted_iota(I32, (tm, LANES), 1)
    lane_f = lane.astype(F32)
    neg = -jnp.inf
    far = float(LANES)

    def first_max(vals):
        m = jnp.max(vals, axis=-1, keepdims=True)
        idx = jnp.min(jnp.where(vals == m, lane_f, far), axis=-1, keepdims=True)
        return m, idx

    lg = jnp.where((lane >= N_EXPERTS) & (lane < N_EXPERTS + N_GROUPS), logits, neg)
    mg, gidx = first_max(lg)
    top_gp = 1.0 / jnp.sum(jnp.exp(lg - mg), axis=-1, keepdims=True)
    grp = gidx.astype(I32) - N_EXPERTS
    le = jnp.where((lane < N_EXPERTS) & ((lane // EXPERTS_PER_GROUP) == grp), logits, neg)
    m1, i1 = first_max(le)
    m2, i2 = first_max(jnp.where(lane_f == i1, neg, le))
    e2 = jnp.exp(m2 - m1)
    w1 = 1.0 / (1.0 + e2)
    return i1.astype(I32), i2.astype(I32), top_gp * w1, top_gp * (e2 * w1)


def _out_router_kernel(*refs, conv):
    if conv:
        (pb_ref, pc_ref, ph_ref, hc_ref, hh_ref, cw_ref, w_ref, hin_ref, fnw_ref, rhi_ref, rlo_ref,
         rb_ref, h_ref, hn_ref, id0_ref, id1_ref, g0_ref, g1_ref, cnt_ref) = refs
    else:
        (y_ref, w_ref, hin_ref, fnw_ref, rhi_ref, rlo_ref,
         rb_ref, h_ref, hn_ref, id0_ref, id1_ref, g0_ref, g1_ref, cnt_ref) = refs
    i = pl.program_id(0)

    if conv:
        u = pc_ref[...].astype(F32) * ph_ref[...].astype(F32)
        prev = hc_ref[...].astype(F32) * hh_ref[...].astype(F32)
        prev = jnp.where(i > 0, prev, 0.0)
        row = lax.broadcasted_iota(I32, u.shape, 0)
        last, last2 = prev[BF16_ROWS - 1:BF16_ROWS], prev[BF16_ROWS - 2:BF16_ROWS - 1]
        u1 = jnp.where(row == 0, last, pltpu.roll(u, 1, 0))
        u2 = jnp.where(row == 0, last2, jnp.where(row == 1, last, pltpu.roll(u, 2, 0)))
        cw = cw_ref[...]
        y = (pb_ref[...].astype(F32) * (cw[0:1] * u2 + cw[1:2] * u1 + cw[2:3] * u)).astype(BF16)
    else:
        y = y_ref[...]

    h = hin_ref[...] + _dot(y, w_ref[...])
    h_ref[...] = h
    hn = _rms(h, fnw_ref[...])
    hn_ref[...] = hn
    i1, i2, g1, g2 = _route(hn, rhi_ref, rlo_ref, rb_ref)
    shape = id0_ref.shape
    id0_ref[...] = jnp.broadcast_to(i1, shape)
    id1_ref[...] = jnp.broadcast_to(i2, shape)
    g0_ref[...] = jnp.broadcast_to(g1, shape)
    g1_ref[...] = jnp.broadcast_to(g2, shape)

    @pl.when(i == 0)
    def _():
        cnt_ref[...] = jnp.zeros_like(cnt_ref)

    lane = lax.broadcasted_iota(I32, shape, 1)
    hit = jnp.where((lane == i1) | (lane == i2), 1.0, 0.0)
    cnt_ref[...] += jnp.sum(hit, axis=0, keepdims=True)


def out_router(y, w_out, h_in, ffn_norm_w, r_hi, r_lo, r_b, conv_w=None):
    T, D = h_in.shape
    K = w_out.shape[0]
    tm = _tile(T, 256)
    conv = conv_w is not None
    row = lambda i: (i, 0)
    fixed = lambda i: (0, 0)
    if conv:
        hb = tm // BF16_ROWS
        halo = lambda col: (lambda i: (jnp.maximum(i * hb - 1, 0), col))
        in_specs = [
            pl.BlockSpec((tm, K), lambda i: (i, 0)),
            pl.BlockSpec((tm, K), lambda i: (i, 1)),
            pl.BlockSpec((tm, K), lambda i: (i, 2)),
            pl.BlockSpec((BF16_ROWS, K), halo(1)),
            pl.BlockSpec((BF16_ROWS, K), halo(2)),
            pl.BlockSpec((SUBLANES, K), fixed),
        ]
        cw = jnp.zeros((SUBLANES, K), F32).at[:CONV_WIDTH].set(conv_w)
        args = [y, y, y, y, y, cw]
    else:
        in_specs = [pl.BlockSpec((tm, K), row)]
        args = [y]
    in_specs += [
        pl.BlockSpec((K, D), fixed),
        pl.BlockSpec((tm, D), row),
        pl.BlockSpec((1, D), fixed),
        pl.BlockSpec((D, LANES), fixed),
        pl.BlockSpec((D, LANES), fixed),
        pl.BlockSpec((1, LANES), fixed),
    ]
    args += [w_out, h_in, ffn_norm_w.reshape(1, D), r_hi, r_lo, r_b]
    wide = lambda dt: jax.ShapeDtypeStruct((T, LANES), dt)
    return pl.pallas_call(
        functools.partial(_out_router_kernel, conv=conv),
        grid=(T // tm,),
        in_specs=in_specs,
        out_specs=[pl.BlockSpec((tm, D), row), pl.BlockSpec((tm, D), row)]
        + [pl.BlockSpec((tm, LANES), row)] * 4 + [pl.BlockSpec((1, LANES), fixed)],
        out_shape=[jax.ShapeDtypeStruct((T, D), F32), jax.ShapeDtypeStruct((T, D), F32),
                   wide(I32), wide(I32), wide(F32), wide(F32), jax.ShapeDtypeStruct((1, LANES), F32)],
        compiler_params=_params("arbitrary"),
        name="out_router_conv" if conv else "out_router",
    )(*args)


def _rank_kernel(id0_ref, id1_ref, pst_ref, d0_ref, d1_ref, carry_ref):
    @pl.when(pl.program_id(0) == 0)
    def _():
        carry_ref[...] = jnp.zeros_like(carry_ref)

    tb = id0_ref.shape[0]
    lane = lax.broadcasted_iota(I32, (tb, LANES), 1)
    oh0 = lane == id0_ref[...]
    oh1 = lane == id1_ref[...]
    hit = jnp.where(oh0 | oh1, 1.0, 0.0)
    r = lax.broadcasted_iota(I32, (tb, tb), 0)
    c = lax.broadcasted_iota(I32, (tb, tb), 1)
    before = jnp.where(c < r, 1.0, 0.0).astype(BF16)
    base = _dot(before, hit.astype(BF16)) + carry_ref[...] + pst_ref[...]
    d0 = jnp.sum(jnp.where(oh0, base, 0.0), axis=-1, keepdims=True)
    d1 = jnp.sum(jnp.where(oh1, base, 0.0), axis=-1, keepdims=True)
    d0_ref[...] = jnp.broadcast_to(d0.astype(I32), (tb, LANES))
    d1_ref[...] = jnp.broadcast_to(d1.astype(I32), (tb, LANES))
    carry_ref[...] += jnp.sum(hit, axis=0, keepdims=True)


def rank(id0, id1, pstart_row):
    T = id0.shape[0]
    tb = _tile(T, 256)
    row = lambda i: (i, 0)
    return pl.pallas_call(
        _rank_kernel,
        grid=(T // tb,),
        in_specs=[pl.BlockSpec((tb, LANES), row), pl.BlockSpec((tb, LANES), row),
                  pl.BlockSpec((1, LANES), lambda i: (0, 0))],
        out_specs=[pl.BlockSpec((tb, LANES), row)] * 2,
        out_shape=[jax.ShapeDtypeStruct((T, LANES), I32)] * 2,
        scratch_shapes=[pltpu.VMEM((1, LANES), F32)],
        compiler_params=_params("arbitrary"),
        name="rank",
    )(id0, id1, pstart_row)


def _invert_kernel(d0_ref, d1_ref, rt_ref):
    def clear(p, carry):
        rt_ref[p] = 0
        return carry

    lax.fori_loop(0, rt_ref.shape[0], clear, 0, unroll=8)

    def place(t, carry):
        rt_ref[d0_ref[t]] = t
        rt_ref[d1_ref[t]] = t
        return carry

    lax.fori_loop(0, d0_ref.shape[0], place, 0, unroll=8)


def invert(d0, d1, n_rows):
    smem = pl.BlockSpec(memory_space=pltpu.SMEM)
    return pl.pallas_call(
        _invert_kernel,
        in_specs=[smem, smem],
        out_specs=smem,
        out_shape=jax.ShapeDtypeStruct((n_rows,), I32),
        name="invert",
    )(d0, d1)


def _row_copy(src_hbm, dst_ref, sem, src_row, dst_row):
    return pltpu.make_async_copy(src_hbm.at[pl.ds(src_row, 1), :], dst_ref.at[pl.ds(dst_row, 1), :], sem)


def _experts_kernel(bexp_ref, rtok_ref, nused_ref, hn_hbm, wgu_ref, wd_ref, y_ref, xbuf, wgu_bf, wd_bf, sem,
                    *, ff):
    i = pl.program_id(0)
    rb = y_ref.shape[0]
    slot = i % 2
    n_used = nused_ref[0]

    def gather(blk, slot, start):
        def body(r, carry):
            cp = _row_copy(hn_hbm, xbuf.at[slot], sem.at[slot], rtok_ref[blk * rb + r], r)
            if start:
                cp.start()
            else:
                cp.wait()
            return carry

        lax.fori_loop(0, rb, body, 0, unroll=8)

    @pl.when(i == 0)
    def _():
        gather(0, 0, True)

    @pl.when(i + 1 < n_used)
    def _():
        gather(i + 1, 1 - slot, True)

    @pl.when((i == 0) | (bexp_ref[i] != bexp_ref[jnp.maximum(i - 1, 0)]))
    def _():
        wgu_bf[...] = wgu_ref[0, 0].astype(BF16)
        wd_bf[...] = wd_ref[0, 0].astype(BF16)

    @pl.when(i < n_used)
    def _():
        gather(i, slot, False)
        x = xbuf[slot].astype(BF16)
        gu = _dot(x, wgu_bf[...])
        gt, up = gu[:, :ff], gu[:, ff:]
        act = (gt * jax.nn.sigmoid(gt) * up).astype(BF16)
        y_ref[...] = _dot(act, wd_bf[...])

    @pl.when(i >= n_used)
    def _():
        y_ref[...] = jnp.zeros_like(y_ref)


def experts(block_exp, row_tok, n_used, hn, w_gate_up, w_down, layer):
    T, D = hn.shape
    ff2 = w_gate_up.shape[-1]
    ff = ff2 // 2
    P = row_tok.shape[0]
    rb = ROW_BLOCK
    grid_spec = pltpu.PrefetchScalarGridSpec(
        num_scalar_prefetch=3,
        grid=(P // rb,),
        in_specs=[
            pl.BlockSpec(memory_space=pl.ANY),
            pl.BlockSpec((1, 1, D, ff2), lambda i, be, rt, nu: (layer, be[i], 0, 0)),
            pl.BlockSpec((1, 1, ff, D), lambda i, be, rt, nu: (layer, be[i], 0, 0)),
        ],
        out_specs=pl.BlockSpec((rb, D), lambda i, be, rt, nu: (i, 0)),
        scratch_shapes=[pltpu.VMEM((2, rb, D), F32), pltpu.VMEM((D, ff2), BF16), pltpu.VMEM((ff, D), BF16),
                        pltpu.SemaphoreType.DMA((2,))],
    )
    return pl.pallas_call(
        functools.partial(_experts_kernel, ff=ff),
        grid_spec=grid_spec,
        out_shape=jax.ShapeDtypeStruct((P, D), F32),
        compiler_params=_params("arbitrary"),
        name="experts",
    )(block_exp, row_tok, n_used, hn, w_gate_up, w_down)


def _combine_kernel(d0_ref, d1_ref, yb_hbm, h_ref, g0_ref, g1_ref, nw_ref, o_ref, buf, sem, *, final_norm):
    i = pl.program_id(0)
    n = pl.num_programs(0)
    tc = o_ref.shape[0]
    slot = i % 2

    def gather(blk, slot, start):
        def body(r, carry):
            t = blk * tc + r
            for k, d_ref in enumerate((d0_ref, d1_ref)):
                cp = _row_copy(yb_hbm, buf.at[slot, k], sem.at[slot], d_ref[t], r)
                if start:
                    cp.start()
                else:
                    cp.wait()
            return carry

        lax.fori_loop(0, tc, body, 0, unroll=8)

    @pl.when(i == 0)
    def _():
        gather(0, 0, True)

    @pl.when(i + 1 < n)
    def _():
        gather(i + 1, 1 - slot, True)

    gather(i, slot, False)
    reps = o_ref.shape[1] // LANES
    wide = lambda g_ref: jnp.concatenate([g_ref[...]] * reps, axis=1)
    out = h_ref[...] + wide(g0_ref) * buf[slot, 0] + wide(g1_ref) * buf[slot, 1]
    if final_norm:
        out = _rms(out, nw_ref[...])
    o_ref[...] = out


def combine(d0, d1, yb, h, g0, g1, norm_w=None):
    T, D = h.shape
    tc = _tile(T, 256)
    final_norm = norm_w is not None
    nw = (norm_w if final_norm else jnp.ones((D,), F32)).reshape(1, D)
    row = lambda i, a, b: (i, 0)
    grid_spec = pltpu.PrefetchScalarGridSpec(
        num_scalar_prefetch=2,
        grid=(T // tc,),
        in_specs=[
            pl.BlockSpec(memory_space=pl.ANY),
            pl.BlockSpec((tc, D), row),
            pl.BlockSpec((tc, LANES), row),
            pl.BlockSpec((tc, LANES), row),
            pl.BlockSpec((1, D), lambda i, a, b: (0, 0)),
        ],
        out_specs=pl.BlockSpec((tc, D), row),
        scratch_shapes=[pltpu.VMEM((2, TOP_K, tc, D), F32), pltpu.SemaphoreType.DMA((2,))],
    )
    return pl.pallas_call(
        functools.partial(_combine_kernel, final_norm=final_norm),
        grid_spec=grid_spec,
        out_shape=jax.ShapeDtypeStruct((T, D), F32),
        compiler_params=_params("arbitrary"),
        name="combine_norm" if final_norm else "combine",
    )(d0, d1, yb, h, g0, g1, nw)


def _pad_cols(w, n):
    return jnp.pad(w, ((0, 0), (0, n - w.shape[1])))


def moe(h, hn, id0, id1, g0, g1, counts, w_gate_up, w_down, layer, final_norm_w=None):
    T, D = h.shape
    rb = ROW_BLOCK
    n_blocks = -(-(T * TOP_K + N_EXPERTS * (rb - 1)) // rb)
    cnt = counts[0, :N_EXPERTS].astype(I32)
    padded = ((cnt + rb - 1) // rb) * rb
    pends = jnp.cumsum(padded)
    pstart_row = _pad_cols((pends - padded).astype(F32).reshape(1, N_EXPERTS), LANES)
    d0b, d1b = rank(id0, id1, pstart_row)
    d0, d1 = d0b[:, 0], d1b[:, 0]
    row_tok = invert(d0, d1, n_blocks * rb)
    block_start = jnp.arange(n_blocks, dtype=I32) * rb
    block_exp = jnp.minimum(jnp.sum(pends[None, :] <= block_start[:, None], axis=1), N_EXPERTS - 1).astype(I32)
    n_used = (pends[-1:] // rb).astype(I32)
    yb = experts(block_exp, row_tok, n_used, hn, w_gate_up, w_down, layer)
    return combine(d0, d1, yb, h, g0, g1, final_norm_w)


def _router_weights(w_group, b_group, w_router, b_router):
    w = _pad_cols(jnp.concatenate([w_router, w_group], axis=1), LANES)
    b = _pad_cols(jnp.concatenate([b_router, b_group]).reshape(1, -1), LANES)
    hi = w.astype(BF16)
    lo = (w - hi.astype(F32)).astype(BF16)
    return hi, lo, b


def kernel(x, mix_norm, gla_w_in, gla_w_gk2, gla_b_gk, gla_head_norm, gla_w_out, conv_w_in, conv_w,
           conv_w_out, ffn_norm, w_group, b_group, w_router, b_router, w_gate_up, w_down, final_norm):
    B, T, D = x.shape
    assert B == 1, "the recurrence state is carried across the whole row axis"
    h = x.reshape(T, D)
    kd, vd = D // 2, D
    dk, dv = kd // GLA_HEADS, vd // GLA_HEADS

    w_in = gla_w_in[0]
    n_main = 2 * kd + 2 * vd
    proj, gl = norm_proj(h, mix_norm[0], w_in[:, :n_main].astype(BF16),
                         _pad_cols(w_in[:, n_main:], LANES).astype(BF16))
    w_gk2 = jnp.pad(gla_w_gk2[0], ((0, LANES - GATE_RANK), (0, 0))).astype(BF16)
    o = gla(proj, gl, w_gk2, gla_b_gk[0].reshape(1, kd), gla_head_norm[0].reshape(1, dv), dk=dk, dv=dv)
    routed = out_router(o, gla_w_out[0].astype(BF16), h, ffn_norm[0],
                        *_router_weights(w_group[0], b_group[0], w_router[0], b_router[0]))
    h = moe(*routed, w_gate_up, w_down, 0)

    proj = norm_proj(h, mix_norm[1], conv_w_in[0].astype(BF16))
    routed = out_router(proj, conv_w_out[0].astype(BF16), h, ffn_norm[1],
                        *_router_weights(w_group[1], b_group[1], w_router[1], b_router[1]),
                        conv_w=conv_w[0])
    out = moe(*routed, w_gate_up, w_down, 1, final_norm_w=final_norm)
    return out.reshape(B, T, D)
```

```python
import functools

import jax
import jax.numpy as jnp
from jax import lax
from jax.experimental import pallas as pl
from jax.experimental.pallas import tpu as pltpu

F32 = jnp.float32
BF16 = jnp.bfloat16
I32 = jnp.int32

EPS = 1e-6
GLA_HEADS = 4
GATE_RANK = 16
GATE_NORMALIZER = 16.0
CONV_WIDTH = 3
N_GROUPS = 4
EXPERTS_PER_GROUP = 8
N_EXPERTS = N_GROUPS * EXPERTS_PER_GROUP
TOP_K = 2

LANES = 128
SUBLANES = 8
BF16_ROWS = 16
VMEM_LIMIT = 56 * 1024 * 1024

GLA_CHUNK = 64
GLA_SUB = 16
GLA_GROUP = 128
ROW_BLOCK = 256


def _tile(n, pref):
    t = min(n, pref)
    assert n % t == 0, (n, t)
    return t


def _params(*sem):
    return pltpu.CompilerParams(dimension_semantics=sem, vmem_limit_bytes=VMEM_LIMIT)


def _dot(a, b):
    return jnp.dot(a, b, preferred_element_type=F32)


def _dot_nt(a, b):
    return lax.dot_general(a, b, (((1,), (1,)), ((), ())), preferred_element_type=F32)


def _dot_tn(a, b):
    return lax.dot_general(a, b, (((0,), (0,)), ((), ())), preferred_element_type=F32)


def _rms(x, w):
    return x * lax.rsqrt(jnp.mean(x * x, axis=-1, keepdims=True) + EPS) * w


def _norm_proj_kernel(*refs, with_extra):
    if with_extra:
        x_ref, nw_ref, w_ref, we_ref, o_ref, e_ref, xn_ref = refs
    else:
        x_ref, nw_ref, w_ref, o_ref, xn_ref = refs

    @pl.when(pl.program_id(1) == 0)
    def _():
        xn_ref[...] = _rms(x_ref[...], nw_ref[...]).astype(BF16)
        if with_extra:
            e_ref[...] = _dot(xn_ref[...], we_ref[...])

    o_ref[...] = _dot(xn_ref[...], w_ref[...]).astype(o_ref.dtype)


def norm_proj(x, norm_w, w, w_extra=None):
    T, D = x.shape
    N = w.shape[1]
    tm, tn = _tile(T, 1024), _tile(N, 1024)
    in_specs = [
        pl.BlockSpec((tm, D), lambda i, j: (i, 0)),
        pl.BlockSpec((1, D), lambda i, j: (0, 0)),
        pl.BlockSpec((D, tn), lambda i, j: (0, j)),
    ]
    out_shape = [jax.ShapeDtypeStruct((T, N), BF16)]
    out_specs = [pl.BlockSpec((tm, tn), lambda i, j: (i, j))]
    args = [x, norm_w.reshape(1, D), w]
    if w_extra is not None:
        in_specs.append(pl.BlockSpec((D, LANES), lambda i, j: (0, 0)))
        out_shape.append(jax.ShapeDtypeStruct((T, LANES), F32))
        out_specs.append(pl.BlockSpec((tm, LANES), lambda i, j: (i, 0)))
        args.append(w_extra)
    out = pl.pallas_call(
        functools.partial(_norm_proj_kernel, with_extra=w_extra is not None),
        grid=(T // tm, N // tn),
        in_specs=in_specs,
        out_specs=out_specs,
        out_shape=out_shape,
        scratch_shapes=[pltpu.VMEM((tm, D), BF16)],
        compiler_params=_params("arbitrary", "arbitrary"),
        name="norm_proj",
    )(*args)
    return out if w_extra is not None else out[0]


def _gla_kernel(q_ref, k_ref, v_ref, g_ref, gl_ref, wgk_ref, bgk_ref, hnorm_ref, o_ref,
                st_ref, bs_ref, qt_ref, kh_ref, bref_ref, *, dk, tb):
    C, SUB, GRP = GLA_CHUNK, GLA_SUB, GLA_GROUP
    NS = C // SUB
    CPG = GRP // C
    scale = dk ** -0.5

    @pl.when(pl.program_id(1) == 0)
    def _():
        st_ref[...] = jnp.zeros_like(st_ref)

    r = lax.broadcasted_iota(I32, (GRP, GRP), 0)
    c = lax.broadcasted_iota(I32, (GRP, GRP), 1)
    same = (r // SUB) == (c // SUB)
    low = jnp.where(same & (c <= r), 1.0, 0.0).astype(BF16)
    upp = jnp.where(same & (c > r), 1.0, 0.0).astype(BF16)
    rr = lax.broadcasted_iota(I32, (CPG * SUBLANES, GRP), 0)
    cc = lax.broadcasted_iota(I32, (CPG * SUBLANES, GRP), 1)
    blk = rr % SUBLANES
    mref = jnp.where(((cc // C) == (rr // SUBLANES)) & ((cc % C) < SUB * blk) & (blk <= NS),
                     1.0, 0.0).astype(BF16)

    for gi in range(tb // GRP):
        sl = pl.ds(gi * GRP, GRP)
        z = _dot(gl_ref[sl, :].astype(BF16), wgk_ref[...]) + bgk_ref[...]
        la = (jnp.minimum(z, 0.0) - jnp.log(1.0 + jnp.exp(-jnp.abs(z)))) * (1.0 / GATE_NORMALIZER)
        hi = la.astype(BF16)
        lo = (la - hi.astype(F32)).astype(BF16)
        bs = _dot(low, hi) + _dot(low, lo)
        ru = _dot(upp, hi) + _dot(upp, lo)
        br = _dot(mref, hi) + _dot(mref, lo)
        bs_ref[sl, :] = bs
        qt_ref[sl, :] = q_ref[sl, :].astype(F32) * jnp.exp(bs) * scale
        kh_ref[sl, :] = k_ref[sl, :].astype(F32) * jnp.exp(ru)
        bref_ref[pl.ds(gi * CPG, CPG)] = br.reshape(CPG, SUBLANES, dk)

    lane = lax.broadcasted_iota(I32, (SUB, C), 1)
    subrow = lax.broadcasted_iota(I32, (SUB, dk), 0)

    def chunk(n, carry):
        r0 = pl.multiple_of(n * C, C)
        rows = pl.ds(r0, C)
        bn = bref_ref[n]
        eb = jnp.exp(bn)
        st = st_ref[...]
        qt = qt_ref[rows, :]
        kh = kh_ref[rows, :]
        bs = bs_ref[rows, :]
        v = v_ref[rows, :]
        qraw = q_ref[rows, :].astype(F32) * scale
        kraw = k_ref[rows, :].astype(F32)

        qin = jnp.concatenate([qt[I * SUB:(I + 1) * SUB] * eb[I:I + 1] for I in range(NS)], axis=0)
        o = _dot_nt(qin.astype(BF16), st.astype(BF16))

        a_rows = []
        for I in range(NS):
            blk_rows = slice(I * SUB, (I + 1) * SUB)
            q_blk, k_blk, bs_blk = qraw[blk_rows], kraw[blk_rows], bs[blk_rows]
            acc = jnp.zeros((SUB, C), F32)
            for j in range(SUB):
                e = jnp.where(subrow >= j, jnp.exp(bs_blk - bs_blk[j:j + 1]), 0.0)
                p = (q_blk * k_blk[j:j + 1]) * e
                acc = jnp.where(lane == I * SUB + j, jnp.sum(p, axis=-1, keepdims=True), acc)
            if I > 0:
                parts = []
                for J in range(NS):
                    if J < I:
                        parts.append(kh[J * SUB:(J + 1) * SUB] * jnp.exp(bn[I:I + 1] - bn[J + 1:J + 2]))
                    else:
                        parts.append(jnp.zeros((SUB, dk), F32))
                k_dec = jnp.concatenate(parts, axis=0)
                acc = acc + _dot_nt(qt[blk_rows].astype(BF16), k_dec.astype(BF16))
            a_rows.append(acc)
        a = jnp.concatenate(a_rows, axis=0)
        o = o + _dot(a.astype(BF16), v)

        k_st = jnp.concatenate(
            [kh[J * SUB:(J + 1) * SUB] * jnp.exp(bn[NS:NS + 1] - bn[J + 1:J + 2]) for J in range(NS)],
            axis=0)
        st_ref[...] = st * eb[NS:NS + 1] + _dot_tn(v, k_st.astype(BF16))

        g = g_ref[rows, :].astype(F32)
        o_ref[rows, :] = (_rms(o, hnorm_ref[...]) * (g * jax.nn.sigmoid(g))).astype(o_ref.dtype)
        return carry

    lax.fori_loop(0, tb // C, chunk, 0)


def gla(proj, gl, w_gk2, b_gk, head_norm, *, dk, dv):
    T = proj.shape[0]
    H = GLA_HEADS
    tb = _tile(T, 512)
    assert tb % GLA_GROUP == 0 and dk % LANES == 0 and dv % LANES == 0
    kd, vd = H * dk, H * dv
    k_off, v_off, g_off = kd // dk, (2 * kd) // dv, (2 * kd + vd) // dv
    return pl.pallas_call(
        functools.partial(_gla_kernel, dk=dk, tb=tb),
        grid=(H, T // tb),
        in_specs=[
            pl.BlockSpec((tb, dk), lambda h, t: (t, h)),
            pl.BlockSpec((tb, dk), lambda h, t: (t, k_off + h)),
            pl.BlockSpec((tb, dv), lambda h, t: (t, v_off + h)),
            pl.BlockSpec((tb, dv), lambda h, t: (t, g_off + h)),
            pl.BlockSpec((tb, LANES), lambda h, t: (t, 0)),
            pl.BlockSpec((LANES, dk), lambda h, t: (0, h)),
            pl.BlockSpec((1, dk), lambda h, t: (0, h)),
            pl.BlockSpec((1, dv), lambda h, t: (0, 0)),
        ],
        out_specs=pl.BlockSpec((tb, dv), lambda h, t: (t, h)),
        out_shape=jax.ShapeDtypeStruct((T, vd), BF16),
        scratch_shapes=[
            pltpu.VMEM((dv, dk), F32),
            pltpu.VMEM((tb, dk), F32),
            pltpu.VMEM((tb, dk), F32),
            pltpu.VMEM((tb, dk), F32),
            pltpu.VMEM((tb // GLA_CHUNK, SUBLANES, dk), F32),
        ],
        compiler_params=_params("arbitrary", "arbitrary"),
        name="gla",
    )(proj, proj, proj, proj, gl, w_gk2, b_gk, head_norm)


def _route(hn, rhi_ref, rlo_ref, rb_ref):
    tm = hn.shape[0]
    hi = hn.astype(BF16)
    lo = (hn - hi.astype(F32)).astype(BF16)
    logits = _dot(hi, rhi_ref[...]) + _dot(hi, rlo_ref[...]) + _dot(lo, rhi_ref[...]) + rb_ref[...]
    lane = lax.broadcasted_iota(I32, (tm, LANES), 1)
    lane_f = lane.astype(F32)
    neg = -jnp.inf
    far = float(LANES)

    def first_max(vals):
        m = jnp.max(vals, axis=-1, keepdims=True)
        idx = jnp.min(jnp.where(vals == m, lane_f, far), axis=-1, keepdims=True)
        return m, idx

    lg = jnp.where((lane >= N_EXPERTS) & (lane < N_EXPERTS + N_GROUPS), logits, neg)
    mg, gidx = first_max(lg)
    top_gp = 1.0 / jnp.sum(jnp.exp(lg - mg), axis=-1, keepdims=True)
    grp = gidx.astype(I32) - N_EXPERTS
    le = jnp.where((lane < N_EXPERTS) & ((lane // EXPERTS_PER_GROUP) == grp), logits, neg)
    m1, i1 = first_max(le)
    m2, i2 = first_max(jnp.where(lane_f == i1, neg, le))
    e2 = jnp.exp(m2 - m1)
    w1 = 1.0 / (1.0 + e2)
    return i1.astype(I32), i2.astype(I32), top_gp * w1, top_gp * (e2 * w1)


def _out_router_kernel(*refs, conv):
    if conv:
        (pb_ref, pc_ref, ph_ref, hc_ref, hh_ref, cw_ref, w_ref, hin_ref, fnw_ref, rhi_ref, rlo_ref,
         rb_ref, h_ref, hn_ref, id0_ref, id1_ref, g0_ref, g1_ref, cnt_ref) = refs
    else:
        (y_ref, w_ref, hin_ref, fnw_ref, rhi_ref, rlo_ref,
         rb_ref, h_ref, hn_ref, id0_ref, id1_ref, g0_ref, g1_ref, cnt_ref) = refs
    i = pl.program_id(0)

    if conv:
        u = pc_ref[...].astype(F32) * ph_ref[...].astype(F32)
        prev = hc_ref[...].astype(F32) * hh_ref[...].astype(F32)
        prev = jnp.where(i > 0, prev, 0.0)
        row = lax.broadcasted_iota(I32, u.shape, 0)
        last, last2 = prev[BF16_ROWS - 1:BF16_ROWS], prev[BF16_ROWS - 2:BF16_ROWS - 1]
        u1 = jnp.where(row == 0, last, pltpu.roll(u, 1, 0))
        u2 = jnp.where(row == 0, last2, jnp.where(row == 1, last, pltpu.roll(u, 2, 0)))
        cw = cw_ref[...]
        y = (pb_ref[...].astype(F32) * (cw[0:1] * u2 + cw[1:2] * u1 + cw[2:3] * u)).astype(BF16)
    else:
        y = y_ref[...]

    h = hin_ref[...] + _dot(y, w_ref[...])
    h_ref[...] = h
    hn = _rms(h, fnw_ref[...])
    hn_ref[...] = hn
    i1, i2, g1, g2 = _route(hn, rhi_ref, rlo_ref, rb_ref)
    shape = id0_ref.shape
    id0_ref[...] = jnp.broadcast_to(i1, shape)
    id1_ref[...] = jnp.broadcast_to(i2, shape)
    g0_ref[...] = jnp.broadcast_to(g1, shape)
    g1_ref[...] = jnp.broadcast_to(g2, shape)

    @pl.when(i == 0)
    def _():
        cnt_ref[...] = jnp.zeros_like(cnt_ref)

    lane = lax.broadcasted_iota(I32, shape, 1)
    hit = jnp.where((lane == i1) | (lane == i2), 1.0, 0.0)
    cnt_ref[...] += jnp.sum(hit, axis=0, keepdims=True)


def out_router(y, w_out, h_in, ffn_norm_w, r_hi, r_lo, r_b, conv_w=None):
    T, D = h_in.shape
    K = w_out.shape[0]
    tm = _tile(T, 256)
    conv = conv_w is not None
    row = lambda i: (i, 0)
    fixed = lambda i: (0, 0)
    if conv:
        hb = tm // BF16_ROWS
        halo = lambda col: (lambda i: (jnp.maximum(i * hb - 1, 0), col))
        in_specs = [
            pl.BlockSpec((tm, K), lambda i: (i, 0)),
            pl.BlockSpec((tm, K), lambda i: (i, 1)),
            pl.BlockSpec((tm, K), lambda i: (i, 2)),
            pl.BlockSpec((BF16_ROWS, K), halo(1)),
            pl.BlockSpec((BF16_ROWS, K), halo(2)),
            pl.BlockSpec((SUBLANES, K), fixed),
        ]
        cw = jnp.zeros((SUBLANES, K), F32).at[:CONV_WIDTH].set(conv_w)
        args = [y, y, y, y, y, cw]
    else:
        in_specs = [pl.BlockSpec((tm, K), row)]
        args = [y]
    in_specs += [
        pl.BlockSpec((K, D), fixed),
        pl.BlockSpec((tm, D), row),
        pl.BlockSpec((1, D), fixed),
        pl.BlockSpec((D, LANES), fixed),
        pl.BlockSpec((D, LANES), fixed),
        pl.BlockSpec((1, LANES), fixed),
    ]
    args += [w_out, h_in, ffn_norm_w.reshape(1, D), r_hi, r_lo, r_b]
    wide = lambda dt: jax.ShapeDtypeStruct((T, LANES), dt)
    return pl.pallas_call(
        functools.partial(_out_router_kernel, conv=conv),
        grid=(T // tm,),
        in_specs=in_specs,
        out_specs=[pl.BlockSpec((tm, D), row), pl.BlockSpec((tm, D), row)]
        + [pl.BlockSpec((tm, LANES), row)] * 4 + [pl.BlockSpec((1, LANES), fixed)],
        out_shape=[jax.ShapeDtypeStruct((T, D), F32), jax.ShapeDtypeStruct((T, D), F32),
                   wide(I32), wide(I32), wide(F32), wide(F32), jax.ShapeDtypeStruct((1, LANES), F32)],
        compiler_params=_params("arbitrary"),
        name="out_router_conv" if conv else "out_router",
    )(*args)


def _rank_kernel(id0_ref, id1_ref, pst_ref, d0_ref, d1_ref, carry_ref):
    @pl.when(pl.program_id(0) == 0)
    def _():
        carry_ref[...] = jnp.zeros_like(carry_ref)

    tb = id0_ref.shape[0]
    lane = lax.broadcasted_iota(I32, (tb, LANES), 1)
    oh0 = lane == id0_ref[...]
    oh1 = lane == id1_ref[...]
    hit = jnp.where(oh0 | oh1, 1.0, 0.0)
    r = lax.broadcasted_iota(I32, (tb, tb), 0)
    c = lax.broadcasted_iota(I32, (tb, tb), 1)
    before = jnp.where(c < r, 1.0, 0.0).astype(BF16)
    base = _dot(before, hit.astype(BF16)) + carry_ref[...] + pst_ref[...]
    d0 = jnp.sum(jnp.where(oh0, base, 0.0), axis=-1, keepdims=True)
    d1 = jnp.sum(jnp.where(oh1, base, 0.0), axis=-1, keepdims=True)
    d0_ref[...] = jnp.broadcast_to(d0.astype(I32), (tb, LANES))
    d1_ref[...] = jnp.broadcast_to(d1.astype(I32), (tb, LANES))
    carry_ref[...] += jnp.sum(hit, axis=0, keepdims=True)


def rank(id0, id1, pstart_row):
    T = id0.shape[0]
    tb = _tile(T, 256)
    row = lambda i: (i, 0)
    return pl.pallas_call(
        _rank_kernel,
        grid=(T // tb,),
        in_specs=[pl.BlockSpec((tb, LANES), row), pl.BlockSpec((tb, LANES), row),
                  pl.BlockSpec((1, LANES), lambda i: (0, 0))],
        out_specs=[pl.BlockSpec((tb, LANES), row)] * 2,
        out_shape=[jax.ShapeDtypeStruct((T, LANES), I32)] * 2,
        scratch_shapes=[pltpu.VMEM((1, LANES), F32)],
        compiler_params=_params("arbitrary"),
        name="rank",
    )(id0, id1, pstart_row)


def _invert_kernel(d0_ref, d1_ref, rt_ref):
    def clear(p, carry):
        rt_ref[p] = 0
        return carry

    lax.fori_loop(0, rt_ref.shape[0], clear, 0, unroll=8)

    def place(t, carry):
        rt_ref[d0_ref[t]] = t
        rt_ref[d1_ref[t]] = t
        return carry

    lax.fori_loop(0, d0_ref.shape[0], place, 0, unroll=8)


def invert(d0, d1, n_rows):
    smem = pl.BlockSpec(memory_space=pltpu.SMEM)
    return pl.pallas_call(
        _invert_kernel,
        in_specs=[smem, smem],
        out_specs=smem,
        out_shape=jax.ShapeDtypeStruct((n_rows,), I32),
        name="invert",
    )(d0, d1)


def _row_copy(src_hbm, dst_ref, sem, src_row, dst_row):
    return pltpu.make_async_copy(src_hbm.at[pl.ds(src_row, 1), :], dst_ref.at[pl.ds(dst_row, 1), :], sem)


def _experts_kernel(bexp_ref, rtok_ref, nused_ref, hn_hbm, wgu_ref, wd_ref, y_ref, xbuf0, xbuf1, wgu_bf, wd_bf,
                    sem, *, ff):
    i = pl.program_id(0)
    rb = y_ref.shape[0]
    n_used = nused_ref[0]
    bufs = (xbuf0, xbuf1)

    def wait_rows(blk, s):
        def body(r, carry):
            _row_copy(hn_hbm, bufs[s], sem.at[s], rtok_ref[blk * rb + r], r).wait()
            return carry

        lax.fori_loop(0, rb, body, 0, unroll=8)

    def start_rows(blk, s):
        for r in range(rb):
            _row_copy(hn_hbm, bufs[s], sem.at[s], rtok_ref[blk * rb + r], r).start()

    @pl.when(i == 0)
    def _():
        start_rows(0, 0)

    @pl.when((i == 0) | (bexp_ref[i] != bexp_ref[jnp.maximum(i - 1, 0)]))
    def _():
        wgu_bf[...] = wgu_ref[0, 0].astype(BF16)
        wd_bf[...] = wd_ref[0, 0].astype(BF16)

    for s in range(2):
        @pl.when((i < n_used) & (i % 2 == s))
        def _():
            wait_rows(i, s)
            start_rows(i + 1, 1 - s)
            x = bufs[s][...].astype(BF16)
            gu = _dot(x, wgu_bf[...])
            gt, up = gu[:, :ff], gu[:, ff:]
            act = (gt * jax.nn.sigmoid(gt) * up).astype(BF16)
            y_ref[...] = _dot(act, wd_bf[...])

        @pl.when((i == n_used) & (i % 2 == s))
        def _():
            wait_rows(i, s)

    @pl.when(i >= n_used)
    def _():
        y_ref[...] = jnp.zeros_like(y_ref)


def experts(block_exp, row_tok, n_used, hn, w_gate_up, w_down, layer):
    T, D = hn.shape
    ff2 = w_gate_up.shape[-1]
    ff = ff2 // 2
    P = row_tok.shape[0]
    rb = ROW_BLOCK
    grid_spec = pltpu.PrefetchScalarGridSpec(
        num_scalar_prefetch=3,
        grid=(P // rb,),
        in_specs=[
            pl.BlockSpec(memory_space=pl.ANY),
            pl.BlockSpec((1, 1, D, ff2), lambda i, be, rt, nu: (layer, be[i], 0, 0)),
            pl.BlockSpec((1, 1, ff, D), lambda i, be, rt, nu: (layer, be[i], 0, 0)),
        ],
        out_specs=pl.BlockSpec((rb, D), lambda i, be, rt, nu: (i, 0)),
        scratch_shapes=[pltpu.VMEM((rb, D), F32), pltpu.VMEM((rb, D), F32), pltpu.VMEM((D, ff2), BF16),
                        pltpu.VMEM((ff, D), BF16), pltpu.SemaphoreType.DMA((2,))],
    )
    return pl.pallas_call(
        functools.partial(_experts_kernel, ff=ff),
        grid_spec=grid_spec,
        out_shape=jax.ShapeDtypeStruct((P, D), F32),
        compiler_params=_params("arbitrary"),
        name="experts",
    )(block_exp, row_tok, n_used, hn, w_gate_up, w_down)


def _combine_kernel(d0_ref, d1_ref, yb_hbm, h_ref, g0_ref, g1_ref, nw_ref, o_ref, buf, sem, *, final_norm):
    i = pl.program_id(0)
    n = pl.num_programs(0)
    tc = o_ref.shape[0]
    slot = i % 2

    def gather(blk, slot, start):
        def body(r, carry):
            t = blk * tc + r
            for k, d_ref in enumerate((d0_ref, d1_ref)):
                cp = _row_copy(yb_hbm, buf.at[slot, k], sem.at[slot], d_ref[t], r)
                if start:
                    cp.start()
                else:
                    cp.wait()
            return carry

        lax.fori_loop(0, tc, body, 0, unroll=8)

    @pl.when(i == 0)
    def _():
        gather(0, 0, True)

    @pl.when(i + 1 < n)
    def _():
        gather(i + 1, 1 - slot, True)

    gather(i, slot, False)
    reps = o_ref.shape[1] // LANES
    wide = lambda g_ref: jnp.concatenate([g_ref[...]] * reps, axis=1)
    out = h_ref[...] + wide(g0_ref) * buf[slot, 0] + wide(g1_ref) * buf[slot, 1]
    if final_norm:
        out = _rms(out, nw_ref[...])
    o_ref[...] = out


def combine(d0, d1, yb, h, g0, g1, norm_w=None):
    T, D = h.shape
    tc = _tile(T, 256)
    final_norm = norm_w is not None
    nw = (norm_w if final_norm else jnp.ones((D,), F32)).reshape(1, D)
    row = lambda i, a, b: (i, 0)
    grid_spec = pltpu.PrefetchScalarGridSpec(
        num_scalar_prefetch=2,
        grid=(T // tc,),
        in_specs=[
            pl.BlockSpec(memory_space=pl.ANY),
            pl.BlockSpec((tc, D), row),
            pl.BlockSpec((tc, LANES), row),
            pl.BlockSpec((tc, LANES), row),
            pl.BlockSpec((1, D), lambda i, a, b: (0, 0)),
        ],
        out_specs=pl.BlockSpec((tc, D), row),
        scratch_shapes=[pltpu.VMEM((2, TOP_K, tc, D), F32), pltpu.SemaphoreType.DMA((2,))],
    )
    return pl.pallas_call(
        functools.partial(_combine_kernel, final_norm=final_norm),
        grid_spec=grid_spec,
        out_shape=jax.ShapeDtypeStruct((T, D), F32),
        compiler_params=_params("arbitrary"),
        name="combine_norm" if final_norm else "combine",
    )(d0, d1, yb, h, g0, g1, nw)


def _pad_cols(w, n):
    return jnp.pad(w, ((0, 0), (0, n - w.shape[1])))


def moe(h, hn, id0, id1, g0, g1, counts, w_gate_up, w_down, layer, final_norm_w=None):
    T, D = h.shape
    rb = ROW_BLOCK
    n_blocks = -(-(T * TOP_K + N_EXPERTS * (rb - 1)) // rb) + 1
    cnt = counts[0, :N_EXPERTS].astype(I32)
    padded = ((cnt + rb - 1) // rb) * rb
    pends = jnp.cumsum(padded)
    pstart_row = _pad_cols((pends - padded).astype(F32).reshape(1, N_EXPERTS), LANES)
    d0b, d1b = rank(id0, id1, pstart_row)
    d0, d1 = d0b[:, 0], d1b[:, 0]
    row_tok = invert(d0, d1, n_blocks * rb)
    block_start = jnp.arange(n_blocks, dtype=I32) * rb
    block_exp = jnp.minimum(jnp.sum(pends[None, :] <= block_start[:, None], axis=1), N_EXPERTS - 1).astype(I32)
    n_used = (pends[-1:] // rb).astype(I32)
    yb = experts(block_exp, row_tok, n_used, hn, w_gate_up, w_down, layer)
    return combine(d0, d1, yb, h, g0, g1, final_norm_w)


def _router_weights(w_group, b_group, w_router, b_router):
    w = _pad_cols(jnp.concatenate([w_router, w_group], axis=1), LANES)
    b = _pad_cols(jnp.concatenate([b_router, b_group]).reshape(1, -1), LANES)
    hi = w.astype(BF16)
    lo = (w - hi.astype(F32)).astype(BF16)
    return hi, lo, b


def kernel(x, mix_norm, gla_w_in, gla_w_gk2, gla_b_gk, gla_head_norm, gla_w_out, conv_w_in, conv_w,
           conv_w_out, ffn_norm, w_group, b_group, w_router, b_router, w_gate_up, w_down, final_norm):
    B, T, D = x.shape
    assert B == 1, "the recurrence state is carried across the whole row axis"
    h = x.reshape(T, D)
    kd, vd = D // 2, D
    dk, dv = kd // GLA_HEADS, vd // GLA_HEADS

    w_in = gla_w_in[0]
    n_main = 2 * kd + 2 * vd
    proj, gl = norm_proj(h, mix_norm[0], w_in[:, :n_main].astype(BF16),
                         _pad_cols(w_in[:, n_main:], LANES).astype(BF16))
    w_gk2 = jnp.pad(gla_w_gk2[0], ((0, LANES - GATE_RANK), (0, 0))).astype(BF16)
    o = gla(proj, gl, w_gk2, gla_b_gk[0].reshape(1, kd), gla_head_norm[0].reshape(1, dv), dk=dk, dv=dv)
    routed = out_router(o, gla_w_out[0].astype(BF16), h, ffn_norm[0],
                        *_router_weights(w_group[0], b_group[0], w_router[0], b_router[0]))
    h = moe(*routed, w_gate_up, w_down, 0)

    proj = norm_proj(h, mix_norm[1], conv_w_in[0].astype(BF16))
    routed = out_router(proj, conv_w_out[0].astype(BF16), h, ffn_norm[1],
                        *_router_weights(w_group[1], b_group[1], w_router[1], b_router[1]),
                        conv_w=conv_w[0])
    out = moe(*routed, w_gate_up, w_down, 1, final_norm_w=final_norm)
    return out.reshape(B, T, D)
```

```python
import functools

import jax
import jax.numpy as jnp
from jax import lax
from jax.experimental import pallas as pl
from jax.experimental.pallas import tpu as pltpu

F32 = jnp.float32
BF16 = jnp.bfloat16
I32 = jnp.int32

EPS = 1e-6
GLA_HEADS = 4
GATE_RANK = 16
GATE_NORMALIZER = 16.0
CONV_WIDTH = 3
N_GROUPS = 4
EXPERTS_PER_GROUP = 8
N_EXPERTS = N_GROUPS * EXPERTS_PER_GROUP
TOP_K = 2

LANES = 128
SUBLANES = 8
BF16_ROWS = 16
VMEM_LIMIT = 56 * 1024 * 1024

GLA_CHUNK = 64
GLA_SUB = 16
GLA_GROUP = 128
GLA_SAFE_DECAY = 60.0
ROW_BLOCK = 256
GATHER_AHEAD = 2


def _tile(n, pref):
    t = min(n, pref)
    assert n % t == 0, (n, t)
    return t


def _params(*sem):
    return pltpu.CompilerParams(dimension_semantics=sem, vmem_limit_bytes=VMEM_LIMIT)


def _dot(a, b):
    return jnp.dot(a, b, preferred_element_type=F32)


def _dot_nt(a, b):
    return lax.dot_general(a, b, (((1,), (1,)), ((), ())), preferred_element_type=F32)


def _dot_tn(a, b):
    return lax.dot_general(a, b, (((0,), (0,)), ((), ())), preferred_element_type=F32)


def _rms(x, w):
    return x * lax.rsqrt(jnp.mean(x * x, axis=-1, keepdims=True) + EPS) * w


def _norm_proj_kernel(*refs, with_extra):
    if with_extra:
        x_ref, nw_ref, w_ref, we_ref, o_ref, e_ref, xn_ref = refs
    else:
        x_ref, nw_ref, w_ref, o_ref, xn_ref = refs

    @pl.when(pl.program_id(1) == 0)
    def _():
        xn_ref[...] = _rms(x_ref[...], nw_ref[...]).astype(BF16)
        if with_extra:
            e_ref[...] = _dot(xn_ref[...], we_ref[...])

    o_ref[...] = _dot(xn_ref[...], w_ref[...]).astype(o_ref.dtype)


def norm_proj(x, norm_w, w, w_extra=None):
    T, D = x.shape
    N = w.shape[1]
    tm, tn = _tile(T, 1024), _tile(N, 1024)
    in_specs = [
        pl.BlockSpec((tm, D), lambda i, j: (i, 0)),
        pl.BlockSpec((1, D), lambda i, j: (0, 0)),
        pl.BlockSpec((D, tn), lambda i, j: (0, j)),
    ]
    out_shape = [jax.ShapeDtypeStruct((T, N), BF16)]
    out_specs = [pl.BlockSpec((tm, tn), lambda i, j: (i, j))]
    args = [x, norm_w.reshape(1, D), w]
    if w_extra is not None:
        in_specs.append(pl.BlockSpec((D, LANES), lambda i, j: (0, 0)))
        out_shape.append(jax.ShapeDtypeStruct((T, LANES), F32))
        out_specs.append(pl.BlockSpec((tm, LANES), lambda i, j: (i, 0)))
        args.append(w_extra)
    out = pl.pallas_call(
        functools.partial(_norm_proj_kernel, with_extra=w_extra is not None),
        grid=(T // tm, N // tn),
        in_specs=in_specs,
        out_specs=out_specs,
        out_shape=out_shape,
        scratch_shapes=[pltpu.VMEM((tm, D), BF16)],
        compiler_params=_params("arbitrary", "arbitrary"),
        name="norm_proj",
    )(*args)
    return out if w_extra is not None else out[0]


def _gla_kernel(q_ref, k_ref, v_ref, g_ref, gl_ref, wgk_ref, bgk_ref, hnorm_ref, o_ref,
                st_ref, bs_ref, qt_ref, kh_ref, bref_ref, *, dk, dv, tb):
    C, SUB, GRP, H = GLA_CHUNK, GLA_SUB, GLA_GROUP, GLA_HEADS
    NS = C // SUB
    CPG = GRP // C
    kd = H * dk
    scale = dk ** -0.5

    @pl.when(pl.program_id(0) == 0)
    def _():
        st_ref[...] = jnp.zeros_like(st_ref)

    r = lax.broadcasted_iota(I32, (GRP, GRP), 0)
    c = lax.broadcasted_iota(I32, (GRP, GRP), 1)
    same = (r // SUB) == (c // SUB)
    low = jnp.where(same & (c <= r), 1.0, 0.0).astype(BF16)
    upp = jnp.where(same & (c > r), 1.0, 0.0).astype(BF16)
    rr = lax.broadcasted_iota(I32, (CPG * SUBLANES, GRP), 0)
    cc = lax.broadcasted_iota(I32, (CPG * SUBLANES, GRP), 1)
    blk = rr % SUBLANES
    mref = jnp.where(((cc // C) == (rr // SUBLANES)) & ((cc % C) < SUB * blk) & (blk <= NS),
                     1.0, 0.0).astype(BF16)

    for gi in range(tb // GRP):
        sl = pl.ds(gi * GRP, GRP)
        z = _dot(gl_ref[sl, :].astype(BF16), wgk_ref[...]) + bgk_ref[...]
        la = (jnp.minimum(z, 0.0) - jnp.log(1.0 + jnp.exp(-jnp.abs(z)))) * (1.0 / GATE_NORMALIZER)
        hi = la.astype(BF16)
        lo = (la - hi.astype(F32)).astype(BF16)
        bs = _dot(low, hi) + _dot(low, lo)
        ru = _dot(upp, hi) + _dot(upp, lo)
        br = _dot(mref, hi) + _dot(mref, lo)
        bs_ref[sl, :] = bs
        qt_ref[sl, :] = q_ref[sl, :].astype(F32) * jnp.exp(bs) * scale
        kh_ref[sl, :] = k_ref[sl, :].astype(F32) * jnp.exp(ru)
        bref_ref[pl.ds(gi * CPG, CPG)] = br.reshape(CPG, SUBLANES, kd)

    lane = lax.broadcasted_iota(I32, (SUB, C), 1)
    subrow = lax.broadcasted_iota(I32, (SUB, dk), 0)

    def scores_pivoted(qin, kraw, bs, bn):
        neg_b = jnp.concatenate([-(bs[J * SUB:(J + 1) * SUB] + bn[J:J + 1]) for J in range(NS)], axis=0)
        s = _dot_nt(qin.astype(BF16), (kraw * jnp.exp(neg_b)).astype(BF16))
        ri = lax.broadcasted_iota(I32, (C, C), 0)
        ci = lax.broadcasted_iota(I32, (C, C), 1)
        return jnp.where(ci <= ri, s, 0.0)

    def scores_stable(qt, kh, qraw, kraw, bs, bn):
        a_rows = []
        for I in range(NS):
            blk_rows = slice(I * SUB, (I + 1) * SUB)
            q_blk, k_blk, bs_blk = qraw[blk_rows], kraw[blk_rows], bs[blk_rows]
            acc = jnp.zeros((SUB, C), F32)
            for j in range(SUB):
                e = jnp.where(subrow >= j, jnp.exp(bs_blk - bs_blk[j:j + 1]), 0.0)
                p = (q_blk * k_blk[j:j + 1]) * e
                acc = jnp.where(lane == I * SUB + j, jnp.sum(p, axis=-1, keepdims=True), acc)
            if I > 0:
                parts = []
                for J in range(NS):
                    if J < I:
                        parts.append(kh[J * SUB:(J + 1) * SUB] * jnp.exp(bn[I:I + 1] - bn[J + 1:J + 2]))
                    else:
                        parts.append(jnp.zeros((SUB, dk), F32))
                k_dec = jnp.concatenate(parts, axis=0)
                acc = acc + _dot_nt(qt[blk_rows].astype(BF16), k_dec.astype(BF16))
            a_rows.append(acc)
        return jnp.concatenate(a_rows, axis=0)

    def chunk(n, carry, *, pivoted):
        r0 = pl.multiple_of(n * C, C)
        rows = pl.ds(r0, C)
        outs = []
        for h in range(H):
            kl = slice(h * dk, (h + 1) * dk)
            vl = slice(h * dv, (h + 1) * dv)
            bn = bref_ref[n, :, kl]
            eb = jnp.exp(bn)
            st = st_ref[h]
            qt = qt_ref[rows, kl]
            kh = kh_ref[rows, kl]
            bs = bs_ref[rows, kl]
            v = v_ref[rows, vl]
            kraw = k_ref[rows, kl].astype(F32)

            qin = jnp.concatenate([qt[I * SUB:(I + 1) * SUB] * eb[I:I + 1] for I in range(NS)], axis=0)
            o = _dot_nt(qin.astype(BF16), st.astype(BF16))
            if pivoted:
                a = scores_pivoted(qin, kraw, bs, bn)
            else:
                a = scores_stable(qt, kh, q_ref[rows, kl].astype(F32) * scale, kraw, bs, bn)
            o = o + _dot(a.astype(BF16), v)

            k_st = jnp.concatenate(
                [kh[J * SUB:(J + 1) * SUB] * jnp.exp(bn[NS:NS + 1] - bn[J + 1:J + 2]) for J in range(NS)],
                axis=0)
            st_ref[h] = st * eb[NS:NS + 1] + _dot_tn(v, k_st.astype(BF16))
            outs.append(_rms(o, hnorm_ref[...]))

        g = g_ref[rows, :].astype(F32)
        o_ref[rows, :] = (jnp.concatenate(outs, axis=1) * (g * jax.nn.sigmoid(g))).astype(o_ref.dtype)
        return carry

    span = jnp.max(-bref_ref[...])

    @pl.when(span < GLA_SAFE_DECAY)
    def _():
        lax.fori_loop(0, tb // C, functools.partial(chunk, pivoted=True), 0)

    @pl.when(jnp.logical_not(span < GLA_SAFE_DECAY))
    def _():
        lax.fori_loop(0, tb // C, functools.partial(chunk, pivoted=False), 0)


def gla(proj, gl, w_gk2, b_gk, head_norm, *, dk, dv):
    T = proj.shape[0]
    H = GLA_HEADS
    tb = _tile(T, 512)
    assert tb % GLA_GROUP == 0 and dk % LANES == 0 and dv % LANES == 0
    kd, vd = H * dk, H * dv
    assert (2 * kd) % vd == 0
    v_off = (2 * kd) // vd
    fixed = lambda t: (0, 0)
    return pl.pallas_call(
        functools.partial(_gla_kernel, dk=dk, dv=dv, tb=tb),
        grid=(T // tb,),
        in_specs=[
            pl.BlockSpec((tb, kd), lambda t: (t, 0)),
            pl.BlockSpec((tb, kd), lambda t: (t, 1)),
            pl.BlockSpec((tb, vd), lambda t: (t, v_off)),
            pl.BlockSpec((tb, vd), lambda t: (t, v_off + 1)),
            pl.BlockSpec((tb, LANES), lambda t: (t, 0)),
            pl.BlockSpec((LANES, kd), fixed),
            pl.BlockSpec((1, kd), fixed),
            pl.BlockSpec((1, dv), fixed),
        ],
        out_specs=pl.BlockSpec((tb, vd), lambda t: (t, 0)),
        out_shape=jax.ShapeDtypeStruct((T, vd), BF16),
        scratch_shapes=[
            pltpu.VMEM((H, dv, dk), F32),
            pltpu.VMEM((tb, kd), F32),
            pltpu.VMEM((tb, kd), F32),
            pltpu.VMEM((tb, kd), F32),
            pltpu.VMEM((tb // GLA_CHUNK, SUBLANES, kd), F32),
        ],
        compiler_params=_params("arbitrary"),
        name="gla",
    )(proj, proj, proj, proj, gl, w_gk2, b_gk, head_norm)


def _route(hn, rhi_ref, rlo_ref, rb_ref):
    tm = hn.shape[0]
    hi = hn.astype(BF16)
    lo = (hn - hi.astype(F32)).astype(BF16)
    logits = _dot(hi, rhi_ref[...]) + _dot(hi, rlo_ref[...]) + _dot(lo, rhi_ref[...]) + rb_ref[...]
    lane = lax.broadcasted_iota(I32, (tm, LANES), 1)
    lane_f = lane.astype(F32)
    neg = -jnp.inf
    far = float(LANES)

    def first_max(vals):
        m = jnp.max(vals, axis=-1, keepdims=True)
        idx = jnp.min(jnp.where(vals == m, lane_f, far), axis=-1, keepdims=True)
        return m, idx

    lg = jnp.where((lane >= N_EXPERTS) & (lane < N_EXPERTS + N_GROUPS), logits, neg)
    mg, gidx = first_max(lg)
    top_gp = 1.0 / jnp.sum(jnp.exp(lg - mg), axis=-1, keepdims=True)
    grp = gidx.astype(I32) - N_EXPERTS
    le = jnp.where((lane < N_EXPERTS) & ((lane // EXPERTS_PER_GROUP) == grp), logits, neg)
    m1, i1 = first_max(le)
    m2, i2 = first_max(jnp.where(lane_f == i1, neg, le))
    e2 = jnp.exp(m2 - m1)
    w1 = 1.0 / (1.0 + e2)
    return i1.astype(I32), i2.astype(I32), top_gp * w1, top_gp * (e2 * w1)


def _out_router_kernel(*refs, conv):
    if conv:
        (pb_ref, pc_ref, ph_ref, hc_ref, hh_ref, cw_ref, w_ref, hin_ref, fnw_ref, rhi_ref, rlo_ref,
         rb_ref, h_ref, hn_ref, id0_ref, id1_ref, g0_ref, g1_ref, cnt_ref) = refs
    else:
        (y_ref, w_ref, hin_ref, fnw_ref, rhi_ref, rlo_ref,
         rb_ref, h_ref, hn_ref, id0_ref, id1_ref, g0_ref, g1_ref, cnt_ref) = refs
    i = pl.program_id(0)

    if conv:
        u = pc_ref[...].astype(F32) * ph_ref[...].astype(F32)
        prev = hc_ref[...].astype(F32) * hh_ref[...].astype(F32)
        prev = jnp.where(i > 0, prev, 0.0)
        row = lax.broadcasted_iota(I32, u.shape, 0)
        last, last2 = prev[BF16_ROWS - 1:BF16_ROWS], prev[BF16_ROWS - 2:BF16_ROWS - 1]
        u1 = jnp.where(row == 0, last, pltpu.roll(u, 1, 0))
        u2 = jnp.where(row == 0, last2, jnp.where(row == 1, last, pltpu.roll(u, 2, 0)))
        cw = cw_ref[...]
        y = (pb_ref[...].astype(F32) * (cw[0:1] * u2 + cw[1:2] * u1 + cw[2:3] * u)).astype(BF16)
    else:
        y = y_ref[...]

    h = hin_ref[...] + _dot(y, w_ref[...])
    h_ref[...] = h
    hn = _rms(h, fnw_ref[...])
    hn_ref[...] = hn
    i1, i2, g1, g2 = _route(hn, rhi_ref, rlo_ref, rb_ref)
    shape = id0_ref.shape
    id0_ref[...] = jnp.broadcast_to(i1, shape)
    id1_ref[...] = jnp.broadcast_to(i2, shape)
    g0_ref[...] = jnp.broadcast_to(g1, shape)
    g1_ref[...] = jnp.broadcast_to(g2, shape)

    @pl.when(i == 0)
    def _():
        cnt_ref[...] = jnp.zeros_like(cnt_ref)

    lane = lax.broadcasted_iota(I32, shape, 1)
    hit = jnp.where((lane == i1) | (lane == i2), 1.0, 0.0)
    cnt_ref[...] += jnp.sum(hit, axis=0, keepdims=True)


def out_router(y, w_out, h_in, ffn_norm_w, r_hi, r_lo, r_b, conv_w=None):
    T, D = h_in.shape
    K = w_out.shape[0]
    tm = _tile(T, 256)
    conv = conv_w is not None
    row = lambda i: (i, 0)
    fixed = lambda i: (0, 0)
    if conv:
        hb = tm // BF16_ROWS
        halo = lambda col: (lambda i: (jnp.maximum(i * hb - 1, 0), col))
        in_specs = [
            pl.BlockSpec((tm, K), lambda i: (i, 0)),
            pl.BlockSpec((tm, K), lambda i: (i, 1)),
            pl.BlockSpec((tm, K), lambda i: (i, 2)),
            pl.BlockSpec((BF16_ROWS, K), halo(1)),
            pl.BlockSpec((BF16_ROWS, K), halo(2)),
            pl.BlockSpec((SUBLANES, K), fixed),
        ]
        cw = jnp.zeros((SUBLANES, K), F32).at[:CONV_WIDTH].set(conv_w)
        args = [y, y, y, y, y, cw]
    else:
        in_specs = [pl.BlockSpec((tm, K), row)]
        args = [y]
    in_specs += [
        pl.BlockSpec((K, D), fixed),
        pl.BlockSpec((tm, D), row),
        pl.BlockSpec((1, D), fixed),
        pl.BlockSpec((D, LANES), fixed),
        pl.BlockSpec((D, LANES), fixed),
        pl.BlockSpec((1, LANES), fixed),
    ]
    args += [w_out, h_in, ffn_norm_w.reshape(1, D), r_hi, r_lo, r_b]
    wide = lambda dt: jax.ShapeDtypeStruct((T, LANES), dt)
    return pl.pallas_call(
        functools.partial(_out_router_kernel, conv=conv),
        grid=(T // tm,),
        in_specs=in_specs,
        out_specs=[pl.BlockSpec((tm, D), row), pl.BlockSpec((tm, D), row)]
        + [pl.BlockSpec((tm, LANES), row)] * 4 + [pl.BlockSpec((1, LANES), fixed)],
        out_shape=[jax.ShapeDtypeStruct((T, D), F32), jax.ShapeDtypeStruct((T, D), F32),
                   wide(I32), wide(I32), wide(F32), wide(F32), jax.ShapeDtypeStruct((1, LANES), F32)],
        compiler_params=_params("arbitrary"),
        name="out_router_conv" if conv else "out_router",
    )(*args)


def _rank_kernel(id0_ref, id1_ref, pst_ref, d0_ref, d1_ref, carry_ref):
    @pl.when(pl.program_id(0) == 0)
    def _():
        carry_ref[...] = jnp.zeros_like(carry_ref)

    tb = id0_ref.shape[0]
    lane = lax.broadcasted_iota(I32, (tb, LANES), 1)
    oh0 = lane == id0_ref[...]
    oh1 = lane == id1_ref[...]
    hit = jnp.where(oh0 | oh1, 1.0, 0.0)
    r = lax.broadcasted_iota(I32, (tb, tb), 0)
    c = lax.broadcasted_iota(I32, (tb, tb), 1)
    before = jnp.where(c < r, 1.0, 0.0).astype(BF16)
    base = _dot(before, hit.astype(BF16)) + carry_ref[...] + pst_ref[...]
    d0 = jnp.sum(jnp.where(oh0, base, 0.0), axis=-1, keepdims=True)
    d1 = jnp.sum(jnp.where(oh1, base, 0.0), axis=-1, keepdims=True)
    d0_ref[...] = jnp.broadcast_to(d0.astype(I32), (tb, LANES))
    d1_ref[...] = jnp.broadcast_to(d1.astype(I32), (tb, LANES))
    carry_ref[...] += jnp.sum(hit, axis=0, keepdims=True)


def rank(id0, id1, pstart_row):
    T = id0.shape[0]
    tb = _tile(T, 256)
    row = lambda i: (i, 0)
    return pl.pallas_call(
        _rank_kernel,
        grid=(T // tb,),
        in_specs=[pl.BlockSpec((tb, LANES), row), pl.BlockSpec((tb, LANES), row),
                  pl.BlockSpec((1, LANES), lambda i: (0, 0))],
        out_specs=[pl.BlockSpec((tb, LANES), row)] * 2,
        out_shape=[jax.ShapeDtypeStruct((T, LANES), I32)] * 2,
        scratch_shapes=[pltpu.VMEM((1, LANES), F32)],
        compiler_params=_params("arbitrary"),
        name="rank",
    )(id0, id1, pstart_row)


def _invert_kernel(d0_ref, d1_ref, rt_ref):
    def clear(p, carry):
        rt_ref[p] = 0
        return carry

    lax.fori_loop(0, rt_ref.shape[0], clear, 0, unroll=8)

    def place(t, carry):
        rt_ref[d0_ref[t]] = t
        rt_ref[d1_ref[t]] = t
        return carry

    lax.fori_loop(0, d0_ref.shape[0], place, 0, unroll=8)


def invert(d0, d1, n_rows):
    smem = pl.BlockSpec(memory_space=pltpu.SMEM)
    return pl.pallas_call(
        _invert_kernel,
        in_specs=[smem, smem],
        out_specs=smem,
        out_shape=jax.ShapeDtypeStruct((n_rows,), I32),
        name="invert",
    )(d0, d1)


def _row_copy(src_hbm, dst_ref, sem, src_row, dst_row):
    return pltpu.make_async_copy(src_hbm.at[pl.ds(src_row, 1), :], dst_ref.at[pl.ds(dst_row, 1), :], sem)


def _experts_kernel(bexp_ref, rtok_ref, nused_ref, hn_hbm, wgu_ref, wd_ref, y_ref, xbuf0, xbuf1, xbuf2, wgu_bf,
                    wd_bf, sem, *, ff):
    i = pl.program_id(0)
    rb = y_ref.shape[0]
    n_used = nused_ref[0]
    bufs = (xbuf0, xbuf1, xbuf2)
    n_buf = len(bufs)
    assert n_buf == GATHER_AHEAD + 1

    def wait_rows(blk, s):
        def body(r, carry):
            _row_copy(hn_hbm, bufs[s], sem.at[s], rtok_ref[blk * rb + r], r).wait()
            return carry

        lax.fori_loop(0, rb, body, 0, unroll=8)

    def start_rows(blk, s):
        for r in range(rb):
            _row_copy(hn_hbm, bufs[s], sem.at[s], rtok_ref[blk * rb + r], r).start()

    @pl.when(i == 0)
    def _():
        for blk in range(GATHER_AHEAD):
            start_rows(blk, blk)

    @pl.when((i == 0) | (bexp_ref[i] != bexp_ref[jnp.maximum(i - 1, 0)]))
    def _():
        wgu_bf[...] = wgu_ref[0, 0].astype(BF16)
        wd_bf[...] = wd_ref[0, 0].astype(BF16)

    for s in range(n_buf):
        @pl.when((i < n_used) & (i % n_buf == s))
        def _():
            wait_rows(i, s)
            start_rows(i + GATHER_AHEAD, (s + GATHER_AHEAD) % n_buf)
            x = bufs[s][...].astype(BF16)
            gu = _dot(x, wgu_bf[...])
            gt, up = gu[:, :ff], gu[:, ff:]
            act = (gt * jax.nn.sigmoid(gt) * up).astype(BF16)
            y_ref[...] = _dot(act, wd_bf[...])

        @pl.when((i >= n_used) & (i < n_used + GATHER_AHEAD) & (i % n_buf == s))
        def _():
            wait_rows(i, s)

    @pl.when(i >= n_used)
    def _():
        y_ref[...] = jnp.zeros_like(y_ref)


def experts(block_exp, row_tok, n_used, hn, w_gate_up, w_down, layer):
    T, D = hn.shape
    ff2 = w_gate_up.shape[-1]
    ff = ff2 // 2
    P = row_tok.shape[0]
    rb = ROW_BLOCK
    grid_spec = pltpu.PrefetchScalarGridSpec(
        num_scalar_prefetch=3,
        grid=(P // rb,),
        in_specs=[
            pl.BlockSpec(memory_space=pl.ANY),
            pl.BlockSpec((1, 1, D, ff2), lambda i, be, rt, nu: (layer, be[i], 0, 0)),
            pl.BlockSpec((1, 1, ff, D), lambda i, be, rt, nu: (layer, be[i], 0, 0)),
        ],
        out_specs=pl.BlockSpec((rb, D), lambda i, be, rt, nu: (i, 0)),
        scratch_shapes=[pltpu.VMEM((rb, D), F32)] * (GATHER_AHEAD + 1)
        + [pltpu.VMEM((D, ff2), BF16), pltpu.VMEM((ff, D), BF16), pltpu.SemaphoreType.DMA((GATHER_AHEAD + 1,))],
    )
    return pl.pallas_call(
        functools.partial(_experts_kernel, ff=ff),
        grid_spec=grid_spec,
        out_shape=jax.ShapeDtypeStruct((P, D), F32),
        compiler_params=_params("arbitrary"),
        name="experts",
    )(block_exp, row_tok, n_used, hn, w_gate_up, w_down)


def _combine_kernel(d0_ref, d1_ref, yb_hbm, h_ref, g0_ref, g1_ref, nw_ref, o_ref, buf, sem, *, final_norm):
    i = pl.program_id(0)
    n = pl.num_programs(0)
    tc = o_ref.shape[0]
    slot = i % 2

    def gather(blk, slot, start):
        def body(r, carry):
            t = blk * tc + r
            for k, d_ref in enumerate((d0_ref, d1_ref)):
                cp = _row_copy(yb_hbm, buf.at[slot, k], sem.at[slot], d_ref[t], r)
                if start:
                    cp.start()
                else:
                    cp.wait()
            return carry

        lax.fori_loop(0, tc, body, 0, unroll=8)

    @pl.when(i == 0)
    def _():
        gather(0, 0, True)

    @pl.when(i + 1 < n)
    def _():
        gather(i + 1, 1 - slot, True)

    gather(i, slot, False)
    reps = o_ref.shape[1] // LANES
    wide = lambda g_ref: jnp.concatenate([g_ref[...]] * reps, axis=1)
    out = h_ref[...] + wide(g0_ref) * buf[slot, 0] + wide(g1_ref) * buf[slot, 1]
    if final_norm:
        out = _rms(out, nw_ref[...])
    o_ref[...] = out


def combine(d0, d1, yb, h, g0, g1, norm_w=None):
    T, D = h.shape
    tc = _tile(T, 256)
    final_norm = norm_w is not None
    nw = (norm_w if final_norm else jnp.ones((D,), F32)).reshape(1, D)
    row = lambda i, a, b: (i, 0)
    grid_spec = pltpu.PrefetchScalarGridSpec(
        num_scalar_prefetch=2,
        grid=(T // tc,),
        in_specs=[
            pl.BlockSpec(memory_space=pl.ANY),
            pl.BlockSpec((tc, D), row),
            pl.BlockSpec((tc, LANES), row),
            pl.BlockSpec((tc, LANES), row),
            pl.BlockSpec((1, D), lambda i, a, b: (0, 0)),
        ],
        out_specs=pl.BlockSpec((tc, D), row),
        scratch_shapes=[pltpu.VMEM((2, TOP_K, tc, D), F32), pltpu.SemaphoreType.DMA((2,))],
    )
    return pl.pallas_call(
        functools.partial(_combine_kernel, final_norm=final_norm),
        grid_spec=grid_spec,
        out_shape=jax.ShapeDtypeStruct((T, D), F32),
        compiler_params=_params("arbitrary"),
        name="combine_norm" if final_norm else "combine",
    )(d0, d1, yb, h, g0, g1, nw)


def _pad_cols(w, n):
    return jnp.pad(w, ((0, 0), (0, n - w.shape[1])))


def moe(h, hn, id0, id1, g0, g1, counts, w_gate_up, w_down, layer, final_norm_w=None):
    T, D = h.shape
    rb = ROW_BLOCK
    n_blocks = -(-(T * TOP_K + N_EXPERTS * (rb - 1)) // rb) + GATHER_AHEAD
    cnt = counts[0, :N_EXPERTS].astype(I32)
    padded = ((cnt + rb - 1) // rb) * rb
    pends = jnp.cumsum(padded)
    pstart_row = _pad_cols((pends - padded).astype(F32).reshape(1, N_EXPERTS), LANES)
    d0b, d1b = rank(id0, id1, pstart_row)
    d0, d1 = d0b[:, 0], d1b[:, 0]
    row_tok = invert(d0, d1, n_blocks * rb)
    block_start = jnp.arange(n_blocks, dtype=I32) * rb
    block_exp = jnp.minimum(jnp.sum(pends[None, :] <= block_start[:, None], axis=1), N_EXPERTS - 1).astype(I32)
    n_used = (pends[-1:] // rb).astype(I32)
    yb = experts(block_exp, row_tok, n_used, hn, w_gate_up, w_down, layer)
    return combine(d0, d1, yb, h, g0, g1, final_norm_w)


def _router_weights(w_group, b_group, w_router, b_router):
    w = _pad_cols(jnp.concatenate([w_router, w_group], axis=1), LANES)
    b = _pad_cols(jnp.concatenate([b_router, b_group]).reshape(1, -1), LANES)
    hi = w.astype(BF16)
    lo = (w - hi.astype(F32)).astype(BF16)
    return hi, lo, b


def kernel(x, mix_norm, gla_w_in, gla_w_gk2, gla_b_gk, gla_head_norm, gla_w_out, conv_w_in, conv_w,
           conv_w_out, ffn_norm, w_group, b_group, w_router, b_router, w_gate_up, w_down, final_norm):
    B, T, D = x.shape
    assert B == 1, "the recurrence state is carried across the whole row axis"
    h = x.reshape(T, D)
    kd, vd = D // 2, D
    dk, dv = kd // GLA_HEADS, vd // GLA_HEADS

    w_in = gla_w_in[0]
    n_main = 2 * kd + 2 * vd
    proj, gl = norm_proj(h, mix_norm[0], w_in[:, :n_main].astype(BF16),
                         _pad_cols(w_in[:, n_main:], LANES).astype(BF16))
    w_gk2 = jnp.pad(gla_w_gk2[0], ((0, LANES - GATE_RANK), (0, 0))).astype(BF16)
    o = gla(proj, gl, w_gk2, gla_b_gk[0].reshape(1, kd), gla_head_norm[0].reshape(1, dv), dk=dk, dv=dv)
    routed = out_router(o, gla_w_out[0].astype(BF16), h, ffn_norm[0],
                        *_router_weights(w_group[0], b_group[0], w_router[0], b_router[0]))
    h = moe(*routed, w_gate_up, w_down, 0)

    proj = norm_proj(h, mix_norm[1], conv_w_in[0].astype(BF16))
    routed = out_router(proj, conv_w_out[0].astype(BF16), h, ffn_norm[1],
                        *_router_weights(w_group[1], b_group[1], w_router[1], b_router[1]),
                        conv_w=conv_w[0])
    out = moe(*routed, w_gate_up, w_down, 1, final_norm_w=final_norm)
    return out.reshape(B, T, D)
```

```python
import functools

import jax
import jax.numpy as jnp
from jax import lax
from jax.experimental import pallas as pl
from jax.experimental.pallas import tpu as pltpu

F32 = jnp.float32
BF16 = jnp.bfloat16
I32 = jnp.int32

EPS = 1e-6
GLA_HEADS = 4
GATE_RANK = 16
GATE_NORMALIZER = 16.0
CONV_WIDTH = 3
N_GROUPS = 4
EXPERTS_PER_GROUP = 8
N_EXPERTS = N_GROUPS * EXPERTS_PER_GROUP
TOP_K = 2

LANES = 128
SUBLANES = 8
BF16_ROWS = 16
HIGH_HALF = -65536
VMEM_LIMIT = 56 * 1024 * 1024

GLA_CHUNK = 64
GLA_SUB = 16
GLA_GROUP = 128
GLA_SAFE_DECAY = 60.0
ROW_BLOCK = 256
GATHER_AHEAD = 2


def _tile(n, pref):
    t = min(n, pref)
    assert n % t == 0, (n, t)
    return t


def _params(*sem):
    return pltpu.CompilerParams(dimension_semantics=sem, vmem_limit_bytes=VMEM_LIMIT)


def _dot(a, b):
    return jnp.dot(a, b, preferred_element_type=F32)


def _dot_nt(a, b):
    return lax.dot_general(a, b, (((1,), (1,)), ((), ())), preferred_element_type=F32)


def _dot_tn(a, b):
    return lax.dot_general(a, b, (((0,), (0,)), ((), ())), preferred_element_type=F32)


def _rms(x, w):
    return x * lax.rsqrt(jnp.mean(x * x, axis=-1, keepdims=True) + EPS) * w


def _packed_sublanes(d):
    assert d % (2 * LANES) == 0
    return d // (2 * LANES)


def _store_packed(ref, x):
    rows, d = x.shape
    half = d // 2
    sub = half // LANES
    lo = lax.shift_right_logical(lax.bitcast_convert_type(x[:, :half].astype(BF16).astype(F32), I32), 16)
    hi = lax.bitcast_convert_type(x[:, half:].astype(BF16).astype(F32), I32) & HIGH_HALF
    words = hi | lo
    for s in range(sub):
        ref[pl.ds(s, rows, stride=sub), :] = words[:, s * LANES:(s + 1) * LANES]


def _load_packed(ref, rows, sub):
    los, his = [], []
    for s in range(sub):
        w = ref[pl.ds(s, rows, stride=sub), :]
        los.append(lax.bitcast_convert_type(lax.shift_left(w, 16), F32))
        his.append(lax.bitcast_convert_type(w & HIGH_HALF, F32))
    return los, his


def _norm_proj_kernel(*refs, with_extra):
    if with_extra:
        x_ref, nw_ref, w_ref, we_ref, o_ref, e_ref, xn_ref = refs
    else:
        x_ref, nw_ref, w_ref, o_ref, xn_ref = refs

    @pl.when(pl.program_id(1) == 0)
    def _():
        xn_ref[...] = _rms(x_ref[...], nw_ref[...]).astype(BF16)
        if with_extra:
            e_ref[...] = _dot(xn_ref[...], we_ref[...])

    o_ref[...] = _dot(xn_ref[...], w_ref[...]).astype(o_ref.dtype)


def norm_proj(x, norm_w, w, w_extra=None):
    T, D = x.shape
    N = w.shape[1]
    tm, tn = _tile(T, 1024), _tile(N, 1024)
    in_specs = [
        pl.BlockSpec((tm, D), lambda i, j: (i, 0)),
        pl.BlockSpec((1, D), lambda i, j: (0, 0)),
        pl.BlockSpec((D, tn), lambda i, j: (0, j)),
    ]
    out_shape = [jax.ShapeDtypeStruct((T, N), BF16)]
    out_specs = [pl.BlockSpec((tm, tn), lambda i, j: (i, j))]
    args = [x, norm_w.reshape(1, D), w]
    if w_extra is not None:
        in_specs.append(pl.BlockSpec((D, LANES), lambda i, j: (0, 0)))
        out_shape.append(jax.ShapeDtypeStruct((T, LANES), F32))
        out_specs.append(pl.BlockSpec((tm, LANES), lambda i, j: (i, 0)))
        args.append(w_extra)
    out = pl.pallas_call(
        functools.partial(_norm_proj_kernel, with_extra=w_extra is not None),
        grid=(T // tm, N // tn),
        in_specs=in_specs,
        out_specs=out_specs,
        out_shape=out_shape,
        scratch_shapes=[pltpu.VMEM((tm, D), BF16)],
        compiler_params=_params("arbitrary", "arbitrary"),
        name="norm_proj",
    )(*args)
    return out if w_extra is not None else out[0]


def _gla_kernel(q_ref, k_ref, v_ref, g_ref, gl_ref, wgk_ref, bgk_ref, hnorm_ref, o_ref,
                st_ref, bs_ref, qt_ref, kh_ref, bref_ref, *, dk, dv, tb):
    C, SUB, GRP, H = GLA_CHUNK, GLA_SUB, GLA_GROUP, GLA_HEADS
    NS = C // SUB
    CPG = GRP // C
    kd = H * dk
    scale = dk ** -0.5

    @pl.when(pl.program_id(0) == 0)
    def _():
        st_ref[...] = jnp.zeros_like(st_ref)

    r = lax.broadcasted_iota(I32, (GRP, GRP), 0)
    c = lax.broadcasted_iota(I32, (GRP, GRP), 1)
    same = (r // SUB) == (c // SUB)
    low = jnp.where(same & (c <= r), 1.0, 0.0).astype(BF16)
    upp = jnp.where(same & (c > r), 1.0, 0.0).astype(BF16)
    rr = lax.broadcasted_iota(I32, (CPG * SUBLANES, GRP), 0)
    cc = lax.broadcasted_iota(I32, (CPG * SUBLANES, GRP), 1)
    blk = rr % SUBLANES
    mref = jnp.where(((cc // C) == (rr // SUBLANES)) & ((cc % C) < SUB * blk) & (blk <= NS),
                     1.0, 0.0).astype(BF16)

    for gi in range(tb // GRP):
        sl = pl.ds(gi * GRP, GRP)
        z = _dot(gl_ref[sl, :].astype(BF16), wgk_ref[...]) + bgk_ref[...]
        la = (jnp.minimum(z, 0.0) - jnp.log(1.0 + jnp.exp(-jnp.abs(z)))) * (1.0 / GATE_NORMALIZER)
        hi = la.astype(BF16)
        lo = (la - hi.astype(F32)).astype(BF16)
        bs = _dot(low, hi) + _dot(low, lo)
        ru = _dot(upp, hi) + _dot(upp, lo)
        br = _dot(mref, hi) + _dot(mref, lo)
        bs_ref[sl, :] = bs
        qt_ref[sl, :] = q_ref[sl, :].astype(F32) * jnp.exp(bs) * scale
        kh_ref[sl, :] = k_ref[sl, :].astype(F32) * jnp.exp(ru)
        bref_ref[pl.ds(gi * CPG, CPG)] = br.reshape(CPG, SUBLANES, kd)

    lane = lax.broadcasted_iota(I32, (SUB, C), 1)
    subrow = lax.broadcasted_iota(I32, (SUB, dk), 0)

    def scores_pivoted(qin, kraw, bs, bn):
        neg_b = jnp.concatenate([-(bs[J * SUB:(J + 1) * SUB] + bn[J:J + 1]) for J in range(NS)], axis=0)
        s = _dot_nt(qin.astype(BF16), (kraw * jnp.exp(neg_b)).astype(BF16))
        ri = lax.broadcasted_iota(I32, (C, C), 0)
        ci = lax.broadcasted_iota(I32, (C, C), 1)
        return jnp.where(ci <= ri, s, 0.0)

    def scores_stable(qt, kh, qraw, kraw, bs, bn):
        a_rows = []
        for I in range(NS):
            blk_rows = slice(I * SUB, (I + 1) * SUB)
            q_blk, k_blk, bs_blk = qraw[blk_rows], kraw[blk_rows], bs[blk_rows]
            acc = jnp.zeros((SUB, C), F32)
            for j in range(SUB):
                e = jnp.where(subrow >= j, jnp.exp(bs_blk - bs_blk[j:j + 1]), 0.0)
                p = (q_blk * k_blk[j:j + 1]) * e
                acc = jnp.where(lane == I * SUB + j, jnp.sum(p, axis=-1, keepdims=True), acc)
            if I > 0:
                parts = []
                for J in range(NS):
                    if J < I:
                        parts.append(kh[J * SUB:(J + 1) * SUB] * jnp.exp(bn[I:I + 1] - bn[J + 1:J + 2]))
                    else:
                        parts.append(jnp.zeros((SUB, dk), F32))
                k_dec = jnp.concatenate(parts, axis=0)
                acc = acc + _dot_nt(qt[blk_rows].astype(BF16), k_dec.astype(BF16))
            a_rows.append(acc)
        return jnp.concatenate(a_rows, axis=0)

    def chunk(n, carry, *, pivoted):
        r0 = pl.multiple_of(n * C, C)
        rows = pl.ds(r0, C)
        outs = []
        for h in range(H):
            kl = slice(h * dk, (h + 1) * dk)
            vl = slice(h * dv, (h + 1) * dv)
            bn = bref_ref[n, :, kl]
            eb = jnp.exp(bn)
            st = st_ref[h]
            qt = qt_ref[rows, kl]
            kh = kh_ref[rows, kl]
            bs = bs_ref[rows, kl]
            v = v_ref[rows, vl]
            kraw = k_ref[rows, kl].astype(F32)

            qin = jnp.concatenate([qt[I * SUB:(I + 1) * SUB] * eb[I:I + 1] for I in range(NS)], axis=0)
            o = _dot_nt(qin.astype(BF16), st.astype(BF16))
            if pivoted:
                a = scores_pivoted(qin, kraw, bs, bn)
            else:
                a = scores_stable(qt, kh, q_ref[rows, kl].astype(F32) * scale, kraw, bs, bn)
            o = o + _dot(a.astype(BF16), v)

            k_st = jnp.concatenate(
                [kh[J * SUB:(J + 1) * SUB] * jnp.exp(bn[NS:NS + 1] - bn[J + 1:J + 2]) for J in range(NS)],
                axis=0)
            st_ref[h] = st * eb[NS:NS + 1] + _dot_tn(v, k_st.astype(BF16))
            outs.append(_rms(o, hnorm_ref[...]))

        g = g_ref[rows, :].astype(F32)
        o_ref[rows, :] = (jnp.concatenate(outs, axis=1) * (g * jax.nn.sigmoid(g))).astype(o_ref.dtype)
        return carry

    span = jnp.max(-bref_ref[...])

    @pl.when(span < GLA_SAFE_DECAY)
    def _():
        lax.fori_loop(0, tb // C, functools.partial(chunk, pivoted=True), 0)

    @pl.when(jnp.logical_not(span < GLA_SAFE_DECAY))
    def _():
        lax.fori_loop(0, tb // C, functools.partial(chunk, pivoted=False), 0)


def gla(proj, gl, w_gk2, b_gk, head_norm, *, dk, dv):
    T = proj.shape[0]
    H = GLA_HEADS
    tb = _tile(T, 512)
    assert tb % GLA_GROUP == 0 and dk % LANES == 0 and dv % LANES == 0
    kd, vd = H * dk, H * dv
    assert (2 * kd) % vd == 0
    v_off = (2 * kd) // vd
    fixed = lambda t: (0, 0)
    return pl.pallas_call(
        functools.partial(_gla_kernel, dk=dk, dv=dv, tb=tb),
        grid=(T // tb,),
        in_specs=[
            pl.BlockSpec((tb, kd), lambda t: (t, 0)),
            pl.BlockSpec((tb, kd), lambda t: (t, 1)),
            pl.BlockSpec((tb, vd), lambda t: (t, v_off)),
            pl.BlockSpec((tb, vd), lambda t: (t, v_off + 1)),
            pl.BlockSpec((tb, LANES), lambda t: (t, 0)),
            pl.BlockSpec((LANES, kd), fixed),
            pl.BlockSpec((1, kd), fixed),
            pl.BlockSpec((1, dv), fixed),
        ],
        out_specs=pl.BlockSpec((tb, vd), lambda t: (t, 0)),
        out_shape=jax.ShapeDtypeStruct((T, vd), BF16),
        scratch_shapes=[
            pltpu.VMEM((H, dv, dk), F32),
            pltpu.VMEM((tb, kd), F32),
            pltpu.VMEM((tb, kd), F32),
            pltpu.VMEM((tb, kd), F32),
            pltpu.VMEM((tb // GLA_CHUNK, SUBLANES, kd), F32),
        ],
        compiler_params=_params("arbitrary"),
        name="gla",
    )(proj, proj, proj, proj, gl, w_gk2, b_gk, head_norm)


def _route(hn, rhi_ref, rlo_ref, rb_ref):
    tm = hn.shape[0]
    hi = hn.astype(BF16)
    lo = (hn - hi.astype(F32)).astype(BF16)
    logits = _dot(hi, rhi_ref[...]) + _dot(hi, rlo_ref[...]) + _dot(lo, rhi_ref[...]) + rb_ref[...]
    lane = lax.broadcasted_iota(I32, (tm, LANES), 1)
    lane_f = lane.astype(F32)
    neg = -jnp.inf
    far = float(LANES)

    def first_max(vals):
        m = jnp.max(vals, axis=-1, keepdims=True)
        idx = jnp.min(jnp.where(vals == m, lane_f, far), axis=-1, keepdims=True)
        return m, idx

    lg = jnp.where((lane >= N_EXPERTS) & (lane < N_EXPERTS + N_GROUPS), logits, neg)
    mg, gidx = first_max(lg)
    top_gp = 1.0 / jnp.sum(jnp.exp(lg - mg), axis=-1, keepdims=True)
    grp = gidx.astype(I32) - N_EXPERTS
    le = jnp.where((lane < N_EXPERTS) & ((lane // EXPERTS_PER_GROUP) == grp), logits, neg)
    m1, i1 = first_max(le)
    m2, i2 = first_max(jnp.where(lane_f == i1, neg, le))
    e2 = jnp.exp(m2 - m1)
    w1 = 1.0 / (1.0 + e2)
    return i1.astype(I32), i2.astype(I32), top_gp * w1, top_gp * (e2 * w1)


def _out_router_kernel(*refs, conv):
    if conv:
        (pb_ref, pc_ref, ph_ref, hc_ref, hh_ref, cw_ref, w_ref, hin_ref, fnw_ref, rhi_ref, rlo_ref,
         rb_ref, h_ref, hn_ref, id0_ref, id1_ref, g0_ref, g1_ref, cnt_ref) = refs
    else:
        (y_ref, w_ref, hin_ref, fnw_ref, rhi_ref, rlo_ref,
         rb_ref, h_ref, hn_ref, id0_ref, id1_ref, g0_ref, g1_ref, cnt_ref) = refs
    i = pl.program_id(0)

    if conv:
        u = pc_ref[...].astype(F32) * ph_ref[...].astype(F32)
        prev = hc_ref[...].astype(F32) * hh_ref[...].astype(F32)
        prev = jnp.where(i > 0, prev, 0.0)
        row = lax.broadcasted_iota(I32, u.shape, 0)
        last, last2 = prev[BF16_ROWS - 1:BF16_ROWS], prev[BF16_ROWS - 2:BF16_ROWS - 1]
        u1 = jnp.where(row == 0, last, pltpu.roll(u, 1, 0))
        u2 = jnp.where(row == 0, last2, jnp.where(row == 1, last, pltpu.roll(u, 2, 0)))
        cw = cw_ref[...]
        y = (pb_ref[...].astype(F32) * (cw[0:1] * u2 + cw[1:2] * u1 + cw[2:3] * u)).astype(BF16)
    else:
        y = y_ref[...]

    h = hin_ref[...] + _dot(y, w_ref[...])
    h_ref[...] = h
    hn = _rms(h, fnw_ref[...])
    _store_packed(hn_ref, hn)
    i1, i2, g1, g2 = _route(hn, rhi_ref, rlo_ref, rb_ref)
    shape = id0_ref.shape
    id0_ref[...] = jnp.broadcast_to(i1, shape)
    id1_ref[...] = jnp.broadcast_to(i2, shape)
    g0_ref[...] = jnp.broadcast_to(g1, shape)
    g1_ref[...] = jnp.broadcast_to(g2, shape)

    @pl.when(i == 0)
    def _():
        cnt_ref[...] = jnp.zeros_like(cnt_ref)

    lane = lax.broadcasted_iota(I32, shape, 1)
    hit = jnp.where((lane == i1) | (lane == i2), 1.0, 0.0)
    cnt_ref[...] += jnp.sum(hit, axis=0, keepdims=True)


def out_router(y, w_out, h_in, ffn_norm_w, r_hi, r_lo, r_b, conv_w=None):
    T, D = h_in.shape
    K = w_out.shape[0]
    tm = _tile(T, 256)
    conv = conv_w is not None
    row = lambda i: (i, 0)
    fixed = lambda i: (0, 0)
    if conv:
        hb = tm // BF16_ROWS
        halo = lambda col: (lambda i: (jnp.maximum(i * hb - 1, 0), col))
        in_specs = [
            pl.BlockSpec((tm, K), lambda i: (i, 0)),
            pl.BlockSpec((tm, K), lambda i: (i, 1)),
            pl.BlockSpec((tm, K), lambda i: (i, 2)),
            pl.BlockSpec((BF16_ROWS, K), halo(1)),
            pl.BlockSpec((BF16_ROWS, K), halo(2)),
            pl.BlockSpec((SUBLANES, K), fixed),
        ]
        cw = jnp.zeros((SUBLANES, K), F32).at[:CONV_WIDTH].set(conv_w)
        args = [y, y, y, y, y, cw]
    else:
        in_specs = [pl.BlockSpec((tm, K), row)]
        args = [y]
    in_specs += [
        pl.BlockSpec((K, D), fixed),
        pl.BlockSpec((tm, D), row),
        pl.BlockSpec((1, D), fixed),
        pl.BlockSpec((D, LANES), fixed),
        pl.BlockSpec((D, LANES), fixed),
        pl.BlockSpec((1, LANES), fixed),
    ]
    args += [w_out, h_in, ffn_norm_w.reshape(1, D), r_hi, r_lo, r_b]
    wide = lambda dt: jax.ShapeDtypeStruct((T, LANES), dt)
    sub = _packed_sublanes(D)
    return pl.pallas_call(
        functools.partial(_out_router_kernel, conv=conv),
        grid=(T // tm,),
        in_specs=in_specs,
        out_specs=[pl.BlockSpec((tm, D), row), pl.BlockSpec((tm * sub, LANES), row)]
        + [pl.BlockSpec((tm, LANES), row)] * 4 + [pl.BlockSpec((1, LANES), fixed)],
        out_shape=[jax.ShapeDtypeStruct((T, D), F32), jax.ShapeDtypeStruct((T * sub, LANES), I32),
                   wide(I32), wide(I32), wide(F32), wide(F32), jax.ShapeDtypeStruct((1, LANES), F32)],
        compiler_params=_params("arbitrary"),
        name="out_router_conv" if conv else "out_router",
    )(*args)


def _rank_kernel(id0_ref, id1_ref, pst_ref, d0_ref, d1_ref, carry_ref):
    @pl.when(pl.program_id(0) == 0)
    def _():
        carry_ref[...] = jnp.zeros_like(carry_ref)

    tb = id0_ref.shape[0]
    lane = lax.broadcasted_iota(I32, (tb, LANES), 1)
    oh0 = lane == id0_ref[...]
    oh1 = lane == id1_ref[...]
    hit = jnp.where(oh0 | oh1, 1.0, 0.0)
    r = lax.broadcasted_iota(I32, (tb, tb), 0)
    c = lax.broadcasted_iota(I32, (tb, tb), 1)
    before = jnp.where(c < r, 1.0, 0.0).astype(BF16)
    base = _dot(before, hit.astype(BF16)) + carry_ref[...] + pst_ref[...]
    d0 = jnp.sum(jnp.where(oh0, base, 0.0), axis=-1, keepdims=True)
    d1 = jnp.sum(jnp.where(oh1, base, 0.0), axis=-1, keepdims=True)
    d0_ref[...] = jnp.broadcast_to(d0.astype(I32), (tb, LANES))
    d1_ref[...] = jnp.broadcast_to(d1.astype(I32), (tb, LANES))
    carry_ref[...] += jnp.sum(hit, axis=0, keepdims=True)


def rank(id0, id1, pstart_row):
    T = id0.shape[0]
    tb = _tile(T, 256)
    row = lambda i: (i, 0)
    return pl.pallas_call(
        _rank_kernel,
        grid=(T // tb,),
        in_specs=[pl.BlockSpec((tb, LANES), row), pl.BlockSpec((tb, LANES), row),
                  pl.BlockSpec((1, LANES), lambda i: (0, 0))],
        out_specs=[pl.BlockSpec((tb, LANES), row)] * 2,
        out_shape=[jax.ShapeDtypeStruct((T, LANES), I32)] * 2,
        scratch_shapes=[pltpu.VMEM((1, LANES), F32)],
        compiler_params=_params("arbitrary"),
        name="rank",
    )(id0, id1, pstart_row)


def _invert_kernel(d0_ref, d1_ref, fill_ref, rt_ref):
    def clear(p, carry):
        rt_ref[p] = 0
        return carry

    for e in range(fill_ref.shape[0] // 2):
        lax.fori_loop(fill_ref[2 * e], fill_ref[2 * e + 1], clear, 0)

    def place(t, carry):
        rt_ref[d0_ref[t]] = t
        rt_ref[d1_ref[t]] = t
        return carry

    lax.fori_loop(0, d0_ref.shape[0], place, 0, unroll=8)


def invert(d0, d1, fill_ranges, n_rows):
    smem = pl.BlockSpec(memory_space=pltpu.SMEM)
    return pl.pallas_call(
        _invert_kernel,
        in_specs=[smem, smem, smem],
        out_specs=smem,
        out_shape=jax.ShapeDtypeStruct((n_rows,), I32),
        name="invert",
    )(d0, d1, fill_ranges)


def _row_copy(src_hbm, dst_ref, sem, src_row, dst_row, sub):
    src = src_hbm.at[pl.ds(pl.multiple_of(src_row * sub, sub), sub), :]
    dst = dst_ref.at[pl.ds(pl.multiple_of(dst_row * sub, sub), sub), :]
    return pltpu.make_async_copy(src, dst, sem)


def _experts_kernel(bexp_ref, rtok_ref, nused_ref, hn_hbm, wgu_ref, wd_ref, y_ref, xbuf0, xbuf1, xbuf2, wgu_bf,
                    wd_bf, sem, *, ff, rb, sub):
    i = pl.program_id(0)
    n_used = nused_ref[0]
    bufs = (xbuf0, xbuf1, xbuf2)
    n_buf = len(bufs)
    assert n_buf == GATHER_AHEAD + 1

    def wait_rows(blk, s):
        def body(r, carry):
            _row_copy(hn_hbm, bufs[s], sem.at[s], rtok_ref[blk * rb + r], r, sub).wait()
            return carry

        lax.fori_loop(0, rb, body, 0, unroll=8)

    def start_rows(blk, s):
        for r in range(rb):
            _row_copy(hn_hbm, bufs[s], sem.at[s], rtok_ref[blk * rb + r], r, sub).start()

    @pl.when(i == 0)
    def _():
        for blk in range(GATHER_AHEAD):
            start_rows(blk, blk)

    @pl.when((i == 0) | (bexp_ref[i] != bexp_ref[jnp.maximum(i - 1, 0)]))
    def _():
        wgu_bf[...] = wgu_ref[0, 0].astype(BF16)
        wd_bf[...] = wd_ref[0, 0].astype(BF16)

    for s in range(n_buf):
        @pl.when((i < n_used) & (i % n_buf == s))
        def _():
            wait_rows(i, s)
            start_rows(i + GATHER_AHEAD, (s + GATHER_AHEAD) % n_buf)
            los, his = _load_packed(bufs[s], rb, sub)
            x = jnp.concatenate([p.astype(BF16) for p in los + his], axis=1)
            gu = _dot(x, wgu_bf[...])
            gt, up = gu[:, :ff], gu[:, ff:]
            act = (gt * jax.nn.sigmoid(gt) * up).astype(BF16)
            _store_packed(y_ref, _dot(act, wd_bf[...]))

        @pl.when((i >= n_used) & (i < n_used + GATHER_AHEAD) & (i % n_buf == s))
        def _():
            wait_rows(i, s)

    @pl.when(i >= n_used)
    def _():
        y_ref[...] = jnp.zeros_like(y_ref)


def experts(block_exp, row_tok, n_used, hn, w_gate_up, w_down, layer):
    D, ff2 = w_gate_up.shape[-2:]
    ff = ff2 // 2
    sub = _packed_sublanes(D)
    P = row_tok.shape[0]
    rb = ROW_BLOCK
    grid_spec = pltpu.PrefetchScalarGridSpec(
        num_scalar_prefetch=3,
        grid=(P // rb,),
        in_specs=[
            pl.BlockSpec(memory_space=pl.ANY),
            pl.BlockSpec((1, 1, D, ff2), lambda i, be, rt, nu: (layer, be[i], 0, 0)),
            pl.BlockSpec((1, 1, ff, D), lambda i, be, rt, nu: (layer, be[i], 0, 0)),
        ],
        out_specs=pl.BlockSpec((rb * sub, LANES), lambda i, be, rt, nu: (i, 0)),
        scratch_shapes=[pltpu.VMEM((rb * sub, LANES), I32)] * (GATHER_AHEAD + 1)
        + [pltpu.VMEM((D, ff2), BF16), pltpu.VMEM((ff, D), BF16), pltpu.SemaphoreType.DMA((GATHER_AHEAD + 1,))],
    )
    return pl.pallas_call(
        functools.partial(_experts_kernel, ff=ff, rb=rb, sub=sub),
        grid_spec=grid_spec,
        out_shape=jax.ShapeDtypeStruct((P * sub, LANES), I32),
        compiler_params=_params("arbitrary"),
        name="experts",
    )(block_exp, row_tok, n_used, hn, w_gate_up, w_down)


def _combine_kernel(d0_ref, d1_ref, yb_hbm, h_ref, g0_ref, g1_ref, nw_ref, o_ref, b00, b01, b10, b11, sem,
                    *, final_norm, sub):
    i = pl.program_id(0)
    n = pl.num_programs(0)
    tc = o_ref.shape[0]
    bufs = ((b00, b01), (b10, b11))
    d_refs = (d0_ref, d1_ref)

    def wait_rows(blk, s):
        def body(r, carry):
            for k in range(TOP_K):
                _row_copy(yb_hbm, bufs[s][k], sem.at[s], d_refs[k][blk * tc + r], r, sub).wait()
            return carry

        lax.fori_loop(0, tc, body, 0, unroll=8)

    def start_rows(blk, s):
        for r in range(tc):
            for k in range(TOP_K):
                _row_copy(yb_hbm, bufs[s][k], sem.at[s], d_refs[k][blk * tc + r], r, sub).start()

    @pl.when(i == 0)
    def _():
        start_rows(0, 0)

    for s in range(2):
        @pl.when(i % 2 == s)
        def _():
            wait_rows(i, s)
            start_rows(i + 1, 1 - s)
            lo0, hi0 = _load_packed(bufs[s][0], tc, sub)
            lo1, hi1 = _load_packed(bufs[s][1], tc, sub)
            g0, g1 = g0_ref[...], g1_ref[...]
            pieces = [h_ref[:, j * LANES:(j + 1) * LANES] + g0 * y0 + g1 * y1
                      for j, (y0, y1) in enumerate(zip(lo0 + hi0, lo1 + hi1))]
            if final_norm:
                o_ref[...] = _rms(jnp.concatenate(pieces, axis=1), nw_ref[...])
            else:
                for j, piece in enumerate(pieces):
                    o_ref[:, j * LANES:(j + 1) * LANES] = piece

            @pl.when(i == n - 1)
            def _():
                wait_rows(i + 1, 1 - s)


def combine(d0, d1, yb, h, g0, g1, norm_w=None):
    T, D = h.shape
    tc = _tile(T, 256)
    sub = _packed_sublanes(D)
    final_norm = norm_w is not None
    nw = (norm_w if final_norm else jnp.ones((D,), F32)).reshape(1, D)
    spare = jnp.zeros((tc,), I32)
    row = lambda i, a, b: (i, 0)
    grid_spec = pltpu.PrefetchScalarGridSpec(
        num_scalar_prefetch=2,
        grid=(T // tc,),
        in_specs=[
            pl.BlockSpec(memory_space=pl.ANY),
            pl.BlockSpec((tc, D), row),
            pl.BlockSpec((tc, LANES), row),
            pl.BlockSpec((tc, LANES), row),
            pl.BlockSpec((1, D), lambda i, a, b: (0, 0)),
        ],
        out_specs=pl.BlockSpec((tc, D), row),
        scratch_shapes=[pltpu.VMEM((tc * sub, LANES), I32)] * (2 * TOP_K) + [pltpu.SemaphoreType.DMA((2,))],
    )
    return pl.pallas_call(
        functools.partial(_combine_kernel, final_norm=final_norm, sub=sub),
        grid_spec=grid_spec,
        out_shape=jax.ShapeDtypeStruct((T, D), F32),
        compiler_params=_params("arbitrary"),
        name="combine_norm" if final_norm else "combine",
    )(jnp.concatenate([d0, spare]), jnp.concatenate([d1, spare]), yb, h, g0, g1, nw)


def _pad_cols(w, n):
    return jnp.pad(w, ((0, 0), (0, n - w.shape[1])))


def moe(h, hn, id0, id1, g0, g1, counts, w_gate_up, w_down, layer, final_norm_w=None):
    T, D = h.shape
    rb = ROW_BLOCK
    n_blocks = -(-(T * TOP_K + N_EXPERTS * (rb - 1)) // rb) + GATHER_AHEAD
    cnt = counts[0, :N_EXPERTS].astype(I32)
    padded = ((cnt + rb - 1) // rb) * rb
    pends = jnp.cumsum(padded)
    pstart_row = _pad_cols((pends - padded).astype(F32).reshape(1, N_EXPERTS), LANES)
    d0b, d1b = rank(id0, id1, pstart_row)
    d0, d1 = d0b[:, 0], d1b[:, 0]
    fill_start = jnp.concatenate([pends - padded + cnt, pends[-1:]])
    fill_end = jnp.concatenate([pends, jnp.full((1,), n_blocks * rb, I32)])
    row_tok = invert(d0, d1, jnp.stack([fill_start, fill_end], axis=1).reshape(-1), n_blocks * rb)
    block_start = jnp.arange(n_blocks, dtype=I32) * rb
    block_exp = jnp.minimum(jnp.sum(pends[None, :] <= block_start[:, None], axis=1), N_EXPERTS - 1).astype(I32)
    n_used = (pends[-1:] // rb).astype(I32)
    yb = experts(block_exp, row_tok, n_used, hn, w_gate_up, w_down, layer)
    return combine(d0, d1, yb, h, g0, g1, final_norm_w)


def _router_weights(w_group, b_group, w_router, b_router):
    w = _pad_cols(jnp.concatenate([w_router, w_group], axis=1), LANES)
    b = _pad_cols(jnp.concatenate([b_router, b_group]).reshape(1, -1), LANES)
    hi = w.astype(BF16)
    lo = (w - hi.astype(F32)).astype(BF16)
    return hi, lo, b


def kernel(x, mix_norm, gla_w_in, gla_w_gk2, gla_b_gk, gla_head_norm, gla_w_out, conv_w_in, conv_w,
           conv_w_out, ffn_norm, w_group, b_group, w_router, b_router, w_gate_up, w_down, final_norm):
    B, T, D = x.shape
    assert B == 1, "the recurrence state is carried across the whole row axis"
    h = x.reshape(T, D)
    kd, vd = D // 2, D
    dk, dv = kd // GLA_HEADS, vd // GLA_HEADS

    w_in = gla_w_in[0]
    n_main = 2 * kd + 2 * vd
    proj, gl = norm_proj(h, mix_norm[0], w_in[:, :n_main].astype(BF16),
                         _pad_cols(w_in[:, n_main:], LANES).astype(BF16))
    w_gk2 = jnp.pad(gla_w_gk2[0], ((0, LANES - GATE_RANK), (0, 0))).astype(BF16)
    o = gla(proj, gl, w_gk2, gla_b_gk[0].reshape(1, kd), gla_head_norm[0].reshape(1, dv), dk=dk, dv=dv)
    routed = out_router(o, gla_w_out[0].astype(BF16), h, ffn_norm[0],
                        *_router_weights(w_group[0], b_group[0], w_router[0], b_router[0]))
    h = moe(*routed, w_gate_up, w_down, 0)

    proj = norm_proj(h, mix_norm[1], conv_w_in[0].astype(BF16))
    routed = out_router(proj, conv_w_out[0].astype(BF16), h, ffn_norm[1],
                        *_router_weights(w_group[1], b_group[1], w_router[1], b_router[1]),
                        conv_w=conv_w[0])
    out = moe(*routed, w_gate_up, w_down, 1, final_norm_w=final_norm)
    return out.reshape(B, T, D)
```

```python
import functools

import jax
import jax.numpy as jnp
from jax import lax
from jax.experimental import pallas as pl
from jax.experimental.pallas import tpu as pltpu

F32 = jnp.float32
BF16 = jnp.bfloat16
I32 = jnp.int32

EPS = 1e-6
GLA_HEADS = 4
GATE_RANK = 16
GATE_NORMALIZER = 16.0
CONV_WIDTH = 3
N_GROUPS = 4
EXPERTS_PER_GROUP = 8
N_EXPERTS = N_GROUPS * EXPERTS_PER_GROUP
TOP_K = 2

LANES = 128
SUBLANES = 8
BF16_ROWS = 16
HIGH_HALF = -65536
VMEM_LIMIT = 56 * 1024 * 1024

GLA_CHUNK = 64
GLA_SUB = 16
GLA_GROUP = 128
GLA_SAFE_DECAY = 60.0
ROW_BLOCK = 256
GATHER_AHEAD = 2


def _tile(n, pref):
    t = min(n, pref)
    assert n % t == 0, (n, t)
    return t


def _params(*sem):
    return pltpu.CompilerParams(dimension_semantics=sem, vmem_limit_bytes=VMEM_LIMIT)


def _dot(a, b):
    return jnp.dot(a, b, preferred_element_type=F32)


def _dot_nt(a, b):
    return lax.dot_general(a, b, (((1,), (1,)), ((), ())), preferred_element_type=F32)


def _dot_tn(a, b):
    return lax.dot_general(a, b, (((0,), (0,)), ((), ())), preferred_element_type=F32)


def _rms(x, w):
    return x * lax.rsqrt(jnp.mean(x * x, axis=-1, keepdims=True) + EPS) * w


def _packed_sublanes(d):
    assert d % (2 * LANES) == 0
    return d // (2 * LANES)


def _store_packed(ref, x):
    rows, d = x.shape
    half = d // 2
    sub = half // LANES
    lo = lax.shift_right_logical(lax.bitcast_convert_type(x[:, :half].astype(BF16).astype(F32), I32), 16)
    hi = lax.bitcast_convert_type(x[:, half:].astype(BF16).astype(F32), I32) & HIGH_HALF
    words = hi | lo
    for s in range(sub):
        ref[pl.ds(s, rows, stride=sub), :] = words[:, s * LANES:(s + 1) * LANES]


def _load_packed(ref, rows, sub):
    los, his = [], []
    for s in range(sub):
        w = ref[pl.ds(s, rows, stride=sub), :]
        los.append(lax.bitcast_convert_type(lax.shift_left(w, 16), F32))
        his.append(lax.bitcast_convert_type(w & HIGH_HALF, F32))
    return los, his


def _norm_proj_kernel(*refs, with_extra):
    if with_extra:
        x_ref, nw_ref, w_ref, we_ref, o_ref, e_ref, xn_ref = refs
    else:
        x_ref, nw_ref, w_ref, o_ref, xn_ref = refs

    @pl.when(pl.program_id(1) == 0)
    def _():
        xn_ref[...] = _rms(x_ref[...], nw_ref[...]).astype(BF16)
        if with_extra:
            e_ref[...] = _dot(xn_ref[...], we_ref[...])

    o_ref[...] = _dot(xn_ref[...], w_ref[...]).astype(o_ref.dtype)


def norm_proj(x, norm_w, w, w_extra=None):
    T, D = x.shape
    N = w.shape[1]
    tm, tn = _tile(T, 1024), _tile(N, 1024)
    in_specs = [
        pl.BlockSpec((tm, D), lambda i, j: (i, 0)),
        pl.BlockSpec((1, D), lambda i, j: (0, 0)),
        pl.BlockSpec((D, tn), lambda i, j: (0, j)),
    ]
    out_shape = [jax.ShapeDtypeStruct((T, N), BF16)]
    out_specs = [pl.BlockSpec((tm, tn), lambda i, j: (i, j))]
    args = [x, norm_w.reshape(1, D), w]
    if w_extra is not None:
        in_specs.append(pl.BlockSpec((D, LANES), lambda i, j: (0, 0)))
        out_shape.append(jax.ShapeDtypeStruct((T, LANES), F32))
        out_specs.append(pl.BlockSpec((tm, LANES), lambda i, j: (i, 0)))
        args.append(w_extra)
    out = pl.pallas_call(
        functools.partial(_norm_proj_kernel, with_extra=w_extra is not None),
        grid=(T // tm, N // tn),
        in_specs=in_specs,
        out_specs=out_specs,
        out_shape=out_shape,
        scratch_shapes=[pltpu.VMEM((tm, D), BF16)],
        compiler_params=_params("arbitrary", "arbitrary"),
        name="norm_proj",
    )(*args)
    return out if w_extra is not None else out[0]


def _gla_kernel(q_ref, k_ref, v_ref, g_ref, gl_ref, wgk_ref, bgk_ref, hnorm_ref, o_ref,
                st_ref, bs_ref, qt_ref, kh_ref, bref_ref, *, dk, dv, tb):
    C, SUB, GRP, H = GLA_CHUNK, GLA_SUB, GLA_GROUP, GLA_HEADS
    NS = C // SUB
    CPG = GRP // C
    kd = H * dk
    scale = dk ** -0.5

    @pl.when(pl.program_id(0) == 0)
    def _():
        st_ref[...] = jnp.zeros_like(st_ref)

    r = lax.broadcasted_iota(I32, (GRP, GRP), 0)
    c = lax.broadcasted_iota(I32, (GRP, GRP), 1)
    same = (r // SUB) == (c // SUB)
    low = jnp.where(same & (c <= r), 1.0, 0.0).astype(BF16)
    upp = jnp.where(same & (c > r), 1.0, 0.0).astype(BF16)
    rr = lax.broadcasted_iota(I32, (CPG * SUBLANES, GRP), 0)
    cc = lax.broadcasted_iota(I32, (CPG * SUBLANES, GRP), 1)
    blk = rr % SUBLANES
    mref = jnp.where(((cc // C) == (rr // SUBLANES)) & ((cc % C) < SUB * blk) & (blk <= NS),
                     1.0, 0.0).astype(BF16)

    for gi in range(tb // GRP):
        sl = pl.ds(gi * GRP, GRP)
        z = _dot(gl_ref[sl, :].astype(BF16), wgk_ref[...]) + bgk_ref[...]
        la = (jnp.minimum(z, 0.0) - jnp.log(1.0 + jnp.exp(-jnp.abs(z)))) * (1.0 / GATE_NORMALIZER)
        hi = la.astype(BF16)
        lo = (la - hi.astype(F32)).astype(BF16)
        bs = _dot(low, hi) + _dot(low, lo)
        ru = _dot(upp, hi) + _dot(upp, lo)
        br = _dot(mref, hi) + _dot(mref, lo)
        bs_ref[sl, :] = bs
        qt_ref[sl, :] = q_ref[sl, :].astype(F32) * jnp.exp(bs) * scale
        kh_ref[sl, :] = k_ref[sl, :].astype(F32) * jnp.exp(ru)
        bref_ref[pl.ds(gi * CPG, CPG)] = br.reshape(CPG, SUBLANES, kd)

    lane = lax.broadcasted_iota(I32, (SUB, C), 1)
    subrow = lax.broadcasted_iota(I32, (SUB, dk), 0)

    def scores_pivoted(qin, kraw, bs, bn):
        neg_b = jnp.concatenate([-(bs[J * SUB:(J + 1) * SUB] + bn[J:J + 1]) for J in range(NS)], axis=0)
        s = _dot_nt(qin.astype(BF16), (kraw * jnp.exp(neg_b)).astype(BF16))
        ri = lax.broadcasted_iota(I32, (C, C), 0)
        ci = lax.broadcasted_iota(I32, (C, C), 1)
        return jnp.where(ci <= ri, s, 0.0)

    def scores_stable(qt, kh, qraw, kraw, bs, bn):
        a_rows = []
        for I in range(NS):
            blk_rows = slice(I * SUB, (I + 1) * SUB)
            q_blk, k_blk, bs_blk = qraw[blk_rows], kraw[blk_rows], bs[blk_rows]
            acc = jnp.zeros((SUB, C), F32)
            for j in range(SUB):
                e = jnp.where(subrow >= j, jnp.exp(bs_blk - bs_blk[j:j + 1]), 0.0)
                p = (q_blk * k_blk[j:j + 1]) * e
                acc = jnp.where(lane == I * SUB + j, jnp.sum(p, axis=-1, keepdims=True), acc)
            if I > 0:
                parts = []
                for J in range(NS):
                    if J < I:
                        parts.append(kh[J * SUB:(J + 1) * SUB] * jnp.exp(bn[I:I + 1] - bn[J + 1:J + 2]))
                    else:
                        parts.append(jnp.zeros((SUB, dk), F32))
                k_dec = jnp.concatenate(parts, axis=0)
                acc = acc + _dot_nt(qt[blk_rows].astype(BF16), k_dec.astype(BF16))
            a_rows.append(acc)
        return jnp.concatenate(a_rows, axis=0)

    def chunk(n, carry, *, pivoted):
        r0 = pl.multiple_of(n * C, C)
        rows = pl.ds(r0, C)
        outs = []
        for h in range(H):
            kl = slice(h * dk, (h + 1) * dk)
            vl = slice(h * dv, (h + 1) * dv)
            bn = bref_ref[n, :, kl]
            eb = jnp.exp(bn)
            st = st_ref[h]
            qt = qt_ref[rows, kl]
            kh = kh_ref[rows, kl]
            bs = bs_ref[rows, kl]
            v = v_ref[rows, vl]
            kraw = k_ref[rows, kl].astype(F32)

            qin = jnp.concatenate([qt[I * SUB:(I + 1) * SUB] * eb[I:I + 1] for I in range(NS)], axis=0)
            o = _dot_nt(qin.astype(BF16), st.astype(BF16))
            if pivoted:
                a = scores_pivoted(qin, kraw, bs, bn)
            else:
                a = scores_stable(qt, kh, q_ref[rows, kl].astype(F32) * scale, kraw, bs, bn)
            o = o + _dot(a.astype(BF16), v)

            k_st = jnp.concatenate(
                [kh[J * SUB:(J + 1) * SUB] * jnp.exp(bn[NS:NS + 1] - bn[J + 1:J + 2]) for J in range(NS)],
                axis=0)
            st_ref[h] = st * eb[NS:NS + 1] + _dot_tn(v, k_st.astype(BF16))
            outs.append(_rms(o, hnorm_ref[...]))

        g = g_ref[rows, :].astype(F32)
        o_ref[rows, :] = (jnp.concatenate(outs, axis=1) * (g * jax.nn.sigmoid(g))).astype(o_ref.dtype)
        return carry

    span = jnp.max(-bref_ref[...])

    @pl.when(span < GLA_SAFE_DECAY)
    def _():
        lax.fori_loop(0, tb // C, functools.partial(chunk, pivoted=True), 0, unroll=2)

    @pl.when(jnp.logical_not(span < GLA_SAFE_DECAY))
    def _():
        lax.fori_loop(0, tb // C, functools.partial(chunk, pivoted=False), 0)


def gla(proj, gl, w_gk2, b_gk, head_norm, *, dk, dv):
    T = proj.shape[0]
    H = GLA_HEADS
    tb = _tile(T, 512)
    assert tb % GLA_GROUP == 0 and dk % LANES == 0 and dv % LANES == 0
    kd, vd = H * dk, H * dv
    assert (2 * kd) % vd == 0
    v_off = (2 * kd) // vd
    fixed = lambda t: (0, 0)
    return pl.pallas_call(
        functools.partial(_gla_kernel, dk=dk, dv=dv, tb=tb),
        grid=(T // tb,),
        in_specs=[
            pl.BlockSpec((tb, kd), lambda t: (t, 0)),
            pl.BlockSpec((tb, kd), lambda t: (t, 1)),
            pl.BlockSpec((tb, vd), lambda t: (t, v_off)),
            pl.BlockSpec((tb, vd), lambda t: (t, v_off + 1)),
            pl.BlockSpec((tb, LANES), lambda t: (t, 0)),
            pl.BlockSpec((LANES, kd), fixed),
            pl.BlockSpec((1, kd), fixed),
            pl.BlockSpec((1, dv), fixed),
        ],
        out_specs=pl.BlockSpec((tb, vd), lambda t: (t, 0)),
        out_shape=jax.ShapeDtypeStruct((T, vd), BF16),
        scratch_shapes=[
            pltpu.VMEM((H, dv, dk), F32),
            pltpu.VMEM((tb, kd), F32),
            pltpu.VMEM((tb, kd), F32),
            pltpu.VMEM((tb, kd), F32),
            pltpu.VMEM((tb // GLA_CHUNK, SUBLANES, kd), F32),
        ],
        compiler_params=_params("arbitrary"),
        name="gla",
    )(proj, proj, proj, proj, gl, w_gk2, b_gk, head_norm)


def _route(hn, rhi_ref, rlo_ref, rb_ref):
    tm = hn.shape[0]
    hi = hn.astype(BF16)
    lo = (hn - hi.astype(F32)).astype(BF16)
    logits = _dot(hi, rhi_ref[...]) + _dot(hi, rlo_ref[...]) + _dot(lo, rhi_ref[...]) + rb_ref[...]
    lane = lax.broadcasted_iota(I32, (tm, LANES), 1)
    lane_f = lane.astype(F32)
    neg = -jnp.inf
    far = float(LANES)

    def first_max(vals):
        m = jnp.max(vals, axis=-1, keepdims=True)
        idx = jnp.min(jnp.where(vals == m, lane_f, far), axis=-1, keepdims=True)
        return m, idx

    lg = jnp.where((lane >= N_EXPERTS) & (lane < N_EXPERTS + N_GROUPS), logits, neg)
    mg, gidx = first_max(lg)
    top_gp = 1.0 / jnp.sum(jnp.exp(lg - mg), axis=-1, keepdims=True)
    grp = gidx.astype(I32) - N_EXPERTS
    le = jnp.where((lane < N_EXPERTS) & ((lane // EXPERTS_PER_GROUP) == grp), logits, neg)
    m1, i1 = first_max(le)
    m2, i2 = first_max(jnp.where(lane_f == i1, neg, le))
    e2 = jnp.exp(m2 - m1)
    w1 = 1.0 / (1.0 + e2)
    return i1.astype(I32), i2.astype(I32), top_gp * w1, top_gp * (e2 * w1)


def _out_router_kernel(*refs, conv):
    if conv:
        (pb_ref, pc_ref, ph_ref, hc_ref, hh_ref, cw_ref, w_ref, hin_ref, fnw_ref, rhi_ref, rlo_ref,
         rb_ref, h_ref, hn_ref, id0_ref, id1_ref, g0_ref, g1_ref, cnt_ref) = refs
    else:
        (y_ref, w_ref, hin_ref, fnw_ref, rhi_ref, rlo_ref,
         rb_ref, h_ref, hn_ref, id0_ref, id1_ref, g0_ref, g1_ref, cnt_ref) = refs
    i = pl.program_id(0)

    if conv:
        u = pc_ref[...].astype(F32) * ph_ref[...].astype(F32)
        prev = hc_ref[...].astype(F32) * hh_ref[...].astype(F32)
        prev = jnp.where(i > 0, prev, 0.0)
        row = lax.broadcasted_iota(I32, u.shape, 0)
        last, last2 = prev[BF16_ROWS - 1:BF16_ROWS], prev[BF16_ROWS - 2:BF16_ROWS - 1]
        u1 = jnp.where(row == 0, last, pltpu.roll(u, 1, 0))
        u2 = jnp.where(row == 0, last2, jnp.where(row == 1, last, pltpu.roll(u, 2, 0)))
        cw = cw_ref[...]
        y = (pb_ref[...].astype(F32) * (cw[0:1] * u2 + cw[1:2] * u1 + cw[2:3] * u)).astype(BF16)
    else:
        y = y_ref[...]

    h = hin_ref[...] + _dot(y, w_ref[...])
    h_ref[...] = h
    hn = _rms(h, fnw_ref[...])
    _store_packed(hn_ref, hn)
    i1, i2, g1, g2 = _route(hn, rhi_ref, rlo_ref, rb_ref)
    shape = id0_ref.shape
    id0_ref[...] = jnp.broadcast_to(i1, shape)
    id1_ref[...] = jnp.broadcast_to(i2, shape)
    g0_ref[...] = jnp.broadcast_to(g1, shape)
    g1_ref[...] = jnp.broadcast_to(g2, shape)

    @pl.when(i == 0)
    def _():
        cnt_ref[...] = jnp.zeros_like(cnt_ref)

    lane = lax.broadcasted_iota(I32, shape, 1)
    hit = jnp.where((lane == i1) | (lane == i2), 1.0, 0.0)
    cnt_ref[...] += jnp.sum(hit, axis=0, keepdims=True)


def out_router(y, w_out, h_in, ffn_norm_w, r_hi, r_lo, r_b, conv_w=None):
    T, D = h_in.shape
    K = w_out.shape[0]
    tm = _tile(T, 256)
    conv = conv_w is not None
    row = lambda i: (i, 0)
    fixed = lambda i: (0, 0)
    if conv:
        hb = tm // BF16_ROWS
        halo = lambda col: (lambda i: (jnp.maximum(i * hb - 1, 0), col))
        in_specs = [
            pl.BlockSpec((tm, K), lambda i: (i, 0)),
            pl.BlockSpec((tm, K), lambda i: (i, 1)),
            pl.BlockSpec((tm, K), lambda i: (i, 2)),
            pl.BlockSpec((BF16_ROWS, K), halo(1)),
            pl.BlockSpec((BF16_ROWS, K), halo(2)),
            pl.BlockSpec((SUBLANES, K), fixed),
        ]
        cw = jnp.zeros((SUBLANES, K), F32).at[:CONV_WIDTH].set(conv_w)
        args = [y, y, y, y, y, cw]
    else:
        in_specs = [pl.BlockSpec((tm, K), row)]
        args = [y]
    in_specs += [
        pl.BlockSpec((K, D), fixed),
        pl.BlockSpec((tm, D), row),
        pl.BlockSpec((1, D), fixed),
        pl.BlockSpec((D, LANES), fixed),
        pl.BlockSpec((D, LANES), fixed),
        pl.BlockSpec((1, LANES), fixed),
    ]
    args += [w_out, h_in, ffn_norm_w.reshape(1, D), r_hi, r_lo, r_b]
    wide = lambda dt: jax.ShapeDtypeStruct((T, LANES), dt)
    sub = _packed_sublanes(D)
    return pl.pallas_call(
        functools.partial(_out_router_kernel, conv=conv),
        grid=(T // tm,),
        in_specs=in_specs,
        out_specs=[pl.BlockSpec((tm, D), row), pl.BlockSpec((tm * sub, LANES), row)]
        + [pl.BlockSpec((tm, LANES), row)] * 4 + [pl.BlockSpec((1, LANES), fixed)],
        out_shape=[jax.ShapeDtypeStruct((T, D), F32), jax.ShapeDtypeStruct((T * sub, LANES), I32),
                   wide(I32), wide(I32), wide(F32), wide(F32), jax.ShapeDtypeStruct((1, LANES), F32)],
        compiler_params=_params("arbitrary"),
        name="out_router_conv" if conv else "out_router",
    )(*args)


def _rank_kernel(id0_ref, id1_ref, pst_ref, d0_ref, d1_ref, carry_ref):
    @pl.when(pl.program_id(0) == 0)
    def _():
        carry_ref[...] = jnp.zeros_like(carry_ref)

    tb = id0_ref.shape[0]
    lane = lax.broadcasted_iota(I32, (tb, LANES), 1)
    oh0 = lane == id0_ref[...]
    oh1 = lane == id1_ref[...]
    hit = jnp.where(oh0 | oh1, 1.0, 0.0)
    r = lax.broadcasted_iota(I32, (tb, tb), 0)
    c = lax.broadcasted_iota(I32, (tb, tb), 1)
    before = jnp.where(c < r, 1.0, 0.0).astype(BF16)
    base = _dot(before, hit.astype(BF16)) + carry_ref[...] + pst_ref[...]
    d0 = jnp.sum(jnp.where(oh0, base, 0.0), axis=-1, keepdims=True)
    d1 = jnp.sum(jnp.where(oh1, base, 0.0), axis=-1, keepdims=True)
    d0_ref[...] = jnp.broadcast_to(d0.astype(I32), (tb, LANES))
    d1_ref[...] = jnp.broadcast_to(d1.astype(I32), (tb, LANES))
    carry_ref[...] += jnp.sum(hit, axis=0, keepdims=True)


def rank(id0, id1, pstart_row):
    T = id0.shape[0]
    tb = _tile(T, 256)
    row = lambda i: (i, 0)
    return pl.pallas_call(
        _rank_kernel,
        grid=(T // tb,),
        in_specs=[pl.BlockSpec((tb, LANES), row), pl.BlockSpec((tb, LANES), row),
                  pl.BlockSpec((1, LANES), lambda i: (0, 0))],
        out_specs=[pl.BlockSpec((tb, LANES), row)] * 2,
        out_shape=[jax.ShapeDtypeStruct((T, LANES), I32)] * 2,
        scratch_shapes=[pltpu.VMEM((1, LANES), F32)],
        compiler_params=_params("arbitrary"),
        name="rank",
    )(id0, id1, pstart_row)


def _invert_kernel(d0_ref, d1_ref, fill_ref, rt_ref):
    def clear(p, carry):
        rt_ref[p] = 0
        return carry

    for e in range(fill_ref.shape[0] // 2):
        lax.fori_loop(fill_ref[2 * e], fill_ref[2 * e + 1], clear, 0)

    def place(t, carry):
        rt_ref[d0_ref[t]] = t
        rt_ref[d1_ref[t]] = t
        return carry

    lax.fori_loop(0, d0_ref.shape[0], place, 0, unroll=8)


def invert(d0, d1, fill_ranges, n_rows):
    smem = pl.BlockSpec(memory_space=pltpu.SMEM)
    return pl.pallas_call(
        _invert_kernel,
        in_specs=[smem, smem, smem],
        out_specs=smem,
        out_shape=jax.ShapeDtypeStruct((n_rows,), I32),
        name="invert",
    )(d0, d1, fill_ranges)


def _row_copy(src_hbm, dst_ref, sem, src_row, dst_row, sub):
    src = src_hbm.at[pl.ds(pl.multiple_of(src_row * sub, sub), sub), :]
    dst = dst_ref.at[pl.ds(pl.multiple_of(dst_row * sub, sub), sub), :]
    return pltpu.make_async_copy(src, dst, sem)


def _experts_kernel(bexp_ref, rtok_ref, nused_ref, hn_hbm, wgu_ref, wd_ref, y_ref, xbuf0, xbuf1, xbuf2, wgu_bf,
                    wd_bf, sem, *, ff, rb, sub):
    i = pl.program_id(0)
    n_used = nused_ref[0]
    bufs = (xbuf0, xbuf1, xbuf2)
    n_buf = len(bufs)
    assert n_buf == GATHER_AHEAD + 1

    def wait_rows(blk, s):
        def body(r, carry):
            _row_copy(hn_hbm, bufs[s], sem.at[s], rtok_ref[blk * rb + r], r, sub).wait()
            return carry

        lax.fori_loop(0, rb, body, 0, unroll=8)

    def start_rows(blk, s):
        for r in range(rb):
            _row_copy(hn_hbm, bufs[s], sem.at[s], rtok_ref[blk * rb + r], r, sub).start(priority=r % 2)

    @pl.when(i == 0)
    def _():
        for blk in range(GATHER_AHEAD):
            start_rows(blk, blk)

    @pl.when((i == 0) | (bexp_ref[i] != bexp_ref[jnp.maximum(i - 1, 0)]))
    def _():
        wgu_bf[...] = wgu_ref[0, 0].astype(BF16)
        wd_bf[...] = wd_ref[0, 0].astype(BF16)

    for s in range(n_buf):
        @pl.when((i < n_used) & (i % n_buf == s))
        def _():
            wait_rows(i, s)
            start_rows(i + GATHER_AHEAD, (s + GATHER_AHEAD) % n_buf)
            los, his = _load_packed(bufs[s], rb, sub)
            x = jnp.concatenate([p.astype(BF16) for p in los + his], axis=1)
            gu = _dot(x, wgu_bf[...])
            gt, up = gu[:, :ff], gu[:, ff:]
            act = (gt * jax.nn.sigmoid(gt) * up).astype(BF16)
            _store_packed(y_ref, _dot(act, wd_bf[...]))

        @pl.when((i >= n_used) & (i < n_used + GATHER_AHEAD) & (i % n_buf == s))
        def _():
            wait_rows(i, s)

    @pl.when(i >= n_used)
    def _():
        y_ref[...] = jnp.zeros_like(y_ref)


def experts(block_exp, row_tok, n_used, hn, w_gate_up, w_down, layer):
    D, ff2 = w_gate_up.shape[-2:]
    ff = ff2 // 2
    sub = _packed_sublanes(D)
    P = row_tok.shape[0]
    rb = ROW_BLOCK
    grid_spec = pltpu.PrefetchScalarGridSpec(
        num_scalar_prefetch=3,
        grid=(P // rb,),
        in_specs=[
            pl.BlockSpec(memory_space=pl.ANY),
            pl.BlockSpec((1, 1, D, ff2), lambda i, be, rt, nu: (layer, be[i], 0, 0)),
            pl.BlockSpec((1, 1, ff, D), lambda i, be, rt, nu: (layer, be[i], 0, 0)),
        ],
        out_specs=pl.BlockSpec((rb * sub, LANES), lambda i, be, rt, nu: (i, 0)),
        scratch_shapes=[pltpu.VMEM((rb * sub, LANES), I32)] * (GATHER_AHEAD + 1)
        + [pltpu.VMEM((D, ff2), BF16), pltpu.VMEM((ff, D), BF16), pltpu.SemaphoreType.DMA((GATHER_AHEAD + 1,))],
    )
    return pl.pallas_call(
        functools.partial(_experts_kernel, ff=ff, rb=rb, sub=sub),
        grid_spec=grid_spec,
        out_shape=jax.ShapeDtypeStruct((P * sub, LANES), I32),
        compiler_params=_params("arbitrary"),
        name="experts",
    )(block_exp, row_tok, n_used, hn, w_gate_up, w_down)


def _combine_kernel(d0_ref, d1_ref, yb_hbm, h_ref, g0_ref, g1_ref, nw_ref, o_ref, b00, b01, b10, b11, sem,
                    *, final_norm, sub):
    i = pl.program_id(0)
    n = pl.num_programs(0)
    tc = o_ref.shape[0]
    bufs = ((b00, b01), (b10, b11))
    d_refs = (d0_ref, d1_ref)

    def wait_rows(blk, s):
        def body(r, carry):
            for k in range(TOP_K):
                _row_copy(yb_hbm, bufs[s][k], sem.at[s], d_refs[k][blk * tc + r], r, sub).wait()
            return carry

        lax.fori_loop(0, tc, body, 0, unroll=8)

    def start_rows(blk, s):
        for r in range(tc):
            for k in range(TOP_K):
                _row_copy(yb_hbm, bufs[s][k], sem.at[s], d_refs[k][blk * tc + r], r, sub).start(priority=k)

    @pl.when(i == 0)
    def _():
        start_rows(0, 0)

    for s in range(2):
        @pl.when(i % 2 == s)
        def _():
            wait_rows(i, s)
            start_rows(i + 1, 1 - s)
            lo0, hi0 = _load_packed(bufs[s][0], tc, sub)
            lo1, hi1 = _load_packed(bufs[s][1], tc, sub)
            g0, g1 = g0_ref[...], g1_ref[...]
            pieces = [h_ref[:, j * LANES:(j + 1) * LANES] + g0 * y0 + g1 * y1
                      for j, (y0, y1) in enumerate(zip(lo0 + hi0, lo1 + hi1))]
            if final_norm:
                o_ref[...] = _rms(jnp.concatenate(pieces, axis=1), nw_ref[...])
            else:
                for j, piece in enumerate(pieces):
                    o_ref[:, j * LANES:(j + 1) * LANES] = piece

            @pl.when(i == n - 1)
            def _():
                wait_rows(i + 1, 1 - s)


def combine(d0, d1, yb, h, g0, g1, norm_w=None):
    T, D = h.shape
    tc = _tile(T, 256)
    sub = _packed_sublanes(D)
    final_norm = norm_w is not None
    nw = (norm_w if final_norm else jnp.ones((D,), F32)).reshape(1, D)
    spare = jnp.zeros((tc,), I32)
    row = lambda i, a, b: (i, 0)
    grid_spec = pltpu.PrefetchScalarGridSpec(
        num_scalar_prefetch=2,
        grid=(T // tc,),
        in_specs=[
            pl.BlockSpec(memory_space=pl.ANY),
            pl.BlockSpec((tc, D), row),
            pl.BlockSpec((tc, LANES), row),
            pl.BlockSpec((tc, LANES), row),
            pl.BlockSpec((1, D), lambda i, a, b: (0, 0)),
        ],
        out_specs=pl.BlockSpec((tc, D), row),
        scratch_shapes=[pltpu.VMEM((tc * sub, LANES), I32)] * (2 * TOP_K) + [pltpu.SemaphoreType.DMA((2,))],
    )
    return pl.pallas_call(
        functools.partial(_combine_kernel, final_norm=final_norm, sub=sub),
        grid_spec=grid_spec,
        out_shape=jax.ShapeDtypeStruct((T, D), F32),
        compiler_params=_params("arbitrary"),
        name="combine_norm" if final_norm else "combine",
    )(jnp.concatenate([d0, spare]), jnp.concatenate([d1, spare]), yb, h, g0, g1, nw)


def _pad_cols(w, n):
    return jnp.pad(w, ((0, 0), (0, n - w.shape[1])))


def moe(h, hn, id0, id1, g0, g1, counts, w_gate_up, w_down, layer, final_norm_w=None):
    T, D = h.shape
    rb = ROW_BLOCK
    n_blocks = -(-(T * TOP_K + N_EXPERTS * (rb - 1)) // rb) + GATHER_AHEAD
    cnt = counts[0, :N_EXPERTS].astype(I32)
    padded = ((cnt + rb - 1) // rb) * rb
    pends = jnp.cumsum(padded)
    pstart_row = _pad_cols((pends - padded).astype(F32).reshape(1, N_EXPERTS), LANES)
    d0b, d1b = rank(id0, id1, pstart_row)
    d0, d1 = d0b[:, 0], d1b[:, 0]
    fill_start = jnp.concatenate([pends - padded + cnt, pends[-1:]])
    fill_end = jnp.concatenate([pends, jnp.full((1,), n_blocks * rb, I32)])
    row_tok = invert(d0, d1, jnp.stack([fill_start, fill_end], axis=1).reshape(-1), n_blocks * rb)
    block_start = jnp.arange(n_blocks, dtype=I32) * rb
    block_exp = jnp.minimum(jnp.sum(pends[None, :] <= block_start[:, None], axis=1), N_EXPERTS - 1).astype(I32)
    n_used = (pends[-1:] // rb).astype(I32)
    yb = experts(block_exp, row_tok, n_used, hn, w_gate_up, w_down, layer)
    return combine(d0, d1, yb, h, g0, g1, final_norm_w)


def _router_weights(w_group, b_group, w_router, b_router):
    w = _pad_cols(jnp.concatenate([w_router, w_group], axis=1), LANES)
    b = _pad_cols(jnp.concatenate([b_router, b_group]).reshape(1, -1), LANES)
    hi = w.astype(BF16)
    lo = (w - hi.astype(F32)).astype(BF16)
    return hi, lo, b


def kernel(x, mix_norm, gla_w_in, gla_w_gk2, gla_b_gk, gla_head_norm, gla_w_out, conv_w_in, conv_w,
           conv_w_out, ffn_norm, w_group, b_group, w_router, b_router, w_gate_up, w_down, final_norm):
    B, T, D = x.shape
    assert B == 1, "the recurrence state is carried across the whole row axis"
    h = x.reshape(T, D)
    kd, vd = D // 2, D
    dk, dv = kd // GLA_HEADS, vd // GLA_HEADS

    w_in = gla_w_in[0]
    n_main = 2 * kd + 2 * vd
    proj, gl = norm_proj(h, mix_norm[0], w_in[:, :n_main].astype(BF16),
                         _pad_cols(w_in[:, n_main:], LANES).astype(BF16))
    w_gk2 = jnp.pad(gla_w_gk2[0], ((0, LANES - GATE_RANK), (0, 0))).astype(BF16)
    o = gla(proj, gl, w_gk2, gla_b_gk[0].reshape(1, kd), gla_head_norm[0].reshape(1, dv), dk=dk, dv=dv)
    routed = out_router(o, gla_w_out[0].astype(BF16), h, ffn_norm[0],
                        *_router_weights(w_group[0], b_group[0], w_router[0], b_router[0]))
    h = moe(*routed, w_gate_up, w_down, 0)

    proj = norm_proj(h, mix_norm[1], conv_w_in[0].astype(BF16))
    routed = out_router(proj, conv_w_out[0].astype(BF16), h, ffn_norm[1],
                        *_router_weights(w_group[1], b_group[1], w_router[1], b_router[1]),
                        conv_w=conv_w[0])
    out = moe(*routed, w_gate_up, w_down, 1, final_norm_w=final_norm)
    return out.reshape(B, T, D)
```

```python
import functools

import jax
import jax.numpy as jnp
from jax import lax
from jax.experimental import pallas as pl
from jax.experimental.pallas import tpu as pltpu

F32 = jnp.float32
BF16 = jnp.bfloat16
I32 = jnp.int32

EPS = 1e-6
GLA_HEADS = 4
GATE_RANK = 16
GATE_NORMALIZER = 16.0
CONV_WIDTH = 3
N_GROUPS = 4
EXPERTS_PER_GROUP = 8
N_EXPERTS = N_GROUPS * EXPERTS_PER_GROUP
TOP_K = 2

LANES = 128
SUBLANES = 8
BF16_ROWS = 16
HIGH_HALF = -65536
VMEM_LIMIT = 56 * 1024 * 1024

GLA_CHUNK = 64
GLA_SUB = 16
GLA_GROUP = 128
GLA_SAFE_DECAY = 60.0
ROW_BLOCK = 256
GATHER_AHEAD = 2


def _tile(n, pref):
    t = min(n, pref)
    assert n % t == 0, (n, t)
    return t


def _params(*sem):
    return pltpu.CompilerParams(dimension_semantics=sem, vmem_limit_bytes=VMEM_LIMIT)


def _dot(a, b):
    return jnp.dot(a, b, preferred_element_type=F32)


def _dot_nt(a, b):
    return lax.dot_general(a, b, (((1,), (1,)), ((), ())), preferred_element_type=F32)


def _dot_tn(a, b):
    return lax.dot_general(a, b, (((0,), (0,)), ((), ())), preferred_element_type=F32)


def _rms(x, w):
    return x * lax.rsqrt(jnp.mean(x * x, axis=-1, keepdims=True) + EPS) * w


def _packed_sublanes(d):
    assert d % (2 * LANES) == 0
    return d // (2 * LANES)


def _store_packed(ref, x):
    rows, d = x.shape
    half = d // 2
    sub = half // LANES
    lo = lax.shift_right_logical(lax.bitcast_convert_type(x[:, :half].astype(BF16).astype(F32), I32), 16)
    hi = lax.bitcast_convert_type(x[:, half:].astype(BF16).astype(F32), I32) & HIGH_HALF
    words = hi | lo
    for s in range(sub):
        ref[pl.ds(s, rows, stride=sub), :] = words[:, s * LANES:(s + 1) * LANES]


def _load_packed(ref, rows, sub):
    los, his = [], []
    for s in range(sub):
        w = ref[pl.ds(s, rows, stride=sub), :]
        los.append(lax.bitcast_convert_type(lax.shift_left(w, 16), F32))
        his.append(lax.bitcast_convert_type(w & HIGH_HALF, F32))
    return los, his


def _norm_proj_kernel(*refs, with_extra):
    if with_extra:
        x_ref, nw_ref, w_ref, we_ref, o_ref, e_ref, xn_ref = refs
    else:
        x_ref, nw_ref, w_ref, o_ref, xn_ref = refs

    @pl.when(pl.program_id(1) == 0)
    def _():
        xn_ref[...] = _rms(x_ref[...], nw_ref[...]).astype(BF16)
        if with_extra:
            e_ref[...] = _dot(xn_ref[...], we_ref[...])

    o_ref[...] = _dot(xn_ref[...], w_ref[...]).astype(o_ref.dtype)


def norm_proj(x, norm_w, w, w_extra=None):
    T, D = x.shape
    N = w.shape[1]
    tm, tn = _tile(T, 1024), _tile(N, 1024)
    in_specs = [
        pl.BlockSpec((tm, D), lambda i, j: (i, 0)),
        pl.BlockSpec((1, D), lambda i, j: (0, 0)),
        pl.BlockSpec((D, tn), lambda i, j: (0, j)),
    ]
    out_shape = [jax.ShapeDtypeStruct((T, N), BF16)]
    out_specs = [pl.BlockSpec((tm, tn), lambda i, j: (i, j))]
    args = [x, norm_w.reshape(1, D), w]
    if w_extra is not None:
        in_specs.append(pl.BlockSpec((D, LANES), lambda i, j: (0, 0)))
        out_shape.append(jax.ShapeDtypeStruct((T, LANES), F32))
        out_specs.append(pl.BlockSpec((tm, LANES), lambda i, j: (i, 0)))
        args.append(w_extra)
    out = pl.pallas_call(
        functools.partial(_norm_proj_kernel, with_extra=w_extra is not None),
        grid=(T // tm, N // tn),
        in_specs=in_specs,
        out_specs=out_specs,
        out_shape=out_shape,
        scratch_shapes=[pltpu.VMEM((tm, D), BF16)],
        compiler_params=_params("arbitrary", "arbitrary"),
        name="norm_proj",
    )(*args)
    return out if w_extra is not None else out[0]


def _gla_kernel(q_ref, k_ref, v_ref, g_ref, gl_ref, wgk_ref, bgk_ref, hnorm_ref, o_ref,
                st_ref, bs_ref, qt_ref, kh_ref, bref_ref, *, dk, dv, tb):
    C, SUB, GRP, H = GLA_CHUNK, GLA_SUB, GLA_GROUP, GLA_HEADS
    NS = C // SUB
    CPG = GRP // C
    kd = H * dk
    scale = dk ** -0.5

    @pl.when(pl.program_id(0) == 0)
    def _():
        st_ref[...] = jnp.zeros_like(st_ref)

    r = lax.broadcasted_iota(I32, (GRP, GRP), 0)
    c = lax.broadcasted_iota(I32, (GRP, GRP), 1)
    same = (r // SUB) == (c // SUB)
    low = jnp.where(same & (c <= r), 1.0, 0.0).astype(BF16)
    upp = jnp.where(same & (c > r), 1.0, 0.0).astype(BF16)
    rr = lax.broadcasted_iota(I32, (CPG * SUBLANES, GRP), 0)
    cc = lax.broadcasted_iota(I32, (CPG * SUBLANES, GRP), 1)
    blk = rr % SUBLANES
    mref = jnp.where(((cc // C) == (rr // SUBLANES)) & ((cc % C) < SUB * blk) & (blk <= NS),
                     1.0, 0.0).astype(BF16)

    for gi in range(tb // GRP):
        sl = pl.ds(gi * GRP, GRP)
        z = _dot(gl_ref[sl, :].astype(BF16), wgk_ref[...]) + bgk_ref[...]
        la = (jnp.minimum(z, 0.0) - jnp.log(1.0 + jnp.exp(-jnp.abs(z)))) * (1.0 / GATE_NORMALIZER)
        hi = la.astype(BF16)
        lo = (la - hi.astype(F32)).astype(BF16)
        bs = _dot(low, hi) + _dot(low, lo)
        ru = _dot(upp, hi) + _dot(upp, lo)
        br = _dot(mref, hi) + _dot(mref, lo)
        bs_ref[sl, :] = bs
        qt_ref[sl, :] = q_ref[sl, :].astype(F32) * jnp.exp(bs) * scale
        kh_ref[sl, :] = k_ref[sl, :].astype(F32) * jnp.exp(ru)
        bref_ref[pl.ds(gi * CPG, CPG)] = br.reshape(CPG, SUBLANES, kd)

    lane = lax.broadcasted_iota(I32, (SUB, C), 1)
    subrow = lax.broadcasted_iota(I32, (SUB, dk), 0)

    def scores_pivoted(qin, kraw, bs, bn):
        neg_b = jnp.concatenate([-(bs[J * SUB:(J + 1) * SUB] + bn[J:J + 1]) for J in range(NS)], axis=0)
        s = _dot_nt(qin.astype(BF16), (kraw * jnp.exp(neg_b)).astype(BF16))
        ri = lax.broadcasted_iota(I32, (C, C), 0)
        ci = lax.broadcasted_iota(I32, (C, C), 1)
        return jnp.where(ci <= ri, s, 0.0)

    def scores_stable(qt, kh, qraw, kraw, bs, bn):
        a_rows = []
        for I in range(NS):
            blk_rows = slice(I * SUB, (I + 1) * SUB)
            q_blk, k_blk, bs_blk = qraw[blk_rows], kraw[blk_rows], bs[blk_rows]
            acc = jnp.zeros((SUB, C), F32)
            for j in range(SUB):
                e = jnp.where(subrow >= j, jnp.exp(bs_blk - bs_blk[j:j + 1]), 0.0)
                p = (q_blk * k_blk[j:j + 1]) * e
                acc = jnp.where(lane == I * SUB + j, jnp.sum(p, axis=-1, keepdims=True), acc)
            if I > 0:
                parts = []
                for J in range(NS):
                    if J < I:
                        parts.append(kh[J * SUB:(J + 1) * SUB] * jnp.exp(bn[I:I + 1] - bn[J + 1:J + 2]))
                    else:
                        parts.append(jnp.zeros((SUB, dk), F32))
                k_dec = jnp.concatenate(parts, axis=0)
                acc = acc + _dot_nt(qt[blk_rows].astype(BF16), k_dec.astype(BF16))
            a_rows.append(acc)
        return jnp.concatenate(a_rows, axis=0)

    def chunk(n, carry, *, pivoted):
        r0 = pl.multiple_of(n * C, C)
        rows = pl.ds(r0, C)
        outs = []
        for h in range(H):
            kl = slice(h * dk, (h + 1) * dk)
            vl = slice(h * dv, (h + 1) * dv)
            bn = bref_ref[n, :, kl]
            eb = jnp.exp(bn)
            st = st_ref[h]
            qt = qt_ref[rows, kl]
            kh = kh_ref[rows, kl]
            bs = bs_ref[rows, kl]
            v = v_ref[rows, vl]
            kraw = k_ref[rows, kl].astype(F32)

            qin = jnp.concatenate([qt[I * SUB:(I + 1) * SUB] * eb[I:I + 1] for I in range(NS)], axis=0)
            o = _dot_nt(qin.astype(BF16), st.astype(BF16))
            if pivoted:
                a = scores_pivoted(qin, kraw, bs, bn)
            else:
                a = scores_stable(qt, kh, q_ref[rows, kl].astype(F32) * scale, kraw, bs, bn)
            o = o + _dot(a.astype(BF16), v)

            k_st = jnp.concatenate(
                [kh[J * SUB:(J + 1) * SUB] * jnp.exp(bn[NS:NS + 1] - bn[J + 1:J + 2]) for J in range(NS)],
                axis=0)
            st_ref[h] = st * eb[NS:NS + 1] + _dot_tn(v, k_st.astype(BF16))
            outs.append(_rms(o, hnorm_ref[...]))

        g = g_ref[rows, :].astype(F32)
        o_ref[rows, :] = (jnp.concatenate(outs, axis=1) * (g * jax.nn.sigmoid(g))).astype(o_ref.dtype)
        return carry

    span = jnp.max(-bref_ref[...])

    @pl.when(span < GLA_SAFE_DECAY)
    def _():
        lax.fori_loop(0, tb // C, functools.partial(chunk, pivoted=True), 0, unroll=2)

    @pl.when(jnp.logical_not(span < GLA_SAFE_DECAY))
    def _():
        lax.fori_loop(0, tb // C, functools.partial(chunk, pivoted=False), 0)


def gla(proj, gl, w_gk2, b_gk, head_norm, *, dk, dv):
    T = proj.shape[0]
    H = GLA_HEADS
    tb = _tile(T, 512)
    assert tb % GLA_GROUP == 0 and dk % LANES == 0 and dv % LANES == 0
    kd, vd = H * dk, H * dv
    assert (2 * kd) % vd == 0
    v_off = (2 * kd) // vd
    fixed = lambda t: (0, 0)
    return pl.pallas_call(
        functools.partial(_gla_kernel, dk=dk, dv=dv, tb=tb),
        grid=(T // tb,),
        in_specs=[
            pl.BlockSpec((tb, kd), lambda t: (t, 0)),
            pl.BlockSpec((tb, kd), lambda t: (t, 1)),
            pl.BlockSpec((tb, vd), lambda t: (t, v_off)),
            pl.BlockSpec((tb, vd), lambda t: (t, v_off + 1)),
            pl.BlockSpec((tb, LANES), lambda t: (t, 0)),
            pl.BlockSpec((LANES, kd), fixed),
            pl.BlockSpec((1, kd), fixed),
            pl.BlockSpec((1, dv), fixed),
        ],
        out_specs=pl.BlockSpec((tb, vd), lambda t: (t, 0)),
        out_shape=jax.ShapeDtypeStruct((T, vd), BF16),
        scratch_shapes=[
            pltpu.VMEM((H, dv, dk), F32),
            pltpu.VMEM((tb, kd), F32),
            pltpu.VMEM((tb, kd), F32),
            pltpu.VMEM((tb, kd), F32),
            pltpu.VMEM((tb // GLA_CHUNK, SUBLANES, kd), F32),
        ],
        compiler_params=_params("arbitrary"),
        name="gla",
    )(proj, proj, proj, proj, gl, w_gk2, b_gk, head_norm)


def _route(hn, rhi_ref, rlo_ref, rb_ref):
    tm = hn.shape[0]
    hi = hn.astype(BF16)
    lo = (hn - hi.astype(F32)).astype(BF16)
    logits = _dot(hi, rhi_ref[...]) + _dot(hi, rlo_ref[...]) + _dot(lo, rhi_ref[...]) + rb_ref[...]
    lane = lax.broadcasted_iota(I32, (tm, LANES), 1)
    lane_f = lane.astype(F32)
    neg = -jnp.inf
    far = float(LANES)

    def first_max(vals):
        m = jnp.max(vals, axis=-1, keepdims=True)
        idx = jnp.min(jnp.where(vals == m, lane_f, far), axis=-1, keepdims=True)
        return m, idx

    lg = jnp.where((lane >= N_EXPERTS) & (lane < N_EXPERTS + N_GROUPS), logits, neg)
    mg, gidx = first_max(lg)
    top_gp = 1.0 / jnp.sum(jnp.exp(lg - mg), axis=-1, keepdims=True)
    grp = gidx.astype(I32) - N_EXPERTS
    le = jnp.where((lane < N_EXPERTS) & ((lane // EXPERTS_PER_GROUP) == grp), logits, neg)
    m1, i1 = first_max(le)
    m2, i2 = first_max(jnp.where(lane_f == i1, neg, le))
    e2 = jnp.exp(m2 - m1)
    w1 = 1.0 / (1.0 + e2)
    return i1.astype(I32), i2.astype(I32), top_gp * w1, top_gp * (e2 * w1)


def _out_router_kernel(*refs, conv):
    if conv:
        (pb_ref, pc_ref, ph_ref, hc_ref, hh_ref, cw_ref, w_ref, hin_ref, fnw_ref, rhi_ref, rlo_ref,
         rb_ref, h_ref, hn_ref, id0_ref, id1_ref, g0_ref, g1_ref, cnt_ref) = refs
    else:
        (y_ref, w_ref, hin_ref, fnw_ref, rhi_ref, rlo_ref,
         rb_ref, h_ref, hn_ref, id0_ref, id1_ref, g0_ref, g1_ref, cnt_ref) = refs
    i = pl.program_id(0)

    if conv:
        u = pc_ref[...].astype(F32) * ph_ref[...].astype(F32)
        prev = hc_ref[...].astype(F32) * hh_ref[...].astype(F32)
        prev = jnp.where(i > 0, prev, 0.0)
        row = lax.broadcasted_iota(I32, u.shape, 0)
        last, last2 = prev[BF16_ROWS - 1:BF16_ROWS], prev[BF16_ROWS - 2:BF16_ROWS - 1]
        u1 = jnp.where(row == 0, last, pltpu.roll(u, 1, 0))
        u2 = jnp.where(row == 0, last2, jnp.where(row == 1, last, pltpu.roll(u, 2, 0)))
        cw = cw_ref[...]
        y = (pb_ref[...].astype(F32) * (cw[0:1] * u2 + cw[1:2] * u1 + cw[2:3] * u)).astype(BF16)
    else:
        y = y_ref[...]

    h = hin_ref[...] + _dot(y, w_ref[...])
    h_ref[...] = h
    hn = _rms(h, fnw_ref[...])
    _store_packed(hn_ref, hn)
    i1, i2, g1, g2 = _route(hn, rhi_ref, rlo_ref, rb_ref)
    shape = id0_ref.shape
    id0_ref[...] = jnp.broadcast_to(i1, shape)
    id1_ref[...] = jnp.broadcast_to(i2, shape)
    g0_ref[...] = jnp.broadcast_to(g1, shape)
    g1_ref[...] = jnp.broadcast_to(g2, shape)

    @pl.when(i == 0)
    def _():
        cnt_ref[...] = jnp.zeros_like(cnt_ref)

    lane = lax.broadcasted_iota(I32, shape, 1)
    hit = jnp.where((lane == i1) | (lane == i2), 1.0, 0.0)
    cnt_ref[...] += jnp.sum(hit, axis=0, keepdims=True)


def out_router(y, w_out, h_in, ffn_norm_w, r_hi, r_lo, r_b, conv_w=None):
    T, D = h_in.shape
    K = w_out.shape[0]
    tm = _tile(T, 256)
    conv = conv_w is not None
    row = lambda i: (i, 0)
    fixed = lambda i: (0, 0)
    if conv:
        hb = tm // BF16_ROWS
        halo = lambda col: (lambda i: (jnp.maximum(i * hb - 1, 0), col))
        in_specs = [
            pl.BlockSpec((tm, K), lambda i: (i, 0)),
            pl.BlockSpec((tm, K), lambda i: (i, 1)),
            pl.BlockSpec((tm, K), lambda i: (i, 2)),
            pl.BlockSpec((BF16_ROWS, K), halo(1)),
            pl.BlockSpec((BF16_ROWS, K), halo(2)),
            pl.BlockSpec((SUBLANES, K), fixed),
        ]
        cw = jnp.zeros((SUBLANES, K), F32).at[:CONV_WIDTH].set(conv_w)
        args = [y, y, y, y, y, cw]
    else:
        in_specs = [pl.BlockSpec((tm, K), row)]
        args = [y]
    in_specs += [
        pl.BlockSpec((K, D), fixed),
        pl.BlockSpec((tm, D), row),
        pl.BlockSpec((1, D), fixed),
        pl.BlockSpec((D, LANES), fixed),
        pl.BlockSpec((D, LANES), fixed),
        pl.BlockSpec((1, LANES), fixed),
    ]
    args += [w_out, h_in, ffn_norm_w.reshape(1, D), r_hi, r_lo, r_b]
    wide = lambda dt: jax.ShapeDtypeStruct((T, LANES), dt)
    sub = _packed_sublanes(D)
    return pl.pallas_call(
        functools.partial(_out_router_kernel, conv=conv),
        grid=(T // tm,),
        in_specs=in_specs,
        out_specs=[pl.BlockSpec((tm, D), row), pl.BlockSpec((tm * sub, LANES), row)]
        + [pl.BlockSpec((tm, LANES), row)] * 4 + [pl.BlockSpec((1, LANES), fixed)],
        out_shape=[jax.ShapeDtypeStruct((T, D), F32), jax.ShapeDtypeStruct((T * sub, LANES), I32),
                   wide(I32), wide(I32), wide(F32), wide(F32), jax.ShapeDtypeStruct((1, LANES), F32)],
        compiler_params=_params("arbitrary"),
        name="out_router_conv" if conv else "out_router",
    )(*args)


def _rank_kernel(id0_ref, id1_ref, pst_ref, d0_ref, d1_ref, carry_ref):
    @pl.when(pl.program_id(0) == 0)
    def _():
        carry_ref[...] = jnp.zeros_like(carry_ref)

    tb = id0_ref.shape[0]
    lane = lax.broadcasted_iota(I32, (tb, LANES), 1)
    oh0 = lane == id0_ref[...]
    oh1 = lane == id1_ref[...]
    hit = jnp.where(oh0 | oh1, 1.0, 0.0)
    r = lax.broadcasted_iota(I32, (tb, tb), 0)
    c = lax.broadcasted_iota(I32, (tb, tb), 1)
    before = jnp.where(c < r, 1.0, 0.0).astype(BF16)
    base = _dot(before, hit.astype(BF16)) + carry_ref[...] + pst_ref[...]
    d0 = jnp.sum(jnp.where(oh0, base, 0.0), axis=-1, keepdims=True)
    d1 = jnp.sum(jnp.where(oh1, base, 0.0), axis=-1, keepdims=True)
    d0_ref[...] = jnp.broadcast_to(d0.astype(I32), (tb, LANES))
    d1_ref[...] = jnp.broadcast_to(d1.astype(I32), (tb, LANES))
    carry_ref[...] += jnp.sum(hit, axis=0, keepdims=True)


def rank(id0, id1, pstart_row):
    T = id0.shape[0]
    tb = _tile(T, 256)
    row = lambda i: (i, 0)
    return pl.pallas_call(
        _rank_kernel,
        grid=(T // tb,),
        in_specs=[pl.BlockSpec((tb, LANES), row), pl.BlockSpec((tb, LANES), row),
                  pl.BlockSpec((1, LANES), lambda i: (0, 0))],
        out_specs=[pl.BlockSpec((tb, LANES), row)] * 2,
        out_shape=[jax.ShapeDtypeStruct((T, LANES), I32)] * 2,
        scratch_shapes=[pltpu.VMEM((1, LANES), F32)],
        compiler_params=_params("arbitrary"),
        name="rank",
    )(id0, id1, pstart_row)


def _invert_kernel(d0_ref, d1_ref, fill_ref, rt_ref):
    def clear(p, carry):
        rt_ref[p] = 0
        return carry

    for e in range(fill_ref.shape[0] // 2):
        lax.fori_loop(fill_ref[2 * e], fill_ref[2 * e + 1], clear, 0)

    def place(t, carry):
        rt_ref[d0_ref[t]] = t
        rt_ref[d1_ref[t]] = t
        return carry

    lax.fori_loop(0, d0_ref.shape[0], place, 0, unroll=8)


def invert(d0, d1, fill_ranges, n_rows):
    smem = pl.BlockSpec(memory_space=pltpu.SMEM)
    return pl.pallas_call(
        _invert_kernel,
        in_specs=[smem, smem, smem],
        out_specs=smem,
        out_shape=jax.ShapeDtypeStruct((n_rows,), I32),
        name="invert",
    )(d0, d1, fill_ranges)


def _row_copy(src_hbm, dst_ref, sem, src_row, dst_row, sub):
    src = src_hbm.at[pl.ds(pl.multiple_of(src_row * sub, sub), sub), :]
    dst = dst_ref.at[pl.ds(pl.multiple_of(dst_row * sub, sub), sub), :]
    return pltpu.make_async_copy(src, dst, sem)


def _experts_kernel(bexp_ref, first_ref, wslot_ref, next_ref, rtok_ref, nused_ref, hn_hbm, wgu_hbm, wd_hbm, y_ref,
                    xbuf0, xbuf1, xbuf2, wgu_f32, wd_f32, wgu_bf, wd_bf, sem, wsem, *, layer, ff, rb, sub):
    i = pl.program_id(0)
    n_used = nused_ref[0]
    bufs = (xbuf0, xbuf1, xbuf2)
    n_buf = len(bufs)
    assert n_buf == GATHER_AHEAD + 1

    def weight_copies(expert, slot):
        return (pltpu.make_async_copy(wgu_hbm.at[layer, expert], wgu_f32.at[slot], wsem.at[0, slot]),
                pltpu.make_async_copy(wd_hbm.at[layer, expert], wd_f32.at[slot], wsem.at[1, slot]))

    def wait_rows(blk, s):
        def body(r, carry):
            _row_copy(hn_hbm, bufs[s], sem.at[s], rtok_ref[blk * rb + r], r, sub).wait()
            return carry

        lax.fori_loop(0, rb, body, 0, unroll=8)

    def start_rows(blk, s):
        for r in range(rb):
            _row_copy(hn_hbm, bufs[s], sem.at[s], rtok_ref[blk * rb + r], r, sub).start(priority=r % 2)

    @pl.when(i == 0)
    def _():
        for cp in weight_copies(bexp_ref[0], 0):
            cp.start()
        for blk in range(GATHER_AHEAD):
            start_rows(blk, blk)

    @pl.when(first_ref[i] == 1)
    def _():
        slot = wslot_ref[i]
        for cp in weight_copies(bexp_ref[i], slot):
            cp.wait()
        nxt = next_ref[i]

        @pl.when(nxt >= 0)
        def _():
            for cp in weight_copies(nxt, 1 - slot):
                cp.start()

        wgu_bf[...] = wgu_f32[slot].astype(BF16)
        wd_bf[...] = wd_f32[slot].astype(BF16)

    for s in range(n_buf):
        @pl.when((i < n_used) & (i % n_buf == s))
        def _():
            wait_rows(i, s)
            start_rows(i + GATHER_AHEAD, (s + GATHER_AHEAD) % n_buf)
            los, his = _load_packed(bufs[s], rb, sub)
            x = jnp.concatenate([p.astype(BF16) for p in los + his], axis=1)
            gu = _dot(x, wgu_bf[...])
            gt, up = gu[:, :ff], gu[:, ff:]
            act = (gt * jax.nn.sigmoid(gt) * up).astype(BF16)
            _store_packed(y_ref, _dot(act, wd_bf[...]))

        @pl.when((i >= n_used) & (i < n_used + GATHER_AHEAD) & (i % n_buf == s))
        def _():
            wait_rows(i, s)

    @pl.when(i >= n_used)
    def _():
        y_ref[...] = jnp.zeros_like(y_ref)


def experts(block_exp, row_tok, n_used, hn, w_gate_up, w_down, layer):
    D, ff2 = w_gate_up.shape[-2:]
    ff = ff2 // 2
    sub = _packed_sublanes(D)
    P = row_tok.shape[0]
    rb = ROW_BLOCK
    n_blocks = P // rb

    blk = jnp.arange(n_blocks, dtype=I32)
    prev = jnp.concatenate([jnp.full((1,), -1, I32), block_exp[:-1]])
    first = (blk < n_used[0]) & (block_exp != prev)
    wslot = (jnp.cumsum(first.astype(I32)) - 1) % 2
    first_at = lax.cummin(jnp.where(first, blk, n_blocks), reverse=True)
    next_first = jnp.concatenate([first_at[1:], jnp.full((1,), n_blocks, I32)])
    next_exp = jnp.where(next_first < n_blocks, block_exp[jnp.minimum(next_first, n_blocks - 1)], -1)

    grid_spec = pltpu.PrefetchScalarGridSpec(
        num_scalar_prefetch=6,
        grid=(n_blocks,),
        in_specs=[pl.BlockSpec(memory_space=pl.ANY)] * 3,
        out_specs=pl.BlockSpec((rb * sub, LANES), lambda i, *_: (i, 0)),
        scratch_shapes=[pltpu.VMEM((rb * sub, LANES), I32)] * (GATHER_AHEAD + 1)
        + [pltpu.VMEM((2, D, ff2), F32), pltpu.VMEM((2, ff, D), F32),
           pltpu.VMEM((D, ff2), BF16), pltpu.VMEM((ff, D), BF16),
           pltpu.SemaphoreType.DMA((GATHER_AHEAD + 1,)), pltpu.SemaphoreType.DMA((2, 2))],
    )
    return pl.pallas_call(
        functools.partial(_experts_kernel, layer=layer, ff=ff, rb=rb, sub=sub),
        grid_spec=grid_spec,
        out_shape=jax.ShapeDtypeStruct((P * sub, LANES), I32),
        compiler_params=_params("arbitrary"),
        name="experts",
    )(block_exp, first.astype(I32), wslot.astype(I32), next_exp.astype(I32), row_tok, n_used,
      hn, w_gate_up, w_down)


def _combine_kernel(d0_ref, d1_ref, yb_hbm, h_ref, g0_ref, g1_ref, nw_ref, o_ref, b00, b01, b10, b11, sem,
                    *, final_norm, sub):
    i = pl.program_id(0)
    n = pl.num_programs(0)
    tc = o_ref.shape[0]
    bufs = ((b00, b01), (b10, b11))
    d_refs = (d0_ref, d1_ref)

    def wait_rows(blk, s):
        def body(r, carry):
            for k in range(TOP_K):
                _row_copy(yb_hbm, bufs[s][k], sem.at[s], d_refs[k][blk * tc + r], r, sub).wait()
            return carry

        lax.fori_loop(0, tc, body, 0, unroll=8)

    def start_rows(blk, s):
        for r in range(tc):
            for k in range(TOP_K):
                _row_copy(yb_hbm, bufs[s][k], sem.at[s], d_refs[k][blk * tc + r], r, sub).start(priority=k)

    @pl.when(i == 0)
    def _():
        start_rows(0, 0)

    for s in range(2):
        @pl.when(i % 2 == s)
        def _():
            wait_rows(i, s)
            start_rows(i + 1, 1 - s)
            lo0, hi0 = _load_packed(bufs[s][0], tc, sub)
            lo1, hi1 = _load_packed(bufs[s][1], tc, sub)
            g0, g1 = g0_ref[...], g1_ref[...]
            pieces = [h_ref[:, j * LANES:(j + 1) * LANES] + g0 * y0 + g1 * y1
                      for j, (y0, y1) in enumerate(zip(lo0 + hi0, lo1 + hi1))]
            if final_norm:
                o_ref[...] = _rms(jnp.concatenate(pieces, axis=1), nw_ref[...])
            else:
                for j, piece in enumerate(pieces):
                    o_ref[:, j * LANES:(j + 1) * LANES] = piece

            @pl.when(i == n - 1)
            def _():
                wait_rows(i + 1, 1 - s)


def combine(d0, d1, yb, h, g0, g1, norm_w=None):
    T, D = h.shape
    tc = _tile(T, 256)
    sub = _packed_sublanes(D)
    final_norm = norm_w is not None
    nw = (norm_w if final_norm else jnp.ones((D,), F32)).reshape(1, D)
    spare = jnp.zeros((tc,), I32)
    row = lambda i, a, b: (i, 0)
    grid_spec = pltpu.PrefetchScalarGridSpec(
        num_scalar_prefetch=2,
        grid=(T // tc,),
        in_specs=[
            pl.BlockSpec(memory_space=pl.ANY),
            pl.BlockSpec((tc, D), row),
            pl.BlockSpec((tc, LANES), row),
            pl.BlockSpec((tc, LANES), row),
            pl.BlockSpec((1, D), lambda i, a, b: (0, 0)),
        ],
        out_specs=pl.BlockSpec((tc, D), row),
        scratch_shapes=[pltpu.VMEM((tc * sub, LANES), I32)] * (2 * TOP_K) + [pltpu.SemaphoreType.DMA((2,))],
    )
    return pl.pallas_call(
        functools.partial(_combine_kernel, final_norm=final_norm, sub=sub),
        grid_spec=grid_spec,
        out_shape=jax.ShapeDtypeStruct((T, D), F32),
        compiler_params=_params("arbitrary"),
        name="combine_norm" if final_norm else "combine",
    )(jnp.concatenate([d0, spare]), jnp.concatenate([d1, spare]), yb, h, g0, g1, nw)


def _pad_cols(w, n):
    return jnp.pad(w, ((0, 0), (0, n - w.shape[1])))


def moe(h, hn, id0, id1, g0, g1, counts, w_gate_up, w_down, layer, final_norm_w=None):
    T, D = h.shape
    rb = ROW_BLOCK
    n_blocks = -(-(T * TOP_K + N_EXPERTS * (rb - 1)) // rb) + GATHER_AHEAD
    cnt = counts[0, :N_EXPERTS].astype(I32)
    padded = ((cnt + rb - 1) // rb) * rb
    pends = jnp.cumsum(padded)
    pstart_row = _pad_cols((pends - padded).astype(F32).reshape(1, N_EXPERTS), LANES)
    d0b, d1b = rank(id0, id1, pstart_row)
    d0, d1 = d0b[:, 0], d1b[:, 0]
    fill_start = jnp.concatenate([pends - padded + cnt, pends[-1:]])
    fill_end = jnp.concatenate([pends, jnp.full((1,), n_blocks * rb, I32)])
    row_tok = invert(d0, d1, jnp.stack([fill_start, fill_end], axis=1).reshape(-1), n_blocks * rb)
    block_start = jnp.arange(n_blocks, dtype=I32) * rb
    block_exp = jnp.minimum(jnp.sum(pends[None, :] <= block_start[:, None], axis=1), N_EXPERTS - 1).astype(I32)
    n_used = (pends[-1:] // rb).astype(I32)
    yb = experts(block_exp, row_tok, n_used, hn, w_gate_up, w_down, layer)
    return combine(d0, d1, yb, h, g0, g1, final_norm_w)


def _router_weights(w_group, b_group, w_router, b_router):
    w = _pad_cols(jnp.concatenate([w_router, w_group], axis=1), LANES)
    b = _pad_cols(jnp.concatenate([b_router, b_group]).reshape(1, -1), LANES)
    hi = w.astype(BF16)
    lo = (w - hi.astype(F32)).astype(BF16)
    return hi, lo, b


def kernel(x, mix_norm, gla_w_in, gla_w_gk2, gla_b_gk, gla_head_norm, gla_w_out, conv_w_in, conv_w,
           conv_w_out, ffn_norm, w_group, b_group, w_router, b_router, w_gate_up, w_down, final_norm):
    B, T, D = x.shape
    assert B == 1, "the recurrence state is carried across the whole row axis"
    h = x.reshape(T, D)
    kd, vd = D // 2, D
    dk, dv = kd // GLA_HEADS, vd // GLA_HEADS

    w_in = gla_w_in[0]
    n_main = 2 * kd + 2 * vd
    proj, gl = norm_proj(h, mix_norm[0], w_in[:, :n_main].astype(BF16),
                         _pad_cols(w_in[:, n_main:], LANES).astype(BF16))
    w_gk2 = jnp.pad(gla_w_gk2[0], ((0, LANES - GATE_RANK), (0, 0))).astype(BF16)
    o = gla(proj, gl, w_gk2, gla_b_gk[0].reshape(1, kd), gla_head_norm[0].reshape(1, dv), dk=dk, dv=dv)
    routed = out_router(o, gla_w_out[0].astype(BF16), h, ffn_norm[0],
                        *_router_weights(w_group[0], b_group[0], w_router[0], b_router[0]))
    h = moe(*routed, w_gate_up, w_down, 0)

    proj = norm_proj(h, mix_norm[1], conv_w_in[0].astype(BF16))
    routed = out_router(proj, conv_w_out[0].astype(BF16), h, ffn_norm[1],
                        *_router_weights(w_group[1], b_group[1], w_router[1], b_router[1]),
                        conv_w=conv_w[0])
    out = moe(*routed, w_gate_up, w_down, 1, final_norm_w=final_norm)
    return out.reshape(B, T, D)
```

```python
import functools

import jax
import jax.numpy as jnp
from jax import lax
from jax.experimental import pallas as pl
from jax.experimental.pallas import tpu as pltpu

F32 = jnp.float32
BF16 = jnp.bfloat16
I32 = jnp.int32

EPS = 1e-6
GLA_HEADS = 4
GATE_RANK = 16
GATE_NORMALIZER = 16.0
CONV_WIDTH = 3
N_GROUPS = 4
EXPERTS_PER_GROUP = 8
N_EXPERTS = N_GROUPS * EXPERTS_PER_GROUP
TOP_K = 2

LANES = 128
SUBLANES = 8
BF16_ROWS = 16
HIGH_HALF = -65536
VMEM_LIMIT = 56 * 1024 * 1024

GLA_CHUNK = 64
GLA_SUB = 16
GLA_GROUP = 128
GLA_SAFE_DECAY = 60.0
ROW_BLOCK = 256
GATHER_AHEAD = 2
ROW_QUEUE, WEIGHT_QUEUE = 0, 1


def _tile(n, pref):
    t = min(n, pref)
    assert n % t == 0, (n, t)
    return t


def _params(*sem):
    return pltpu.CompilerParams(dimension_semantics=sem, vmem_limit_bytes=VMEM_LIMIT)


def _dot(a, b):
    return jnp.dot(a, b, preferred_element_type=F32)


def _dot_nt(a, b):
    return lax.dot_general(a, b, (((1,), (1,)), ((), ())), preferred_element_type=F32)


def _dot_tn(a, b):
    return lax.dot_general(a, b, (((0,), (0,)), ((), ())), preferred_element_type=F32)


def _rms(x, w):
    return x * lax.rsqrt(jnp.mean(x * x, axis=-1, keepdims=True) + EPS) * w


def _packed_sublanes(d):
    assert d % (2 * LANES) == 0
    return d // (2 * LANES)


def _store_packed(ref, x):
    rows, d = x.shape
    half = d // 2
    sub = half // LANES
    lo = lax.shift_right_logical(lax.bitcast_convert_type(x[:, :half].astype(BF16).astype(F32), I32), 16)
    hi = lax.bitcast_convert_type(x[:, half:].astype(BF16).astype(F32), I32) & HIGH_HALF
    words = hi | lo
    for s in range(sub):
        ref[pl.ds(s, rows, stride=sub), :] = words[:, s * LANES:(s + 1) * LANES]


def _load_packed(ref, rows, sub):
    los, his = [], []
    for s in range(sub):
        w = ref[pl.ds(s, rows, stride=sub), :]
        los.append(lax.bitcast_convert_type(lax.shift_left(w, 16), F32))
        his.append(lax.bitcast_convert_type(w & HIGH_HALF, F32))
    return los, his


def _norm_proj_kernel(*refs, with_extra):
    if with_extra:
        x_ref, nw_ref, w_ref, we_ref, o_ref, e_ref, xn_ref = refs
    else:
        x_ref, nw_ref, w_ref, o_ref, xn_ref = refs

    @pl.when(pl.program_id(1) == 0)
    def _():
        xn_ref[...] = _rms(x_ref[...], nw_ref[...]).astype(BF16)
        if with_extra:
            e_ref[...] = _dot(xn_ref[...], we_ref[...])

    o_ref[...] = _dot(xn_ref[...], w_ref[...]).astype(o_ref.dtype)


def norm_proj(x, norm_w, w, w_extra=None):
    T, D = x.shape
    N = w.shape[1]
    tm, tn = _tile(T, 1024), _tile(N, 1024)
    in_specs = [
        pl.BlockSpec((tm, D), lambda i, j: (i, 0)),
        pl.BlockSpec((1, D), lambda i, j: (0, 0)),
        pl.BlockSpec((D, tn), lambda i, j: (0, j)),
    ]
    out_shape = [jax.ShapeDtypeStruct((T, N), BF16)]
    out_specs = [pl.BlockSpec((tm, tn), lambda i, j: (i, j))]
    args = [x, norm_w.reshape(1, D), w]
    if w_extra is not None:
        in_specs.append(pl.BlockSpec((D, LANES), lambda i, j: (0, 0)))
        out_shape.append(jax.ShapeDtypeStruct((T, LANES), F32))
        out_specs.append(pl.BlockSpec((tm, LANES), lambda i, j: (i, 0)))
        args.append(w_extra)
    out = pl.pallas_call(
        functools.partial(_norm_proj_kernel, with_extra=w_extra is not None),
        grid=(T // tm, N // tn),
        in_specs=in_specs,
        out_specs=out_specs,
        out_shape=out_shape,
        scratch_shapes=[pltpu.VMEM((tm, D), BF16)],
        compiler_params=_params("arbitrary", "arbitrary"),
        name="norm_proj",
    )(*args)
    return out if w_extra is not None else out[0]


def _gla_kernel(q_ref, k_ref, v_ref, g_ref, gl_ref, wgk_ref, bgk_ref, hnorm_ref, o_ref,
                st_ref, bs_ref, qt_ref, kh_ref, bref_ref, *, dk, dv, tb):
    C, SUB, GRP, H = GLA_CHUNK, GLA_SUB, GLA_GROUP, GLA_HEADS
    NS = C // SUB
    CPG = GRP // C
    kd = H * dk
    scale = dk ** -0.5

    @pl.when(pl.program_id(0) == 0)
    def _():
        st_ref[...] = jnp.zeros_like(st_ref)

    r = lax.broadcasted_iota(I32, (GRP, GRP), 0)
    c = lax.broadcasted_iota(I32, (GRP, GRP), 1)
    same = (r // SUB) == (c // SUB)
    low = jnp.where(same & (c <= r), 1.0, 0.0).astype(BF16)
    upp = jnp.where(same & (c > r), 1.0, 0.0).astype(BF16)
    rr = lax.broadcasted_iota(I32, (CPG * SUBLANES, GRP), 0)
    cc = lax.broadcasted_iota(I32, (CPG * SUBLANES, GRP), 1)
    blk = rr % SUBLANES
    mref = jnp.where(((cc // C) == (rr // SUBLANES)) & ((cc % C) < SUB * blk) & (blk <= NS),
                     1.0, 0.0).astype(BF16)

    for gi in range(tb // GRP):
        sl = pl.ds(gi * GRP, GRP)
        z = _dot(gl_ref[sl, :].astype(BF16), wgk_ref[...]) + bgk_ref[...]
        la = (jnp.minimum(z, 0.0) - jnp.log(1.0 + jnp.exp(-jnp.abs(z)))) * (1.0 / GATE_NORMALIZER)
        hi = la.astype(BF16)
        lo = (la - hi.astype(F32)).astype(BF16)
        bs = _dot(low, hi) + _dot(low, lo)
        ru = _dot(upp, hi) + _dot(upp, lo)
        br = _dot(mref, hi) + _dot(mref, lo)
        bs_ref[sl, :] = bs
        qt_ref[sl, :] = q_ref[sl, :].astype(F32) * jnp.exp(bs) * scale
        kh_ref[sl, :] = k_ref[sl, :].astype(F32) * jnp.exp(ru)
        bref_ref[pl.ds(gi * CPG, CPG)] = br.reshape(CPG, SUBLANES, kd)

    lane = lax.broadcasted_iota(I32, (SUB, C), 1)
    subrow = lax.broadcasted_iota(I32, (SUB, dk), 0)

    def scores_pivoted(qin, kraw, bs, bn):
        neg_b = jnp.concatenate([-(bs[J * SUB:(J + 1) * SUB] + bn[J:J + 1]) for J in range(NS)], axis=0)
        s = _dot_nt(qin.astype(BF16), (kraw * jnp.exp(neg_b)).astype(BF16))
        ri = lax.broadcasted_iota(I32, (C, C), 0)
        ci = lax.broadcasted_iota(I32, (C, C), 1)
        return jnp.where(ci <= ri, s, 0.0)

    def scores_stable(qt, kh, qraw, kraw, bs, bn):
        a_rows = []
        for I in range(NS):
            blk_rows = slice(I * SUB, (I + 1) * SUB)
            q_blk, k_blk, bs_blk = qraw[blk_rows], kraw[blk_rows], bs[blk_rows]
            acc = jnp.zeros((SUB, C), F32)
            for j in range(SUB):
                e = jnp.where(subrow >= j, jnp.exp(bs_blk - bs_blk[j:j + 1]), 0.0)
                p = (q_blk * k_blk[j:j + 1]) * e
                acc = jnp.where(lane == I * SUB + j, jnp.sum(p, axis=-1, keepdims=True), acc)
            if I > 0:
                parts = []
                for J in range(NS):
                    if J < I:
                        parts.append(kh[J * SUB:(J + 1) * SUB] * jnp.exp(bn[I:I + 1] - bn[J + 1:J + 2]))
                    else:
                        parts.append(jnp.zeros((SUB, dk), F32))
                k_dec = jnp.concatenate(parts, axis=0)
                acc = acc + _dot_nt(qt[blk_rows].astype(BF16), k_dec.astype(BF16))
            a_rows.append(acc)
        return jnp.concatenate(a_rows, axis=0)

    def chunk(n, carry, *, pivoted):
        r0 = pl.multiple_of(n * C, C)
        rows = pl.ds(r0, C)
        outs = []
        for h in range(H):
            kl = slice(h * dk, (h + 1) * dk)
            vl = slice(h * dv, (h + 1) * dv)
            bn = bref_ref[n, :, kl]
            eb = jnp.exp(bn)
            st = st_ref[h]
            qt = qt_ref[rows, kl]
            kh = kh_ref[rows, kl]
            bs = bs_ref[rows, kl]
            v = v_ref[rows, vl]
            kraw = k_ref[rows, kl].astype(F32)

            qin = jnp.concatenate([qt[I * SUB:(I + 1) * SUB] * eb[I:I + 1] for I in range(NS)], axis=0)
            o = _dot_nt(qin.astype(BF16), st.astype(BF16))
            if pivoted:
                a = scores_pivoted(qin, kraw, bs, bn)
            else:
                a = scores_stable(qt, kh, q_ref[rows, kl].astype(F32) * scale, kraw, bs, bn)
            o = o + _dot(a.astype(BF16), v)

            k_st = jnp.concatenate(
                [kh[J * SUB:(J + 1) * SUB] * jnp.exp(bn[NS:NS + 1] - bn[J + 1:J + 2]) for J in range(NS)],
                axis=0)
            st_ref[h] = st * eb[NS:NS + 1] + _dot_tn(v, k_st.astype(BF16))
            outs.append(_rms(o, hnorm_ref[...]))

        g = g_ref[rows, :].astype(F32)
        o_ref[rows, :] = (jnp.concatenate(outs, axis=1) * (g * jax.nn.sigmoid(g))).astype(o_ref.dtype)
        return carry

    span = jnp.max(-bref_ref[...])

    @pl.when(span < GLA_SAFE_DECAY)
    def _():
        lax.fori_loop(0, tb // C, functools.partial(chunk, pivoted=True), 0, unroll=2)

    @pl.when(jnp.logical_not(span < GLA_SAFE_DECAY))
    def _():
        lax.fori_loop(0, tb // C, functools.partial(chunk, pivoted=False), 0)


def gla(proj, gl, w_gk2, b_gk, head_norm, *, dk, dv):
    T = proj.shape[0]
    H = GLA_HEADS
    tb = _tile(T, 512)
    assert tb % GLA_GROUP == 0 and dk % LANES == 0 and dv % LANES == 0
    kd, vd = H * dk, H * dv
    assert (2 * kd) % vd == 0
    v_off = (2 * kd) // vd
    fixed = lambda t: (0, 0)
    return pl.pallas_call(
        functools.partial(_gla_kernel, dk=dk, dv=dv, tb=tb),
        grid=(T // tb,),
        in_specs=[
            pl.BlockSpec((tb, kd), lambda t: (t, 0)),
            pl.BlockSpec((tb, kd), lambda t: (t, 1)),
            pl.BlockSpec((tb, vd), lambda t: (t, v_off)),
            pl.BlockSpec((tb, vd), lambda t: (t, v_off + 1)),
            pl.BlockSpec((tb, LANES), lambda t: (t, 0)),
            pl.BlockSpec((LANES, kd), fixed),
            pl.BlockSpec((1, kd), fixed),
            pl.BlockSpec((1, dv), fixed),
        ],
        out_specs=pl.BlockSpec((tb, vd), lambda t: (t, 0)),
        out_shape=jax.ShapeDtypeStruct((T, vd), BF16),
        scratch_shapes=[
            pltpu.VMEM((H, dv, dk), F32),
            pltpu.VMEM((tb, kd), F32),
            pltpu.VMEM((tb, kd), F32),
            pltpu.VMEM((tb, kd), F32),
            pltpu.VMEM((tb // GLA_CHUNK, SUBLANES, kd), F32),
        ],
        compiler_params=_params("arbitrary"),
        name="gla",
    )(proj, proj, proj, proj, gl, w_gk2, b_gk, head_norm)


def _route(hn, rw_ref, rb_ref):
    tm = hn.shape[0]
    hi = hn.astype(BF16)
    lo = (hn - hi.astype(F32)).astype(BF16)
    both = _dot(hi, rw_ref[...])
    logits = both[:, :LANES] + both[:, LANES:] + _dot(lo, rw_ref[:, :LANES]) + rb_ref[...]
    lane = lax.broadcasted_iota(I32, (tm, LANES), 1)
    lane_f = lane.astype(F32)
    neg = -jnp.inf
    far = float(LANES)

    def first_max(vals):
        m = jnp.max(vals, axis=-1, keepdims=True)
        idx = jnp.min(jnp.where(vals == m, lane_f, far), axis=-1, keepdims=True)
        return m, idx

    lg = jnp.where((lane >= N_EXPERTS) & (lane < N_EXPERTS + N_GROUPS), logits, neg)
    mg, gidx = first_max(lg)
    top_gp = 1.0 / jnp.sum(jnp.exp(lg - mg), axis=-1, keepdims=True)
    grp = gidx.astype(I32) - N_EXPERTS
    le = jnp.where((lane < N_EXPERTS) & ((lane // EXPERTS_PER_GROUP) == grp), logits, neg)
    m1, i1 = first_max(le)
    m2, i2 = first_max(jnp.where(lane_f == i1, neg, le))
    e2 = jnp.exp(m2 - m1)
    w1 = 1.0 / (1.0 + e2)
    return i1.astype(I32), i2.astype(I32), top_gp * w1, top_gp * (e2 * w1)


def _out_router_kernel(*refs, conv):
    if conv:
        (pb_ref, pc_ref, ph_ref, hc_ref, hh_ref, cw_ref, w_ref, hin_ref, fnw_ref, rw_ref,
         rb_ref, h_ref, hn_ref, id0_ref, id1_ref, g0_ref, g1_ref, cnt_ref) = refs
    else:
        (y_ref, w_ref, hin_ref, fnw_ref, rw_ref,
         rb_ref, h_ref, hn_ref, id0_ref, id1_ref, g0_ref, g1_ref, cnt_ref) = refs
    i = pl.program_id(0)

    if conv:
        u = pc_ref[...].astype(F32) * ph_ref[...].astype(F32)
        prev = hc_ref[...].astype(F32) * hh_ref[...].astype(F32)
        prev = jnp.where(i > 0, prev, 0.0)
        row = lax.broadcasted_iota(I32, u.shape, 0)
        last, last2 = prev[BF16_ROWS - 1:BF16_ROWS], prev[BF16_ROWS - 2:BF16_ROWS - 1]
        u1 = jnp.where(row == 0, last, pltpu.roll(u, 1, 0))
        u2 = jnp.where(row == 0, last2, jnp.where(row == 1, last, pltpu.roll(u, 2, 0)))
        cw = cw_ref[...]
        y = (pb_ref[...].astype(F32) * (cw[0:1] * u2 + cw[1:2] * u1 + cw[2:3] * u)).astype(BF16)
    else:
        y = y_ref[...]

    h = hin_ref[...] + _dot(y, w_ref[...])
    h_ref[...] = h
    hn = _rms(h, fnw_ref[...])
    _store_packed(hn_ref, hn)
    i1, i2, g1, g2 = _route(hn, rw_ref, rb_ref)
    shape = id0_ref.shape
    id0_ref[...] = jnp.broadcast_to(i1, shape)
    id1_ref[...] = jnp.broadcast_to(i2, shape)
    g0_ref[...] = jnp.broadcast_to(g1, shape)
    g1_ref[...] = jnp.broadcast_to(g2, shape)

    @pl.when(i == 0)
    def _():
        cnt_ref[...] = jnp.zeros_like(cnt_ref)

    lane = lax.broadcasted_iota(I32, shape, 1)
    hit = jnp.where((lane == i1) | (lane == i2), 1.0, 0.0)
    cnt_ref[...] += jnp.sum(hit, axis=0, keepdims=True)


def out_router(y, w_out, h_in, ffn_norm_w, r_w, r_b, conv_w=None):
    T, D = h_in.shape
    K = w_out.shape[0]
    tm = _tile(T, 256)
    conv = conv_w is not None
    row = lambda i: (i, 0)
    fixed = lambda i: (0, 0)
    if conv:
        hb = tm // BF16_ROWS
        halo = lambda col: (lambda i: (jnp.maximum(i * hb - 1, 0), col))
        in_specs = [
            pl.BlockSpec((tm, K), lambda i: (i, 0)),
            pl.BlockSpec((tm, K), lambda i: (i, 1)),
            pl.BlockSpec((tm, K), lambda i: (i, 2)),
            pl.BlockSpec((BF16_ROWS, K), halo(1)),
            pl.BlockSpec((BF16_ROWS, K), halo(2)),
            pl.BlockSpec((SUBLANES, K), fixed),
        ]
        cw = jnp.zeros((SUBLANES, K), F32).at[:CONV_WIDTH].set(conv_w)
        args = [y, y, y, y, y, cw]
    else:
        in_specs = [pl.BlockSpec((tm, K), row)]
        args = [y]
    in_specs += [
        pl.BlockSpec((K, D), fixed),
        pl.BlockSpec((tm, D), row),
        pl.BlockSpec((1, D), fixed),
        pl.BlockSpec((D, 2 * LANES), fixed),
        pl.BlockSpec((1, LANES), fixed),
    ]
    args += [w_out, h_in, ffn_norm_w.reshape(1, D), r_w, r_b]
    wide = lambda dt: jax.ShapeDtypeStruct((T, LANES), dt)
    sub = _packed_sublanes(D)
    return pl.pallas_call(
        functools.partial(_out_router_kernel, conv=conv),
        grid=(T // tm,),
        in_specs=in_specs,
        out_specs=[pl.BlockSpec((tm, D), row), pl.BlockSpec((tm * sub, LANES), row)]
        + [pl.BlockSpec((tm, LANES), row)] * 4 + [pl.BlockSpec((1, LANES), fixed)],
        out_shape=[jax.ShapeDtypeStruct((T, D), F32), jax.ShapeDtypeStruct((T * sub, LANES), I32),
                   wide(I32), wide(I32), wide(F32), wide(F32), jax.ShapeDtypeStruct((1, LANES), F32)],
        compiler_params=_params("arbitrary"),
        name="out_router_conv" if conv else "out_router",
    )(*args)


def _rank_kernel(id0_ref, id1_ref, pst_ref, d0_ref, d1_ref, carry_ref):
    @pl.when(pl.program_id(0) == 0)
    def _():
        carry_ref[...] = jnp.zeros_like(carry_ref)

    tb = id0_ref.shape[0]
    lane = lax.broadcasted_iota(I32, (tb, LANES), 1)
    oh0 = lane == id0_ref[...]
    oh1 = lane == id1_ref[...]
    hit = jnp.where(oh0 | oh1, 1.0, 0.0)
    r = lax.broadcasted_iota(I32, (tb, tb), 0)
    c = lax.broadcasted_iota(I32, (tb, tb), 1)
    before = jnp.where(c < r, 1.0, 0.0).astype(BF16)
    base = _dot(before, hit.astype(BF16)) + carry_ref[...] + pst_ref[...]
    d0 = jnp.sum(jnp.where(oh0, base, 0.0), axis=-1, keepdims=True)
    d1 = jnp.sum(jnp.where(oh1, base, 0.0), axis=-1, keepdims=True)
    d0_ref[...] = jnp.broadcast_to(d0.astype(I32), (tb, LANES))
    d1_ref[...] = jnp.broadcast_to(d1.astype(I32), (tb, LANES))
    carry_ref[...] += jnp.sum(hit, axis=0, keepdims=True)


def rank(id0, id1, pstart_row):
    T = id0.shape[0]
    tb = _tile(T, 256)
    row = lambda i: (i, 0)
    return pl.pallas_call(
        _rank_kernel,
        grid=(T // tb,),
        in_specs=[pl.BlockSpec((tb, LANES), row), pl.BlockSpec((tb, LANES), row),
                  pl.BlockSpec((1, LANES), lambda i: (0, 0))],
        out_specs=[pl.BlockSpec((tb, LANES), row)] * 2,
        out_shape=[jax.ShapeDtypeStruct((T, LANES), I32)] * 2,
        scratch_shapes=[pltpu.VMEM((1, LANES), F32)],
        compiler_params=_params("arbitrary"),
        name="rank",
    )(id0, id1, pstart_row)


def _invert_kernel(d0_ref, d1_ref, fill_ref, rt_ref):
    width = SUBLANES

    def clear(p, carry):
        for k in range(width):
            rt_ref[p * width + k] = 0
        return carry

    for e in range(fill_ref.shape[0] // 2):
        lax.fori_loop(fill_ref[2 * e] // width, fill_ref[2 * e + 1] // width, clear, 0)

    def place(t, carry):
        rt_ref[d0_ref[t]] = t
        rt_ref[d1_ref[t]] = t
        return carry

    lax.fori_loop(0, d0_ref.shape[0], place, 0, unroll=16)


def invert(d0, d1, fill_ranges, n_rows):
    smem = pl.BlockSpec(memory_space=pltpu.SMEM)
    return pl.pallas_call(
        _invert_kernel,
        in_specs=[smem, smem, smem],
        out_specs=smem,
        out_shape=jax.ShapeDtypeStruct((n_rows,), I32),
        name="invert",
    )(d0, d1, fill_ranges)


def _row_copy(src_hbm, dst_ref, sem, src_row, dst_row, sub):
    src = src_hbm.at[pl.ds(pl.multiple_of(src_row * sub, sub), sub), :]
    dst = dst_ref.at[pl.ds(pl.multiple_of(dst_row * sub, sub), sub), :]
    return pltpu.make_async_copy(src, dst, sem)


def _experts_kernel(bexp_ref, first_ref, wslot_ref, next_ref, rtok_ref, nused_ref, hn_hbm, wgu_hbm, wd_hbm, y_ref,
                    xbuf0, xbuf1, xbuf2, wgu_f32, wd_f32, wgu_bf, wd_bf, sem, wsem, *, layer, ff, rb, sub):
    i = pl.program_id(0)
    n_used = nused_ref[0]
    bufs = (xbuf0, xbuf1, xbuf2)
    n_buf = len(bufs)
    assert n_buf == GATHER_AHEAD + 1

    def weight_copies(expert, slot):
        return (pltpu.make_async_copy(wgu_hbm.at[layer, expert], wgu_f32.at[slot], wsem.at[0, slot]),
                pltpu.make_async_copy(wd_hbm.at[layer, expert], wd_f32.at[slot], wsem.at[1, slot]))

    def wait_rows(blk, s):
        def body(r, carry):
            _row_copy(hn_hbm, bufs[s], sem.at[s], rtok_ref[blk * rb + r], r, sub).wait()
            return carry

        lax.fori_loop(0, rb, body, 0, unroll=8)

    def start_rows(blk, s):
        for r in range(rb):
            _row_copy(hn_hbm, bufs[s], sem.at[s], rtok_ref[blk * rb + r], r, sub).start(priority=ROW_QUEUE)

    @pl.when(i == 0)
    def _():
        for cp in weight_copies(bexp_ref[0], 0):
            cp.start(priority=WEIGHT_QUEUE)
        for blk in range(GATHER_AHEAD):
            start_rows(blk, blk)

    @pl.when(first_ref[i] == 1)
    def _():
        slot = wslot_ref[i]
        for cp in weight_copies(bexp_ref[i], slot):
            cp.wait()
        nxt = next_ref[i]

        @pl.when(nxt >= 0)
        def _():
            for cp in weight_copies(nxt, 1 - slot):
                cp.start(priority=WEIGHT_QUEUE)

        wgu_bf[...] = wgu_f32[slot].astype(BF16)
        wd_bf[...] = wd_f32[slot].astype(BF16)

    for s in range(n_buf):
        @pl.when((i < n_used) & (i % n_buf == s))
        def _():
            wait_rows(i, s)
            start_rows(i + GATHER_AHEAD, (s + GATHER_AHEAD) % n_buf)
            los, his = _load_packed(bufs[s], rb, sub)
            x = jnp.concatenate([p.astype(BF16) for p in los + his], axis=1)
            gu = _dot(x, wgu_bf[...])
            gt, up = gu[:, :ff], gu[:, ff:]
            act = (gt * jax.nn.sigmoid(gt) * up).astype(BF16)
            _store_packed(y_ref, _dot(act, wd_bf[...]))

        @pl.when((i >= n_used) & (i < n_used + GATHER_AHEAD) & (i % n_buf == s))
        def _():
            wait_rows(i, s)

    @pl.when(i >= n_used)
    def _():
        y_ref[...] = jnp.zeros_like(y_ref)


def experts(block_exp, row_tok, n_used, hn, w_gate_up, w_down, layer):
    D, ff2 = w_gate_up.shape[-2:]
    ff = ff2 // 2
    sub = _packed_sublanes(D)
    P = row_tok.shape[0]
    rb = ROW_BLOCK
    n_blocks = P // rb

    blk = jnp.arange(n_blocks, dtype=I32)
    prev = jnp.concatenate([jnp.full((1,), -1, I32), block_exp[:-1]])
    first = (blk < n_used[0]) & (block_exp != prev)
    wslot = (jnp.cumsum(first.astype(I32)) - 1) % 2
    first_at = lax.cummin(jnp.where(first, blk, n_blocks), reverse=True)
    next_first = jnp.concatenate([first_at[1:], jnp.full((1,), n_blocks, I32)])
    next_exp = jnp.where(next_first < n_blocks, block_exp[jnp.minimum(next_first, n_blocks - 1)], -1)

    grid_spec = pltpu.PrefetchScalarGridSpec(
        num_scalar_prefetch=6,
        grid=(n_blocks,),
        in_specs=[pl.BlockSpec(memory_space=pl.ANY)] * 3,
        out_specs=pl.BlockSpec((rb * sub, LANES), lambda i, *_: (i, 0)),
        scratch_shapes=[pltpu.VMEM((rb * sub, LANES), I32)] * (GATHER_AHEAD + 1)
        + [pltpu.VMEM((2, D, ff2), F32), pltpu.VMEM((2, ff, D), F32),
           pltpu.VMEM((D, ff2), BF16), pltpu.VMEM((ff, D), BF16),
           pltpu.SemaphoreType.DMA((GATHER_AHEAD + 1,)), pltpu.SemaphoreType.DMA((2, 2))],
    )
    return pl.pallas_call(
        functools.partial(_experts_kernel, layer=layer, ff=ff, rb=rb, sub=sub),
        grid_spec=grid_spec,
        out_shape=jax.ShapeDtypeStruct((P * sub, LANES), I32),
        compiler_params=_params("arbitrary"),
        name="experts",
    )(block_exp, first.astype(I32), wslot.astype(I32), next_exp.astype(I32), row_tok, n_used,
      hn, w_gate_up, w_down)


def _combine_kernel(d0_ref, d1_ref, yb_hbm, h_ref, g0_ref, g1_ref, nw_ref, o_ref, b00, b01, b10, b11, sem,
                    *, final_norm, sub):
    i = pl.program_id(0)
    n = pl.num_programs(0)
    tc = o_ref.shape[0]
    bufs = ((b00, b01), (b10, b11))
    d_refs = (d0_ref, d1_ref)

    def wait_rows(blk, s):
        def body(r, carry):
            for k in range(TOP_K):
                _row_copy(yb_hbm, bufs[s][k], sem.at[s], d_refs[k][blk * tc + r], r, sub).wait()
            return carry

        lax.fori_loop(0, tc, body, 0, unroll=8)

    def start_rows(blk, s):
        for r in range(tc):
            for k in range(TOP_K):
                _row_copy(yb_hbm, bufs[s][k], sem.at[s], d_refs[k][blk * tc + r], r, sub).start(priority=k)

    @pl.when(i == 0)
    def _():
        start_rows(0, 0)

    for s in range(2):
        @pl.when(i % 2 == s)
        def _():
            wait_rows(i, s)
            start_rows(i + 1, 1 - s)
            lo0, hi0 = _load_packed(bufs[s][0], tc, sub)
            lo1, hi1 = _load_packed(bufs[s][1], tc, sub)
            g0, g1 = g0_ref[...], g1_ref[...]
            pieces = [h_ref[:, j * LANES:(j + 1) * LANES] + g0 * y0 + g1 * y1
                      for j, (y0, y1) in enumerate(zip(lo0 + hi0, lo1 + hi1))]
            if final_norm:
                o_ref[...] = _rms(jnp.concatenate(pieces, axis=1), nw_ref[...])
            else:
                for j, piece in enumerate(pieces):
                    o_ref[:, j * LANES:(j + 1) * LANES] = piece

            @pl.when(i == n - 1)
            def _():
                wait_rows(i + 1, 1 - s)


def combine(d0, d1, yb, h, g0, g1, norm_w=None):
    T, D = h.shape
    tc = _tile(T, 256)
    sub = _packed_sublanes(D)
    final_norm = norm_w is not None
    nw = (norm_w if final_norm else jnp.ones((D,), F32)).reshape(1, D)
    spare = jnp.zeros((tc,), I32)
    row = lambda i, a, b: (i, 0)
    grid_spec = pltpu.PrefetchScalarGridSpec(
        num_scalar_prefetch=2,
        grid=(T // tc,),
        in_specs=[
            pl.BlockSpec(memory_space=pl.ANY),
            pl.BlockSpec((tc, D), row),
            pl.BlockSpec((tc, LANES), row),
            pl.BlockSpec((tc, LANES), row),
            pl.BlockSpec((1, D), lambda i, a, b: (0, 0)),
        ],
        out_specs=pl.BlockSpec((tc, D), row),
        scratch_shapes=[pltpu.VMEM((tc * sub, LANES), I32)] * (2 * TOP_K) + [pltpu.SemaphoreType.DMA((2,))],
    )
    return pl.pallas_call(
        functools.partial(_combine_kernel, final_norm=final_norm, sub=sub),
        grid_spec=grid_spec,
        out_shape=jax.ShapeDtypeStruct((T, D), F32),
        compiler_params=_params("arbitrary"),
        name="combine_norm" if final_norm else "combine",
    )(jnp.concatenate([d0, spare]), jnp.concatenate([d1, spare]), yb, h, g0, g1, nw)


def _pad_cols(w, n):
    return jnp.pad(w, ((0, 0), (0, n - w.shape[1])))


def moe(h, hn, id0, id1, g0, g1, counts, w_gate_up, w_down, layer, final_norm_w=None):
    T, D = h.shape
    rb = ROW_BLOCK
    n_blocks = -(-(T * TOP_K + N_EXPERTS * (rb - 1)) // rb) + GATHER_AHEAD
    cnt = counts[0, :N_EXPERTS].astype(I32)
    padded = ((cnt + rb - 1) // rb) * rb
    pends = jnp.cumsum(padded)
    pstart_row = _pad_cols((pends - padded).astype(F32).reshape(1, N_EXPERTS), LANES)
    d0b, d1b = rank(id0, id1, pstart_row)
    d0, d1 = d0b[:, 0], d1b[:, 0]
    fill_start = jnp.concatenate([pends - padded + cnt, pends[-1:]])
    fill_end = jnp.concatenate([pends, jnp.full((1,), n_blocks * rb, I32)])
    row_tok = invert(d0, d1, jnp.stack([fill_start, fill_end], axis=1).reshape(-1), n_blocks * rb)
    block_start = jnp.arange(n_blocks, dtype=I32) * rb
    block_exp = jnp.minimum(jnp.sum(pends[None, :] <= block_start[:, None], axis=1), N_EXPERTS - 1).astype(I32)
    n_used = (pends[-1:] // rb).astype(I32)
    yb = experts(block_exp, row_tok, n_used, hn, w_gate_up, w_down, layer)
    return combine(d0, d1, yb, h, g0, g1, final_norm_w)


def _router_weights(w_group, b_group, w_router, b_router):
    w = _pad_cols(jnp.concatenate([w_router, w_group], axis=1), LANES)
    b = _pad_cols(jnp.concatenate([b_router, b_group]).reshape(1, -1), LANES)
    hi = w.astype(BF16)
    lo = (w - hi.astype(F32)).astype(BF16)
    return jnp.concatenate([hi, lo], axis=1), b


def kernel(x, mix_norm, gla_w_in, gla_w_gk2, gla_b_gk, gla_head_norm, gla_w_out, conv_w_in, conv_w,
           conv_w_out, ffn_norm, w_group, b_group, w_router, b_router, w_gate_up, w_down, final_norm):
    B, T, D = x.shape
    assert B == 1, "the recurrence state is carried across the whole row axis"
    h = x.reshape(T, D)
    kd, vd = D // 2, D
    dk, dv = kd // GLA_HEADS, vd // GLA_HEADS

    w_in = gla_w_in[0]
    n_main = 2 * kd + 2 * vd
    proj, gl = norm_proj(h, mix_norm[0], w_in[:, :n_main].astype(BF16),
                         _pad_cols(w_in[:, n_main:], LANES).astype(BF16))
    w_gk2 = jnp.pad(gla_w_gk2[0], ((0, LANES - GATE_RANK), (0, 0))).astype(BF16)
    o = gla(proj, gl, w_gk2, gla_b_gk[0].reshape(1, kd), gla_head_norm[0].reshape(1, dv), dk=dk, dv=dv)
    routed = out_router(o, gla_w_out[0].astype(BF16), h, ffn_norm[0],
                        *_router_weights(w_group[0], b_group[0], w_router[0], b_router[0]))
    h = moe(*routed, w_gate_up, w_down, 0)

    proj = norm_proj(h, mix_norm[1], conv_w_in[0].astype(BF16))
    routed = out_router(proj, conv_w_out[0].astype(BF16), h, ffn_norm[1],
                        *_router_weights(w_group[1], b_group[1], w_router[1], b_router[1]),
                        conv_w=conv_w[0])
    out = moe(*routed, w_gate_up, w_down, 1, final_norm_w=final_norm)
    return out.reshape(B, T, D)
```

```python
import functools

import jax
import jax.numpy as jnp
from jax import lax
from jax.experimental import pallas as pl
from jax.experimental.pallas import tpu as pltpu

F32 = jnp.float32
BF16 = jnp.bfloat16
I32 = jnp.int32

EPS = 1e-6
GLA_HEADS = 4
GATE_RANK = 16
GATE_NORMALIZER = 16.0
CONV_WIDTH = 3
N_GROUPS = 4
EXPERTS_PER_GROUP = 8
N_EXPERTS = N_GROUPS * EXPERTS_PER_GROUP
TOP_K = 2

LANES = 128
SUBLANES = 8
BF16_ROWS = 16
HIGH_HALF = -65536
VMEM_LIMIT = 56 * 1024 * 1024

GLA_CHUNK = 64
GLA_SUB = 16
GLA_GROUP = 128
GLA_SAFE_DECAY = 60.0
ROW_BLOCK = 256
GATHER_AHEAD = 2
ROW_QUEUE, WEIGHT_QUEUE = 0, 1


def _tile(n, pref):
    t = min(n, pref)
    assert n % t == 0, (n, t)
    return t


def _params(*sem):
    return pltpu.CompilerParams(dimension_semantics=sem, vmem_limit_bytes=VMEM_LIMIT)


def _dot(a, b):
    return jnp.dot(a, b, preferred_element_type=F32)


def _dot_nt(a, b):
    return lax.dot_general(a, b, (((1,), (1,)), ((), ())), preferred_element_type=F32)


def _dot_tn(a, b):
    return lax.dot_general(a, b, (((0,), (0,)), ((), ())), preferred_element_type=F32)


def _rms(x, w):
    return x * lax.rsqrt(jnp.mean(x * x, axis=-1, keepdims=True) + EPS) * w


def _packed_sublanes(d):
    assert d % (2 * LANES) == 0
    return d // (2 * LANES)


def _store_packed(ref, x):
    rows, d = x.shape
    half = d // 2
    sub = half // LANES
    lo = lax.shift_right_logical(lax.bitcast_convert_type(x[:, :half].astype(BF16).astype(F32), I32), 16)
    hi = lax.bitcast_convert_type(x[:, half:].astype(BF16).astype(F32), I32) & HIGH_HALF
    words = hi | lo
    for s in range(sub):
        ref[pl.ds(s, rows, stride=sub), :] = words[:, s * LANES:(s + 1) * LANES]


def _load_packed(ref, rows, sub):
    los, his = [], []
    for s in range(sub):
        w = ref[pl.ds(s, rows, stride=sub), :]
        los.append(lax.bitcast_convert_type(lax.shift_left(w, 16), F32))
        his.append(lax.bitcast_convert_type(w & HIGH_HALF, F32))
    return los, his


def _norm_proj_kernel(*refs, with_extra):
    if with_extra:
        x_ref, nw_ref, w_ref, we_ref, o_ref, e_ref, xn_ref = refs
    else:
        x_ref, nw_ref, w_ref, o_ref, xn_ref = refs

    @pl.when(pl.program_id(1) == 0)
    def _():
        xn_ref[...] = _rms(x_ref[...], nw_ref[...]).astype(BF16)
        if with_extra:
            e_ref[...] = _dot(xn_ref[...], we_ref[...])

    o_ref[...] = _dot(xn_ref[...], w_ref[...]).astype(o_ref.dtype)


def norm_proj(x, norm_w, w, w_extra=None):
    T, D = x.shape
    N = w.shape[1]
    tm, tn = _tile(T, 1024), _tile(N, 1024)
    in_specs = [
        pl.BlockSpec((tm, D), lambda i, j: (i, 0)),
        pl.BlockSpec((1, D), lambda i, j: (0, 0)),
        pl.BlockSpec((D, tn), lambda i, j: (0, j)),
    ]
    out_shape = [jax.ShapeDtypeStruct((T, N), BF16)]
    out_specs = [pl.BlockSpec((tm, tn), lambda i, j: (i, j))]
    args = [x, norm_w.reshape(1, D), w]
    if w_extra is not None:
        in_specs.append(pl.BlockSpec((D, LANES), lambda i, j: (0, 0)))
        out_shape.append(jax.ShapeDtypeStruct((T, LANES), F32))
        out_specs.append(pl.BlockSpec((tm, LANES), lambda i, j: (i, 0)))
        args.append(w_extra)
    out = pl.pallas_call(
        functools.partial(_norm_proj_kernel, with_extra=w_extra is not None),
        grid=(T // tm, N // tn),
        in_specs=in_specs,
        out_specs=out_specs,
        out_shape=out_shape,
        scratch_shapes=[pltpu.VMEM((tm, D), BF16)],
        compiler_params=_params("arbitrary", "arbitrary"),
        name="norm_proj",
    )(*args)
    return out if w_extra is not None else out[0]


def _gla_kernel(q_ref, k_ref, v_ref, g_ref, gl_ref, wgk_ref, bgk_ref, hnorm_ref, o_ref,
                st_ref, bs_ref, qt_ref, kh_ref, bref_ref, *, dk, dv, tb):
    C, SUB, GRP, H = GLA_CHUNK, GLA_SUB, GLA_GROUP, GLA_HEADS
    NS = C // SUB
    CPG = GRP // C
    kd = H * dk
    scale = dk ** -0.5

    @pl.when(pl.program_id(0) == 0)
    def _():
        st_ref[...] = jnp.zeros_like(st_ref)

    r = lax.broadcasted_iota(I32, (GRP, GRP), 0)
    c = lax.broadcasted_iota(I32, (GRP, GRP), 1)
    same = (r // SUB) == (c // SUB)
    low = jnp.where(same & (c <= r), 1.0, 0.0).astype(BF16)
    upp = jnp.where(same & (c > r), 1.0, 0.0).astype(BF16)
    rr = lax.broadcasted_iota(I32, (CPG * SUBLANES, GRP), 0)
    cc = lax.broadcasted_iota(I32, (CPG * SUBLANES, GRP), 1)
    blk = rr % SUBLANES
    mref = jnp.where(((cc // C) == (rr // SUBLANES)) & ((cc % C) < SUB * blk) & (blk <= NS),
                     1.0, 0.0).astype(BF16)

    for gi in range(tb // GRP):
        sl = pl.ds(gi * GRP, GRP)
        z = _dot(gl_ref[sl, :].astype(BF16), wgk_ref[...]) + bgk_ref[...]
        la = (jnp.minimum(z, 0.0) - jnp.log(1.0 + jnp.exp(-jnp.abs(z)))) * (1.0 / GATE_NORMALIZER)
        hi = la.astype(BF16)
        lo = (la - hi.astype(F32)).astype(BF16)
        bs = _dot(low, hi) + _dot(low, lo)
        ru = _dot(upp, hi) + _dot(upp, lo)
        br = _dot(mref, hi) + _dot(mref, lo)
        bs_ref[sl, :] = bs
        qt_ref[sl, :] = q_ref[sl, :].astype(F32) * jnp.exp(bs) * scale
        kh_ref[sl, :] = k_ref[sl, :].astype(F32) * jnp.exp(ru)
        bref_ref[pl.ds(gi * CPG, CPG)] = br.reshape(CPG, SUBLANES, kd)

    lane = lax.broadcasted_iota(I32, (SUB, C), 1)
    subrow = lax.broadcasted_iota(I32, (SUB, dk), 0)

    def scores_pivoted(qin, kraw, bs, bn):
        neg_b = jnp.concatenate([-(bs[J * SUB:(J + 1) * SUB] + bn[J:J + 1]) for J in range(NS)], axis=0)
        s = _dot_nt(qin.astype(BF16), (kraw * jnp.exp(neg_b)).astype(BF16))
        ri = lax.broadcasted_iota(I32, (C, C), 0)
        ci = lax.broadcasted_iota(I32, (C, C), 1)
        return jnp.where(ci <= ri, s, 0.0)

    def scores_stable(qt, kh, qraw, kraw, bs, bn):
        a_rows = []
        for I in range(NS):
            blk_rows = slice(I * SUB, (I + 1) * SUB)
            q_blk, k_blk, bs_blk = qraw[blk_rows], kraw[blk_rows], bs[blk_rows]
            acc = jnp.zeros((SUB, C), F32)
            for j in range(SUB):
                e = jnp.where(subrow >= j, jnp.exp(bs_blk - bs_blk[j:j + 1]), 0.0)
                p = (q_blk * k_blk[j:j + 1]) * e
                acc = jnp.where(lane == I * SUB + j, jnp.sum(p, axis=-1, keepdims=True), acc)
            if I > 0:
                parts = []
                for J in range(NS):
                    if J < I:
                        parts.append(kh[J * SUB:(J + 1) * SUB] * jnp.exp(bn[I:I + 1] - bn[J + 1:J + 2]))
                    else:
                        parts.append(jnp.zeros((SUB, dk), F32))
                k_dec = jnp.concatenate(parts, axis=0)
                acc = acc + _dot_nt(qt[blk_rows].astype(BF16), k_dec.astype(BF16))
            a_rows.append(acc)
        return jnp.concatenate(a_rows, axis=0)

    def chunk(n, carry, *, pivoted):
        r0 = pl.multiple_of(n * C, C)
        rows = pl.ds(r0, C)
        outs = []
        for h in range(H):
            kl = slice(h * dk, (h + 1) * dk)
            vl = slice(h * dv, (h + 1) * dv)
            bn = bref_ref[n, :, kl]
            eb = jnp.exp(bn)
            st = st_ref[h]
            qt = qt_ref[rows, kl]
            kh = kh_ref[rows, kl]
            bs = bs_ref[rows, kl]
            v = v_ref[rows, vl]
            kraw = k_ref[rows, kl].astype(F32)

            qin = jnp.concatenate([qt[I * SUB:(I + 1) * SUB] * eb[I:I + 1] for I in range(NS)], axis=0)
            o = _dot_nt(qin.astype(BF16), st.astype(BF16))
            if pivoted:
                a = scores_pivoted(qin, kraw, bs, bn)
            else:
                a = scores_stable(qt, kh, q_ref[rows, kl].astype(F32) * scale, kraw, bs, bn)
            o = o + _dot(a.astype(BF16), v)

            k_st = jnp.concatenate(
                [kh[J * SUB:(J + 1) * SUB] * jnp.exp(bn[NS:NS + 1] - bn[J + 1:J + 2]) for J in range(NS)],
                axis=0)
            st_ref[h] = st * eb[NS:NS + 1] + _dot_tn(v, k_st.astype(BF16))
            outs.append(_rms(o, hnorm_ref[...]))

        g = g_ref[rows, :].astype(F32)
        o_ref[rows, :] = (jnp.concatenate(outs, axis=1) * (g * jax.nn.sigmoid(g))).astype(o_ref.dtype)
        return carry

    span = jnp.max(-bref_ref[...])

    @pl.when(span < GLA_SAFE_DECAY)
    def _():
        lax.fori_loop(0, tb // C, functools.partial(chunk, pivoted=True), 0, unroll=2)

    @pl.when(jnp.logical_not(span < GLA_SAFE_DECAY))
    def _():
        lax.fori_loop(0, tb // C, functools.partial(chunk, pivoted=False), 0)


def gla(proj, gl, w_gk2, b_gk, head_norm, *, dk, dv):
    T = proj.shape[0]
    H = GLA_HEADS
    tb = _tile(T, 512)
    assert tb % GLA_GROUP == 0 and dk % LANES == 0 and dv % LANES == 0
    kd, vd = H * dk, H * dv
    assert (2 * kd) % vd == 0
    v_off = (2 * kd) // vd
    fixed = lambda t: (0, 0)
    return pl.pallas_call(
        functools.partial(_gla_kernel, dk=dk, dv=dv, tb=tb),
        grid=(T // tb,),
        in_specs=[
            pl.BlockSpec((tb, kd), lambda t: (t, 0)),
            pl.BlockSpec((tb, kd), lambda t: (t, 1)),
            pl.BlockSpec((tb, vd), lambda t: (t, v_off)),
            pl.BlockSpec((tb, vd), lambda t: (t, v_off + 1)),
            pl.BlockSpec((tb, LANES), lambda t: (t, 0)),
            pl.BlockSpec((LANES, kd), fixed),
            pl.BlockSpec((1, kd), fixed),
            pl.BlockSpec((1, dv), fixed),
        ],
        out_specs=pl.BlockSpec((tb, vd), lambda t: (t, 0)),
        out_shape=jax.ShapeDtypeStruct((T, vd), BF16),
        scratch_shapes=[
            pltpu.VMEM((H, dv, dk), F32),
            pltpu.VMEM((tb, kd), F32),
            pltpu.VMEM((tb, kd), F32),
            pltpu.VMEM((tb, kd), F32),
            pltpu.VMEM((tb // GLA_CHUNK, SUBLANES, kd), F32),
        ],
        compiler_params=_params("arbitrary"),
        name="gla",
    )(proj, proj, proj, proj, gl, w_gk2, b_gk, head_norm)


def _route(hn, rw_ref, rb_ref):
    tm = hn.shape[0]
    hi = hn.astype(BF16)
    lo = (hn - hi.astype(F32)).astype(BF16)
    both = _dot(hi, rw_ref[...])
    logits = both[:, :LANES] + both[:, LANES:] + _dot(lo, rw_ref[:, :LANES]) + rb_ref[...]
    lane = lax.broadcasted_iota(I32, (tm, LANES), 1)
    lane_f = lane.astype(F32)
    neg = -jnp.inf
    far = float(LANES)

    def first_max(vals):
        m = jnp.max(vals, axis=-1, keepdims=True)
        idx = jnp.min(jnp.where(vals == m, lane_f, far), axis=-1, keepdims=True)
        return m, idx

    lg = jnp.where((lane >= N_EXPERTS) & (lane < N_EXPERTS + N_GROUPS), logits, neg)
    mg, gidx = first_max(lg)
    top_gp = 1.0 / jnp.sum(jnp.exp(lg - mg), axis=-1, keepdims=True)
    grp = gidx.astype(I32) - N_EXPERTS
    le = jnp.where((lane < N_EXPERTS) & ((lane // EXPERTS_PER_GROUP) == grp), logits, neg)
    m1, i1 = first_max(le)
    m2, i2 = first_max(jnp.where(lane_f == i1, neg, le))
    e2 = jnp.exp(m2 - m1)
    w1 = 1.0 / (1.0 + e2)
    return i1.astype(I32), i2.astype(I32), top_gp * w1, top_gp * (e2 * w1)


def _out_router_kernel(*refs, conv):
    if conv:
        (pb_ref, pc_ref, ph_ref, hc_ref, hh_ref, cw_ref, w_ref, hin_ref, fnw_ref, rw_ref,
         rb_ref, h_ref, hn_ref, id0_ref, id1_ref, g0_ref, g1_ref, cnt_ref) = refs
    else:
        (y_ref, w_ref, hin_ref, fnw_ref, rw_ref,
         rb_ref, h_ref, hn_ref, id0_ref, id1_ref, g0_ref, g1_ref, cnt_ref) = refs
    i = pl.program_id(0)

    if conv:
        u = pc_ref[...].astype(F32) * ph_ref[...].astype(F32)
        prev = hc_ref[...].astype(F32) * hh_ref[...].astype(F32)
        prev = jnp.where(i > 0, prev, 0.0)
        row = lax.broadcasted_iota(I32, u.shape, 0)
        last, last2 = prev[BF16_ROWS - 1:BF16_ROWS], prev[BF16_ROWS - 2:BF16_ROWS - 1]
        u1 = jnp.where(row == 0, last, pltpu.roll(u, 1, 0))
        u2 = jnp.where(row == 0, last2, jnp.where(row == 1, last, pltpu.roll(u, 2, 0)))
        cw = cw_ref[...]
        y = (pb_ref[...].astype(F32) * (cw[0:1] * u2 + cw[1:2] * u1 + cw[2:3] * u)).astype(BF16)
    else:
        y = y_ref[...]

    h = hin_ref[...] + _dot(y, w_ref[...])
    h_ref[...] = h
    hn = _rms(h, fnw_ref[...])
    _store_packed(hn_ref, hn)
    i1, i2, g1, g2 = _route(hn, rw_ref, rb_ref)
    shape = id0_ref.shape
    id0_ref[...] = jnp.broadcast_to(i1, shape)
    id1_ref[...] = jnp.broadcast_to(i2, shape)
    g0_ref[...] = jnp.broadcast_to(g1, shape)
    g1_ref[...] = jnp.broadcast_to(g2, shape)

    @pl.when(i == 0)
    def _():
        cnt_ref[...] = jnp.zeros_like(cnt_ref)

    lane = lax.broadcasted_iota(I32, shape, 1)
    hit = jnp.where((lane == i1) | (lane == i2), 1.0, 0.0)
    cnt_ref[...] += jnp.sum(hit, axis=0, keepdims=True)


def out_router(y, w_out, h_in, ffn_norm_w, r_w, r_b, conv_w=None):
    T, D = h_in.shape
    K = w_out.shape[0]
    tm = _tile(T, 256)
    conv = conv_w is not None
    row = lambda i: (i, 0)
    fixed = lambda i: (0, 0)
    if conv:
        hb = tm // BF16_ROWS
        halo = lambda col: (lambda i: (jnp.maximum(i * hb - 1, 0), col))
        in_specs = [
            pl.BlockSpec((tm, K), lambda i: (i, 0)),
            pl.BlockSpec((tm, K), lambda i: (i, 1)),
            pl.BlockSpec((tm, K), lambda i: (i, 2)),
            pl.BlockSpec((BF16_ROWS, K), halo(1)),
            pl.BlockSpec((BF16_ROWS, K), halo(2)),
            pl.BlockSpec((SUBLANES, K), fixed),
        ]
        cw = jnp.zeros((SUBLANES, K), F32).at[:CONV_WIDTH].set(conv_w)
        args = [y, y, y, y, y, cw]
    else:
        in_specs = [pl.BlockSpec((tm, K), row)]
        args = [y]
    in_specs += [
        pl.BlockSpec((K, D), fixed),
        pl.BlockSpec((tm, D), row),
        pl.BlockSpec((1, D), fixed),
        pl.BlockSpec((D, 2 * LANES), fixed),
        pl.BlockSpec((1, LANES), fixed),
    ]
    args += [w_out, h_in, ffn_norm_w.reshape(1, D), r_w, r_b]
    wide = lambda dt: jax.ShapeDtypeStruct((T, LANES), dt)
    sub = _packed_sublanes(D)
    return pl.pallas_call(
        functools.partial(_out_router_kernel, conv=conv),
        grid=(T // tm,),
        in_specs=in_specs,
        out_specs=[pl.BlockSpec((tm, D), row), pl.BlockSpec((tm * sub, LANES), row)]
        + [pl.BlockSpec((tm, LANES), row)] * 4 + [pl.BlockSpec((1, LANES), fixed)],
        out_shape=[jax.ShapeDtypeStruct((T, D), F32), jax.ShapeDtypeStruct((T * sub, LANES), I32),
                   wide(I32), wide(I32), wide(F32), wide(F32), jax.ShapeDtypeStruct((1, LANES), F32)],
        compiler_params=_params("arbitrary"),
        name="out_router_conv" if conv else "out_router",
    )(*args)


def _rank_kernel(id0_ref, id1_ref, pst_ref, d0_ref, d1_ref, carry_ref):
    @pl.when(pl.program_id(0) == 0)
    def _():
        carry_ref[...] = jnp.zeros_like(carry_ref)

    tb = id0_ref.shape[0]
    lane = lax.broadcasted_iota(I32, (tb, LANES), 1)
    oh0 = lane == id0_ref[...]
    oh1 = lane == id1_ref[...]
    hit = jnp.where(oh0 | oh1, 1.0, 0.0)
    r = lax.broadcasted_iota(I32, (tb, tb), 0)
    c = lax.broadcasted_iota(I32, (tb, tb), 1)
    before = jnp.where(c < r, 1.0, 0.0).astype(BF16)
    base = _dot(before, hit.astype(BF16)) + carry_ref[...] + pst_ref[...]
    d0 = jnp.sum(jnp.where(oh0, base, 0.0), axis=-1, keepdims=True)
    d1 = jnp.sum(jnp.where(oh1, base, 0.0), axis=-1, keepdims=True)
    d0_ref[...] = jnp.broadcast_to(d0.astype(I32), (tb, LANES))
    d1_ref[...] = jnp.broadcast_to(d1.astype(I32), (tb, LANES))
    carry_ref[...] += jnp.sum(hit, axis=0, keepdims=True)


def rank(id0, id1, pstart_row):
    T = id0.shape[0]
    tb = _tile(T, 256)
    row = lambda i: (i, 0)
    return pl.pallas_call(
        _rank_kernel,
        grid=(T // tb,),
        in_specs=[pl.BlockSpec((tb, LANES), row), pl.BlockSpec((tb, LANES), row),
                  pl.BlockSpec((1, LANES), lambda i: (0, 0))],
        out_specs=[pl.BlockSpec((tb, LANES), row)] * 2,
        out_shape=[jax.ShapeDtypeStruct((T, LANES), I32)] * 2,
        scratch_shapes=[pltpu.VMEM((1, LANES), F32)],
        compiler_params=_params("arbitrary"),
        name="rank",
    )(id0, id1, pstart_row)


def _invert_kernel(d0_ref, d1_ref, fill_ref, rt_ref):
    width = SUBLANES

    def clear(p, carry):
        for k in range(width):
            rt_ref[p * width + k] = 0
        return carry

    for e in range(fill_ref.shape[0] // 2):
        lax.fori_loop(fill_ref[2 * e] // width, fill_ref[2 * e + 1] // width, clear, 0)

    def place(t, carry):
        rt_ref[d0_ref[t]] = t
        rt_ref[d1_ref[t]] = t
        return carry

    lax.fori_loop(0, d0_ref.shape[0], place, 0, unroll=16)


def invert(d0, d1, fill_ranges, n_rows):
    smem = pl.BlockSpec(memory_space=pltpu.SMEM)
    return pl.pallas_call(
        _invert_kernel,
        in_specs=[smem, smem, smem],
        out_specs=smem,
        out_shape=jax.ShapeDtypeStruct((n_rows,), I32),
        name="invert",
    )(d0, d1, fill_ranges)


def _row_copy(src_hbm, dst_ref, sem, src_row, dst_row, sub):
    src = src_hbm.at[pl.ds(pl.multiple_of(src_row * sub, sub), sub), :]
    dst = dst_ref.at[pl.ds(pl.multiple_of(dst_row * sub, sub), sub), :]
    return pltpu.make_async_copy(src, dst, sem)


def _experts_kernel(bexp_ref, first_ref, wslot_ref, next_ref, rtok_ref, nused_ref, hn_hbm, wgu_hbm, wd_hbm, y_ref,
                    xbuf0, xbuf1, xbuf2, wgu_f32, wd_f32, wgu_bf, wd_bf, sem, wsem, *, layer, ff, rb, sub):
    i = pl.program_id(0)
    n_used = nused_ref[0]
    bufs = (xbuf0, xbuf1, xbuf2)
    n_buf = len(bufs)
    assert n_buf == GATHER_AHEAD + 1

    def weight_copies(expert, slot):
        return (pltpu.make_async_copy(wgu_hbm.at[layer, expert], wgu_f32.at[slot], wsem.at[0, slot]),
                pltpu.make_async_copy(wd_hbm.at[layer, expert], wd_f32.at[slot], wsem.at[1, slot]))

    def wait_rows(s):
        pltpu.make_async_copy(hn_hbm.at[pl.ds(0, rb * sub), :], bufs[s], sem.at[s]).wait()

    def start_rows(blk, s):
        for r in range(rb):
            _row_copy(hn_hbm, bufs[s], sem.at[s], rtok_ref[blk * rb + r], r, sub).start(priority=ROW_QUEUE)

    @pl.when(i == 0)
    def _():
        for cp in weight_copies(bexp_ref[0], 0):
            cp.start(priority=WEIGHT_QUEUE)
        for blk in range(GATHER_AHEAD):
            start_rows(blk, blk)

    @pl.when(first_ref[i] == 1)
    def _():
        slot = wslot_ref[i]
        for cp in weight_copies(bexp_ref[i], slot):
            cp.wait()
        nxt = next_ref[i]

        @pl.when(nxt >= 0)
        def _():
            for cp in weight_copies(nxt, 1 - slot):
                cp.start(priority=WEIGHT_QUEUE)

        wgu_bf[...] = wgu_f32[slot].astype(BF16)
        wd_bf[...] = wd_f32[slot].astype(BF16)

    for s in range(n_buf):
        @pl.when((i < n_used) & (i % n_buf == s))
        def _():
            wait_rows(s)
            start_rows(i + GATHER_AHEAD, (s + GATHER_AHEAD) % n_buf)
            los, his = _load_packed(bufs[s], rb, sub)
            x = jnp.concatenate([p.astype(BF16) for p in los + his], axis=1)
            gu = _dot(x, wgu_bf[...])
            gt, up = gu[:, :ff], gu[:, ff:]
            act = (gt * jax.nn.sigmoid(gt) * up).astype(BF16)
            _store_packed(y_ref, _dot(act, wd_bf[...]))

        @pl.when((i >= n_used) & (i < n_used + GATHER_AHEAD) & (i % n_buf == s))
        def _():
            wait_rows(s)

    @pl.when(i >= n_used)
    def _():
        y_ref[...] = jnp.zeros_like(y_ref)


def experts(block_exp, row_tok, n_used, hn, w_gate_up, w_down, layer):
    D, ff2 = w_gate_up.shape[-2:]
    ff = ff2 // 2
    sub = _packed_sublanes(D)
    P = row_tok.shape[0]
    rb = ROW_BLOCK
    n_blocks = P // rb

    blk = jnp.arange(n_blocks, dtype=I32)
    prev = jnp.concatenate([jnp.full((1,), -1, I32), block_exp[:-1]])
    first = (blk < n_used[0]) & (block_exp != prev)
    wslot = (jnp.cumsum(first.astype(I32)) - 1) % 2
    first_at = lax.cummin(jnp.where(first, blk, n_blocks), reverse=True)
    next_first = jnp.concatenate([first_at[1:], jnp.full((1,), n_blocks, I32)])
    next_exp = jnp.where(next_first < n_blocks, block_exp[jnp.minimum(next_first, n_blocks - 1)], -1)

    grid_spec = pltpu.PrefetchScalarGridSpec(
        num_scalar_prefetch=6,
        grid=(n_blocks,),
        in_specs=[pl.BlockSpec(memory_space=pl.ANY)] * 3,
        out_specs=pl.BlockSpec((rb * sub, LANES), lambda i, *_: (i, 0)),
        scratch_shapes=[pltpu.VMEM((rb * sub, LANES), I32)] * (GATHER_AHEAD + 1)
        + [pltpu.VMEM((2, D, ff2), F32), pltpu.VMEM((2, ff, D), F32),
           pltpu.VMEM((D, ff2), BF16), pltpu.VMEM((ff, D), BF16),
           pltpu.SemaphoreType.DMA((GATHER_AHEAD + 1,)), pltpu.SemaphoreType.DMA((2, 2))],
    )
    return pl.pallas_call(
        functools.partial(_experts_kernel, layer=layer, ff=ff, rb=rb, sub=sub),
        grid_spec=grid_spec,
        out_shape=jax.ShapeDtypeStruct((P * sub, LANES), I32),
        compiler_params=_params("arbitrary"),
        name="experts",
    )(block_exp, first.astype(I32), wslot.astype(I32), next_exp.astype(I32), row_tok, n_used,
      hn, w_gate_up, w_down)


def _combine_kernel(d0_ref, d1_ref, yb_hbm, h_ref, g0_ref, g1_ref, nw_ref, o_ref, b00, b01, b10, b11, sem,
                    *, final_norm, sub):
    i = pl.program_id(0)
    n = pl.num_programs(0)
    tc = o_ref.shape[0]
    bufs = ((b00, b01), (b10, b11))
    d_refs = (d0_ref, d1_ref)

    def wait_rows(s):
        for k in range(TOP_K):
            pltpu.make_async_copy(yb_hbm.at[pl.ds(0, tc * sub), :], bufs[s][k], sem.at[s]).wait()

    def start_rows(blk, s):
        for r in range(tc):
            for k in range(TOP_K):
                _row_copy(yb_hbm, bufs[s][k], sem.at[s], d_refs[k][blk * tc + r], r, sub).start(priority=k)

    @pl.when(i == 0)
    def _():
        start_rows(0, 0)

    for s in range(2):
        @pl.when(i % 2 == s)
        def _():
            wait_rows(s)
            start_rows(i + 1, 1 - s)
            lo0, hi0 = _load_packed(bufs[s][0], tc, sub)
            lo1, hi1 = _load_packed(bufs[s][1], tc, sub)
            g0, g1 = g0_ref[...], g1_ref[...]
            pieces = [h_ref[:, j * LANES:(j + 1) * LANES] + g0 * y0 + g1 * y1
                      for j, (y0, y1) in enumerate(zip(lo0 + hi0, lo1 + hi1))]
            if final_norm:
                o_ref[...] = _rms(jnp.concatenate(pieces, axis=1), nw_ref[...])
            else:
                for j, piece in enumerate(pieces):
                    o_ref[:, j * LANES:(j + 1) * LANES] = piece

            @pl.when(i == n - 1)
            def _():
                wait_rows(1 - s)


def combine(d0, d1, yb, h, g0, g1, norm_w=None):
    T, D = h.shape
    tc = _tile(T, 256)
    sub = _packed_sublanes(D)
    final_norm = norm_w is not None
    nw = (norm_w if final_norm else jnp.ones((D,), F32)).reshape(1, D)
    spare = jnp.zeros((tc,), I32)
    row = lambda i, a, b: (i, 0)
    grid_spec = pltpu.PrefetchScalarGridSpec(
        num_scalar_prefetch=2,
        grid=(T // tc,),
        in_specs=[
            pl.BlockSpec(memory_space=pl.ANY),
            pl.BlockSpec((tc, D), row),
            pl.BlockSpec((tc, LANES), row),
            pl.BlockSpec((tc, LANES), row),
            pl.BlockSpec((1, D), lambda i, a, b: (0, 0)),
        ],
        out_specs=pl.BlockSpec((tc, D), row),
        scratch_shapes=[pltpu.VMEM((tc * sub, LANES), I32)] * (2 * TOP_K) + [pltpu.SemaphoreType.DMA((2,))],
    )
    return pl.pallas_call(
        functools.partial(_combine_kernel, final_norm=final_norm, sub=sub),
        grid_spec=grid_spec,
        out_shape=jax.ShapeDtypeStruct((T, D), F32),
        compiler_params=_params("arbitrary"),
        name="combine_norm" if final_norm else "combine",
    )(jnp.concatenate([d0, spare]), jnp.concatenate([d1, spare]), yb, h, g0, g1, nw)


def _pad_cols(w, n):
    return jnp.pad(w, ((0, 0), (0, n - w.shape[1])))


def moe(h, hn, id0, id1, g0, g1, counts, w_gate_up, w_down, layer, final_norm_w=None):
    T, D = h.shape
    rb = ROW_BLOCK
    n_blocks = -(-(T * TOP_K + N_EXPERTS * (rb - 1)) // rb) + GATHER_AHEAD
    cnt = counts[0, :N_EXPERTS].astype(I32)
    padded = ((cnt + rb - 1) // rb) * rb
    pends = jnp.cumsum(padded)
    pstart_row = _pad_cols((pends - padded).astype(F32).reshape(1, N_EXPERTS), LANES)
    d0b, d1b = rank(id0, id1, pstart_row)
    d0, d1 = d0b[:, 0], d1b[:, 0]
    fill_start = jnp.concatenate([pends - padded + cnt, pends[-1:]])
    fill_end = jnp.concatenate([pends, jnp.full((1,), n_blocks * rb, I32)])
    row_tok = invert(d0, d1, jnp.stack([fill_start, fill_end], axis=1).reshape(-1), n_blocks * rb)
    block_start = jnp.arange(n_blocks, dtype=I32) * rb
    block_exp = jnp.minimum(jnp.sum(pends[None, :] <= block_start[:, None], axis=1), N_EXPERTS - 1).astype(I32)
    n_used = (pends[-1:] // rb).astype(I32)
    yb = experts(block_exp, row_tok, n_used, hn, w_gate_up, w_down, layer)
    return combine(d0, d1, yb, h, g0, g1, final_norm_w)


def _router_weights(w_group, b_group, w_router, b_router):
    w = _pad_cols(jnp.concatenate([w_router, w_group], axis=1), LANES)
    b = _pad_cols(jnp.concatenate([b_router, b_group]).reshape(1, -1), LANES)
    hi = w.astype(BF16)
    lo = (w - hi.astype(F32)).astype(BF16)
    return jnp.concatenate([hi, lo], axis=1), b


def kernel(x, mix_norm, gla_w_in, gla_w_gk2, gla_b_gk, gla_head_norm, gla_w_out, conv_w_in, conv_w,
           conv_w_out, ffn_norm, w_group, b_group, w_router, b_router, w_gate_up, w_down, final_norm):
    B, T, D = x.shape
    assert B == 1, "the recurrence state is carried across the whole row axis"
    h = x.reshape(T, D)
    kd, vd = D // 2, D
    dk, dv = kd // GLA_HEADS, vd // GLA_HEADS

    w_in = gla_w_in[0]
    n_main = 2 * kd + 2 * vd
    proj, gl = norm_proj(h, mix_norm[0], w_in[:, :n_main].astype(BF16),
                         _pad_cols(w_in[:, n_main:], LANES).astype(BF16))
    w_gk2 = jnp.pad(gla_w_gk2[0], ((0, LANES - GATE_RANK), (0, 0))).astype(BF16)
    o = gla(proj, gl, w_gk2, gla_b_gk[0].reshape(1, kd), gla_head_norm[0].reshape(1, dv), dk=dk, dv=dv)
    routed = out_router(o, gla_w_out[0].astype(BF16), h, ffn_norm[0],
                        *_router_weights(w_group[0], b_group[0], w_router[0], b_router[0]))
    h = moe(*routed, w_gate_up, w_down, 0)

    proj = norm_proj(h, mix_norm[1], conv_w_in[0].astype(BF16))
    routed = out_router(proj, conv_w_out[0].astype(BF16), h, ffn_norm[1],
                        *_router_weights(w_group[1], b_group[1], w_router[1], b_router[1]),
                        conv_w=conv_w[0])
    out = moe(*routed, w_gate_up, w_down, 1, final_norm_w=final_norm)
    return out.reshape(B, T, D)
```

```python
import functools

import jax
import jax.numpy as jnp
from jax import lax
from jax.experimental import pallas as pl
from jax.experimental.pallas import tpu as pltpu

F32 = jnp.float32
BF16 = jnp.bfloat16
I32 = jnp.int32

EPS = 1e-6
GLA_HEADS = 4
GATE_RANK = 16
GATE_NORMALIZER = 16.0
CONV_WIDTH = 3
N_GROUPS = 4
EXPERTS_PER_GROUP = 8
N_EXPERTS = N_GROUPS * EXPERTS_PER_GROUP
TOP_K = 2

LANES = 128
SUBLANES = 8
BF16_ROWS = 16
HIGH_HALF = -65536
VMEM_LIMIT = 56 * 1024 * 1024

GLA_CHUNK = 128
GLA_SUB = 16
GLA_BREF_ROWS = -(-(GLA_CHUNK // GLA_SUB + 1) // SUBLANES) * SUBLANES
GLA_GROUP = 128
GLA_SAFE_DECAY = 60.0
ROW_BLOCK = 256
GATHER_AHEAD = 2
ROW_QUEUE, WEIGHT_QUEUE = 0, 1


def _tile(n, pref):
    t = min(n, pref)
    while n % t:
        t -= LANES
    assert t > 0, (n, pref)
    return t


def _params(*sem):
    return pltpu.CompilerParams(dimension_semantics=sem, vmem_limit_bytes=VMEM_LIMIT)


def _dot(a, b):
    return jnp.dot(a, b, preferred_element_type=F32)


def _dot_nt(a, b):
    return lax.dot_general(a, b, (((1,), (1,)), ((), ())), preferred_element_type=F32)


def _dot_tn(a, b):
    return lax.dot_general(a, b, (((0,), (0,)), ((), ())), preferred_element_type=F32)


def _rms(x, w):
    return x * lax.rsqrt(jnp.mean(x * x, axis=-1, keepdims=True) + EPS) * w


def _packed_sublanes(d):
    assert d % (2 * LANES) == 0
    return d // (2 * LANES)


def _store_packed(ref, x):
    rows, d = x.shape
    half = d // 2
    sub = half // LANES
    lo = lax.shift_right_logical(lax.bitcast_convert_type(x[:, :half].astype(BF16).astype(F32), I32), 16)
    hi = lax.bitcast_convert_type(x[:, half:].astype(BF16).astype(F32), I32) & HIGH_HALF
    words = hi | lo
    for s in range(sub):
        ref[pl.ds(s, rows, stride=sub), :] = words[:, s * LANES:(s + 1) * LANES]


def _load_packed(ref, rows, sub):
    los, his = [], []
    for s in range(sub):
        w = ref[pl.ds(s, rows, stride=sub), :]
        los.append(lax.bitcast_convert_type(lax.shift_left(w, 16), F32))
        his.append(lax.bitcast_convert_type(w & HIGH_HALF, F32))
    return los, his


def _norm_proj_kernel(*refs, with_extra):
    if with_extra:
        x_ref, nw_ref, w_ref, we_ref, o_ref, e_ref, xn_ref = refs
    else:
        x_ref, nw_ref, w_ref, o_ref, xn_ref = refs

    @pl.when(pl.program_id(1) == 0)
    def _():
        xn_ref[...] = _rms(x_ref[...], nw_ref[...]).astype(BF16)
        if with_extra:
            e_ref[...] = _dot(xn_ref[...], we_ref[...])

    o_ref[...] = _dot(xn_ref[...], w_ref[...]).astype(o_ref.dtype)


def norm_proj(x, norm_w, w, w_extra=None):
    T, D = x.shape
    N = w.shape[1]
    tm, tn = _tile(T, 1024), _tile(N, 2048)
    in_specs = [
        pl.BlockSpec((tm, D), lambda i, j: (i, 0)),
        pl.BlockSpec((1, D), lambda i, j: (0, 0)),
        pl.BlockSpec((D, tn), lambda i, j: (0, j)),
    ]
    out_shape = [jax.ShapeDtypeStruct((T, N), BF16)]
    out_specs = [pl.BlockSpec((tm, tn), lambda i, j: (i, j))]
    args = [x, norm_w.reshape(1, D), w]
    if w_extra is not None:
        in_specs.append(pl.BlockSpec((D, LANES), lambda i, j: (0, 0)))
        out_shape.append(jax.ShapeDtypeStruct((T, LANES), F32))
        out_specs.append(pl.BlockSpec((tm, LANES), lambda i, j: (i, 0)))
        args.append(w_extra)
    out = pl.pallas_call(
        functools.partial(_norm_proj_kernel, with_extra=w_extra is not None),
        grid=(T // tm, N // tn),
        in_specs=in_specs,
        out_specs=out_specs,
        out_shape=out_shape,
        scratch_shapes=[pltpu.VMEM((tm, D), BF16)],
        compiler_params=_params("arbitrary", "arbitrary"),
        name="norm_proj",
    )(*args)
    return out if w_extra is not None else out[0]


def _gla_kernel(q_ref, k_ref, v_ref, g_ref, gl_ref, wgk_ref, bgk_ref, hnorm_ref, o_ref,
                st_ref, bs_ref, qt_ref, kh_ref, bref_ref, *, dk, dv, tb):
    C, SUB, GRP, H, BR = GLA_CHUNK, GLA_SUB, GLA_GROUP, GLA_HEADS, GLA_BREF_ROWS
    NS = C // SUB
    CPG = GRP // C
    kd = H * dk
    scale = dk ** -0.5

    @pl.when(pl.program_id(0) == 0)
    def _():
        st_ref[...] = jnp.zeros_like(st_ref)

    r = lax.broadcasted_iota(I32, (GRP, GRP), 0)
    c = lax.broadcasted_iota(I32, (GRP, GRP), 1)
    same = (r // SUB) == (c // SUB)
    low = jnp.where(same & (c <= r), 1.0, 0.0).astype(BF16)
    upp = jnp.where(same & (c > r), 1.0, 0.0).astype(BF16)
    rr = lax.broadcasted_iota(I32, (CPG * BR, GRP), 0)
    cc = lax.broadcasted_iota(I32, (CPG * BR, GRP), 1)
    blk = rr % BR
    mref = jnp.where(((cc // C) == (rr // BR)) & ((cc % C) < SUB * blk) & (blk <= NS),
                     1.0, 0.0).astype(BF16)

    for gi in range(tb // GRP):
        sl = pl.ds(gi * GRP, GRP)
        z = _dot(gl_ref[sl, :].astype(BF16), wgk_ref[...]) + bgk_ref[...]
        la = (jnp.minimum(z, 0.0) - jnp.log(1.0 + jnp.exp(-jnp.abs(z)))) * (1.0 / GATE_NORMALIZER)
        hi = la.astype(BF16)
        lo = (la - hi.astype(F32)).astype(BF16)
        bs = _dot(low, hi) + _dot(low, lo)
        ru = _dot(upp, hi) + _dot(upp, lo)
        br = _dot(mref, hi) + _dot(mref, lo)
        bs_ref[sl, :] = bs
        qt_ref[sl, :] = q_ref[sl, :].astype(F32) * jnp.exp(bs) * scale
        kh_ref[sl, :] = k_ref[sl, :].astype(F32) * jnp.exp(ru)
        bref_ref[pl.ds(gi * CPG, CPG)] = br.reshape(CPG, BR, kd)

    lane = lax.broadcasted_iota(I32, (SUB, C), 1)
    subrow = lax.broadcasted_iota(I32, (SUB, dk), 0)

    def scores_pivoted(qin, kraw, bs, bn):
        neg_b = jnp.concatenate([-(bs[J * SUB:(J + 1) * SUB] + bn[J:J + 1]) for J in range(NS)], axis=0)
        s = _dot_nt(qin.astype(BF16), (kraw * jnp.exp(neg_b)).astype(BF16))
        ri = lax.broadcasted_iota(I32, (C, C), 0)
        ci = lax.broadcasted_iota(I32, (C, C), 1)
        return jnp.where(ci <= ri, s, 0.0)

    def scores_stable(qt, kh, qraw, kraw, bs, bn):
        a_rows = []
        for I in range(NS):
            blk_rows = slice(I * SUB, (I + 1) * SUB)
            q_blk, k_blk, bs_blk = qraw[blk_rows], kraw[blk_rows], bs[blk_rows]
            acc = jnp.zeros((SUB, C), F32)
            for j in range(SUB):
                e = jnp.where(subrow >= j, jnp.exp(bs_blk - bs_blk[j:j + 1]), 0.0)
                p = (q_blk * k_blk[j:j + 1]) * e
                acc = jnp.where(lane == I * SUB + j, jnp.sum(p, axis=-1, keepdims=True), acc)
            if I > 0:
                parts = []
                for J in range(NS):
                    if J < I:
                        parts.append(kh[J * SUB:(J + 1) * SUB] * jnp.exp(bn[I:I + 1] - bn[J + 1:J + 2]))
                    else:
                        parts.append(jnp.zeros((SUB, dk), F32))
                k_dec = jnp.concatenate(parts, axis=0)
                acc = acc + _dot_nt(qt[blk_rows].astype(BF16), k_dec.astype(BF16))
            a_rows.append(acc)
        return jnp.concatenate(a_rows, axis=0)

    def chunk(n, carry, *, pivoted):
        r0 = pl.multiple_of(n * C, C)
        rows = pl.ds(r0, C)
        outs = []
        for h in range(H):
            kl = slice(h * dk, (h + 1) * dk)
            vl = slice(h * dv, (h + 1) * dv)
            bn = bref_ref[n, :, kl]
            eb = jnp.exp(bn)
            st = st_ref[h]
            qt = qt_ref[rows, kl]
            kh = kh_ref[rows, kl]
            bs = bs_ref[rows, kl]
            v = v_ref[rows, vl]
            kraw = k_ref[rows, kl].astype(F32)

            qin = jnp.concatenate([qt[I * SUB:(I + 1) * SUB] * eb[I:I + 1] for I in range(NS)], axis=0)
            o = _dot_nt(qin.astype(BF16), st.astype(BF16))
            if pivoted:
                a = scores_pivoted(qin, kraw, bs, bn)
            else:
                a = scores_stable(qt, kh, q_ref[rows, kl].astype(F32) * scale, kraw, bs, bn)
            o = o + _dot(a.astype(BF16), v)

            k_st = jnp.concatenate(
                [kh[J * SUB:(J + 1) * SUB] * jnp.exp(bn[NS:NS + 1] - bn[J + 1:J + 2]) for J in range(NS)],
                axis=0)
            st_ref[h] = st * eb[NS:NS + 1] + _dot_tn(v, k_st.astype(BF16))
            outs.append(_rms(o, hnorm_ref[...]))

        g = g_ref[rows, :].astype(F32)
        o_ref[rows, :] = (jnp.concatenate(outs, axis=1) * (g * jax.nn.sigmoid(g))).astype(o_ref.dtype)
        return carry

    span = jnp.max(-bref_ref[...])

    @pl.when(span < GLA_SAFE_DECAY)
    def _():
        lax.fori_loop(0, tb // C, functools.partial(chunk, pivoted=True), 0)

    @pl.when(jnp.logical_not(span < GLA_SAFE_DECAY))
    def _():
        lax.fori_loop(0, tb // C, functools.partial(chunk, pivoted=False), 0)


def gla(proj, gl, w_gk2, b_gk, head_norm, *, dk, dv):
    T = proj.shape[0]
    H = GLA_HEADS
    tb = _tile(T, 512)
    assert tb % GLA_GROUP == 0 and dk % LANES == 0 and dv % LANES == 0
    kd, vd = H * dk, H * dv
    assert (2 * kd) % vd == 0
    v_off = (2 * kd) // vd
    fixed = lambda t: (0, 0)
    return pl.pallas_call(
        functools.partial(_gla_kernel, dk=dk, dv=dv, tb=tb),
        grid=(T // tb,),
        in_specs=[
            pl.BlockSpec((tb, kd), lambda t: (t, 0)),
            pl.BlockSpec((tb, kd), lambda t: (t, 1)),
            pl.BlockSpec((tb, vd), lambda t: (t, v_off)),
            pl.BlockSpec((tb, vd), lambda t: (t, v_off + 1)),
            pl.BlockSpec((tb, LANES), lambda t: (t, 0)),
            pl.BlockSpec((LANES, kd), fixed),
            pl.BlockSpec((1, kd), fixed),
            pl.BlockSpec((1, dv), fixed),
        ],
        out_specs=pl.BlockSpec((tb, vd), lambda t: (t, 0)),
        out_shape=jax.ShapeDtypeStruct((T, vd), BF16),
        scratch_shapes=[
            pltpu.VMEM((H, dv, dk), F32),
            pltpu.VMEM((tb, kd), F32),
            pltpu.VMEM((tb, kd), F32),
            pltpu.VMEM((tb, kd), F32),
            pltpu.VMEM((tb // GLA_CHUNK, GLA_BREF_ROWS, kd), F32),
        ],
        compiler_params=_params("arbitrary"),
        name="gla",
    )(proj, proj, proj, proj, gl, w_gk2, b_gk, head_norm)


def _route(hn, rw_ref, rb_ref):
    tm = hn.shape[0]
    hi = hn.astype(BF16)
    lo = (hn - hi.astype(F32)).astype(BF16)
    both = _dot(hi, rw_ref[...])
    logits = both[:, :LANES] + both[:, LANES:] + _dot(lo, rw_ref[:, :LANES]) + rb_ref[...]
    lane = lax.broadcasted_iota(I32, (tm, LANES), 1)
    lane_f = lane.astype(F32)
    neg = -jnp.inf
    far = float(LANES)

    def first_max(vals):
        m = jnp.max(vals, axis=-1, keepdims=True)
        idx = jnp.min(jnp.where(vals == m, lane_f, far), axis=-1, keepdims=True)
        return m, idx

    lg = jnp.where((lane >= N_EXPERTS) & (lane < N_EXPERTS + N_GROUPS), logits, neg)
    mg, gidx = first_max(lg)
    top_gp = 1.0 / jnp.sum(jnp.exp(lg - mg), axis=-1, keepdims=True)
    grp = gidx.astype(I32) - N_EXPERTS
    le = jnp.where((lane < N_EXPERTS) & ((lane // EXPERTS_PER_GROUP) == grp), logits, neg)
    m1, i1 = first_max(le)
    m2, i2 = first_max(jnp.where(lane_f == i1, neg, le))
    e2 = jnp.exp(m2 - m1)
    w1 = 1.0 / (1.0 + e2)
    return i1.astype(I32), i2.astype(I32), top_gp * w1, top_gp * (e2 * w1)


def _out_router_kernel(*refs, conv):
    if conv:
        (pb_ref, pc_ref, ph_ref, hc_ref, hh_ref, cw_ref, w_ref, hin_ref, fnw_ref, rw_ref,
         rb_ref, h_ref, hn_ref, id0_ref, id1_ref, g0_ref, g1_ref, cnt_ref) = refs
    else:
        (y_ref, w_ref, hin_ref, fnw_ref, rw_ref,
         rb_ref, h_ref, hn_ref, id0_ref, id1_ref, g0_ref, g1_ref, cnt_ref) = refs
    i = pl.program_id(0)

    if conv:
        u = pc_ref[...].astype(F32) * ph_ref[...].astype(F32)
        prev = hc_ref[...].astype(F32) * hh_ref[...].astype(F32)
        prev = jnp.where(i > 0, prev, 0.0)
        row = lax.broadcasted_iota(I32, u.shape, 0)
        last, last2 = prev[BF16_ROWS - 1:BF16_ROWS], prev[BF16_ROWS - 2:BF16_ROWS - 1]
        u1 = jnp.where(row == 0, last, pltpu.roll(u, 1, 0))
        u2 = jnp.where(row == 0, last2, jnp.where(row == 1, last, pltpu.roll(u, 2, 0)))
        cw = cw_ref[...]
        y = (pb_ref[...].astype(F32) * (cw[0:1] * u2 + cw[1:2] * u1 + cw[2:3] * u)).astype(BF16)
    else:
        y = y_ref[...]

    h = hin_ref[...] + _dot(y, w_ref[...])
    h_ref[...] = h
    hn = _rms(h, fnw_ref[...])
    _store_packed(hn_ref, hn)
    i1, i2, g1, g2 = _route(hn, rw_ref, rb_ref)
    shape = id0_ref.shape
    id0_ref[...] = jnp.broadcast_to(i1, shape)
    id1_ref[...] = jnp.broadcast_to(i2, shape)
    g0_ref[...] = jnp.broadcast_to(g1, shape)
    g1_ref[...] = jnp.broadcast_to(g2, shape)

    @pl.when(i == 0)
    def _():
        cnt_ref[...] = jnp.zeros_like(cnt_ref)

    lane = lax.broadcasted_iota(I32, shape, 1)
    hit = jnp.where((lane == i1) | (lane == i2), 1.0, 0.0)
    cnt_ref[...] += jnp.sum(hit, axis=0, keepdims=True)


def out_router(y, w_out, h_in, ffn_norm_w, r_w, r_b, conv_w=None):
    T, D = h_in.shape
    K = w_out.shape[0]
    tm = _tile(T, 256)
    conv = conv_w is not None
    row = lambda i: (i, 0)
    fixed = lambda i: (0, 0)
    if conv:
        hb = tm // BF16_ROWS
        halo = lambda col: (lambda i: (jnp.maximum(i * hb - 1, 0), col))
        in_specs = [
            pl.BlockSpec((tm, K), lambda i: (i, 0)),
            pl.BlockSpec((tm, K), lambda i: (i, 1)),
            pl.BlockSpec((tm, K), lambda i: (i, 2)),
            pl.BlockSpec((BF16_ROWS, K), halo(1)),
            pl.BlockSpec((BF16_ROWS, K), halo(2)),
            pl.BlockSpec((SUBLANES, K), fixed),
        ]
        cw = jnp.zeros((SUBLANES, K), F32).at[:CONV_WIDTH].set(conv_w)
        args = [y, y, y, y, y, cw]
    else:
        in_specs = [pl.BlockSpec((tm, K), row)]
        args = [y]
    in_specs += [
        pl.BlockSpec((K, D), fixed),
        pl.BlockSpec((tm, D), row),
        pl.BlockSpec((1, D), fixed),
        pl.BlockSpec((D, 2 * LANES), fixed),
        pl.BlockSpec((1, LANES), fixed),
    ]
    args += [w_out, h_in, ffn_norm_w.reshape(1, D), r_w, r_b]
    wide = lambda dt: jax.ShapeDtypeStruct((T, LANES), dt)
    sub = _packed_sublanes(D)
    return pl.pallas_call(
        functools.partial(_out_router_kernel, conv=conv),
        grid=(T // tm,),
        in_specs=in_specs,
        out_specs=[pl.BlockSpec((tm, D), row), pl.BlockSpec((tm * sub, LANES), row)]
        + [pl.BlockSpec((tm, LANES), row)] * 4 + [pl.BlockSpec((1, LANES), fixed)],
        out_shape=[jax.ShapeDtypeStruct((T, D), F32), jax.ShapeDtypeStruct((T * sub, LANES), I32),
                   wide(I32), wide(I32), wide(F32), wide(F32), jax.ShapeDtypeStruct((1, LANES), F32)],
        compiler_params=_params("arbitrary"),
        name="out_router_conv" if conv else "out_router",
    )(*args)


def _rank_kernel(id0_ref, id1_ref, pst_ref, d0_ref, d1_ref, carry_ref):
    @pl.when(pl.program_id(0) == 0)
    def _():
        carry_ref[...] = jnp.zeros_like(carry_ref)

    tb = id0_ref.shape[0]
    lane = lax.broadcasted_iota(I32, (tb, LANES), 1)
    oh0 = lane == id0_ref[...]
    oh1 = lane == id1_ref[...]
    hit = jnp.where(oh0 | oh1, 1.0, 0.0)
    r = lax.broadcasted_iota(I32, (tb, tb), 0)
    c = lax.broadcasted_iota(I32, (tb, tb), 1)
    before = jnp.where(c < r, 1.0, 0.0).astype(BF16)
    base = _dot(before, hit.astype(BF16)) + carry_ref[...] + pst_ref[...]
    d0 = jnp.sum(jnp.where(oh0, base, 0.0), axis=-1, keepdims=True)
    d1 = jnp.sum(jnp.where(oh1, base, 0.0), axis=-1, keepdims=True)
    d0_ref[...] = jnp.broadcast_to(d0.astype(I32), (tb, LANES))
    d1_ref[...] = jnp.broadcast_to(d1.astype(I32), (tb, LANES))
    carry_ref[...] += jnp.sum(hit, axis=0, keepdims=True)


def rank(id0, id1, pstart_row):
    T = id0.shape[0]
    tb = _tile(T, 1024)
    row = lambda i: (i, 0)
    return pl.pallas_call(
        _rank_kernel,
        grid=(T // tb,),
        in_specs=[pl.BlockSpec((tb, LANES), row), pl.BlockSpec((tb, LANES), row),
                  pl.BlockSpec((1, LANES), lambda i: (0, 0))],
        out_specs=[pl.BlockSpec((tb, LANES), row)] * 2,
        out_shape=[jax.ShapeDtypeStruct((T, LANES), I32)] * 2,
        scratch_shapes=[pltpu.VMEM((1, LANES), F32)],
        compiler_params=_params("arbitrary"),
        name="rank",
    )(id0, id1, pstart_row)


def _invert_kernel(d0_ref, d1_ref, fill_ref, rt_ref):
    width = SUBLANES

    def clear(p, carry):
        for k in range(width):
            rt_ref[p * width + k] = 0
        return carry

    for e in range(fill_ref.shape[0] // 2):
        lax.fori_loop(fill_ref[2 * e] // width, fill_ref[2 * e + 1] // width, clear, 0)

    def place(t, carry):
        rt_ref[d0_ref[t]] = t
        rt_ref[d1_ref[t]] = t
        return carry

    lax.fori_loop(0, d0_ref.shape[0], place, 0, unroll=16)


def invert(d0, d1, fill_ranges, n_rows):
    smem = pl.BlockSpec(memory_space=pltpu.SMEM)
    return pl.pallas_call(
        _invert_kernel,
        in_specs=[smem, smem, smem],
        out_specs=smem,
        out_shape=jax.ShapeDtypeStruct((n_rows,), I32),
        name="invert",
    )(d0, d1, fill_ranges)


def _row_copy(src_hbm, dst_ref, sem, src_row, dst_row, sub):
    src = src_hbm.at[pl.ds(pl.multiple_of(src_row * sub, sub), sub), :]
    dst = dst_ref.at[pl.ds(pl.multiple_of(dst_row * sub, sub), sub), :]
    return pltpu.make_async_copy(src, dst, sem)


def _experts_kernel(bexp_ref, first_ref, wslot_ref, next_ref, rtok_ref, nused_ref, hn_hbm, wgu_hbm, wd_hbm, y_ref,
                    xbuf0, xbuf1, xbuf2, wgu_f32, wd_f32, wgu_bf, wd_bf, sem, wsem, *, layer, ff, rb, sub):
    i = pl.program_id(0)
    n_used = nused_ref[0]
    bufs = (xbuf0, xbuf1, xbuf2)
    n_buf = len(bufs)
    assert n_buf == GATHER_AHEAD + 1

    def weight_copies(expert, slot):
        return (pltpu.make_async_copy(wgu_hbm.at[layer, expert], wgu_f32.at[slot], wsem.at[0, slot]),
                pltpu.make_async_copy(wd_hbm.at[layer, expert], wd_f32.at[slot], wsem.at[1, slot]))

    def wait_rows(s):
        pltpu.make_async_copy(hn_hbm.at[pl.ds(0, rb * sub), :], bufs[s], sem.at[s]).wait()

    def start_rows(blk, s):
        for r in range(rb):
            _row_copy(hn_hbm, bufs[s], sem.at[s], rtok_ref[blk * rb + r], r, sub).start(priority=ROW_QUEUE)

    @pl.when(i == 0)
    def _():
        for cp in weight_copies(bexp_ref[0], 0):
            cp.start(priority=WEIGHT_QUEUE)
        for blk in range(GATHER_AHEAD):
            start_rows(blk, blk)

    @pl.when(first_ref[i] == 1)
    def _():
        slot = wslot_ref[i]
        for cp in weight_copies(bexp_ref[i], slot):
            cp.wait()
        nxt = next_ref[i]

        @pl.when(nxt >= 0)
        def _():
            for cp in weight_copies(nxt, 1 - slot):
                cp.start(priority=WEIGHT_QUEUE)

        wgu_bf[...] = wgu_f32[slot].astype(BF16)
        wd_bf[...] = wd_f32[slot].astype(BF16)

    for s in range(n_buf):
        @pl.when((i < n_used) & (i % n_buf == s))
        def _():
            wait_rows(s)
            start_rows(i + GATHER_AHEAD, (s + GATHER_AHEAD) % n_buf)
            los, his = _load_packed(bufs[s], rb, sub)
            x = jnp.concatenate([p.astype(BF16) for p in los + his], axis=1)
            gu = _dot(x, wgu_bf[...])
            gt, up = gu[:, :ff], gu[:, ff:]
            act = (gt * jax.nn.sigmoid(gt) * up).astype(BF16)
            _store_packed(y_ref, _dot(act, wd_bf[...]))

        @pl.when((i >= n_used) & (i < n_used + GATHER_AHEAD) & (i % n_buf == s))
        def _():
            wait_rows(s)

    @pl.when(i >= n_used)
    def _():
        y_ref[...] = jnp.zeros_like(y_ref)


def experts(block_exp, row_tok, n_used, hn, w_gate_up, w_down, layer):
    D, ff2 = w_gate_up.shape[-2:]
    ff = ff2 // 2
    sub = _packed_sublanes(D)
    P = row_tok.shape[0]
    rb = ROW_BLOCK
    n_blocks = P // rb

    blk = jnp.arange(n_blocks, dtype=I32)
    prev = jnp.concatenate([jnp.full((1,), -1, I32), block_exp[:-1]])
    first = (blk < n_used[0]) & (block_exp != prev)
    wslot = (jnp.cumsum(first.astype(I32)) - 1) % 2
    first_at = lax.cummin(jnp.where(first, blk, n_blocks), reverse=True)
    next_first = jnp.concatenate([first_at[1:], jnp.full((1,), n_blocks, I32)])
    next_exp = jnp.where(next_first < n_blocks, block_exp[jnp.minimum(next_first, n_blocks - 1)], -1)

    grid_spec = pltpu.PrefetchScalarGridSpec(
        num_scalar_prefetch=6,
        grid=(n_blocks,),
        in_specs=[pl.BlockSpec(memory_space=pl.ANY)] * 3,
        out_specs=pl.BlockSpec((rb * sub, LANES), lambda i, *_: (i, 0)),
        scratch_shapes=[pltpu.VMEM((rb * sub, LANES), I32)] * (GATHER_AHEAD + 1)
        + [pltpu.VMEM((2, D, ff2), F32), pltpu.VMEM((2, ff, D), F32),
           pltpu.VMEM((D, ff2), BF16), pltpu.VMEM((ff, D), BF16),
           pltpu.SemaphoreType.DMA((GATHER_AHEAD + 1,)), pltpu.SemaphoreType.DMA((2, 2))],
    )
    return pl.pallas_call(
        functools.partial(_experts_kernel, layer=layer, ff=ff, rb=rb, sub=sub),
        grid_spec=grid_spec,
        out_shape=jax.ShapeDtypeStruct((P * sub, LANES), I32),
        compiler_params=_params("arbitrary"),
        name="experts",
    )(block_exp, first.astype(I32), wslot.astype(I32), next_exp.astype(I32), row_tok, n_used,
      hn, w_gate_up, w_down)


def _combine_kernel(d0_ref, d1_ref, yb_hbm, h_ref, g0_ref, g1_ref, nw_ref, o_ref, b00, b01, b10, b11, sem,
                    *, final_norm, sub):
    i = pl.program_id(0)
    n = pl.num_programs(0)
    tc = o_ref.shape[0]
    bufs = ((b00, b01), (b10, b11))
    d_refs = (d0_ref, d1_ref)

    def wait_rows(s):
        for k in range(TOP_K):
            pltpu.make_async_copy(yb_hbm.at[pl.ds(0, tc * sub), :], bufs[s][k], sem.at[s]).wait()

    def start_rows(blk, s):
        for r in range(tc):
            for k in range(TOP_K):
                _row_copy(yb_hbm, bufs[s][k], sem.at[s], d_refs[k][blk * tc + r], r, sub).start(priority=k)

    @pl.when(i == 0)
    def _():
        start_rows(0, 0)

    for s in range(2):
        @pl.when(i % 2 == s)
        def _():
            wait_rows(s)
            start_rows(i + 1, 1 - s)
            lo0, hi0 = _load_packed(bufs[s][0], tc, sub)
            lo1, hi1 = _load_packed(bufs[s][1], tc, sub)
            g0, g1 = g0_ref[...], g1_ref[...]
            pieces = [h_ref[:, j * LANES:(j + 1) * LANES] + g0 * y0 + g1 * y1
                      for j, (y0, y1) in enumerate(zip(lo0 + hi0, lo1 + hi1))]
            if final_norm:
                o_ref[...] = _rms(jnp.concatenate(pieces, axis=1), nw_ref[...])
            else:
                for j, piece in enumerate(pieces):
                    o_ref[:, j * LANES:(j + 1) * LANES] = piece

            @pl.when(i == n - 1)
            def _():
                wait_rows(1 - s)


def combine(d0, d1, yb, h, g0, g1, norm_w=None):
    T, D = h.shape
    tc = _tile(T, 256)
    sub = _packed_sublanes(D)
    final_norm = norm_w is not None
    nw = (norm_w if final_norm else jnp.ones((D,), F32)).reshape(1, D)
    spare = jnp.zeros((tc,), I32)
    row = lambda i, a, b: (i, 0)
    grid_spec = pltpu.PrefetchScalarGridSpec(
        num_scalar_prefetch=2,
        grid=(T // tc,),
        in_specs=[
            pl.BlockSpec(memory_space=pl.ANY),
            pl.BlockSpec((tc, D), row),
            pl.BlockSpec((tc, LANES), row),
            pl.BlockSpec((tc, LANES), row),
            pl.BlockSpec((1, D), lambda i, a, b: (0, 0)),
        ],
        out_specs=pl.BlockSpec((tc, D), row),
        scratch_shapes=[pltpu.VMEM((tc * sub, LANES), I32)] * (2 * TOP_K) + [pltpu.SemaphoreType.DMA((2,))],
    )
    return pl.pallas_call(
        functools.partial(_combine_kernel, final_norm=final_norm, sub=sub),
        grid_spec=grid_spec,
        out_shape=jax.ShapeDtypeStruct((T, D), F32),
        compiler_params=_params("arbitrary"),
        name="combine_norm" if final_norm else "combine",
    )(jnp.concatenate([d0, spare]), jnp.concatenate([d1, spare]), yb, h, g0, g1, nw)


def _pad_cols(w, n):
    return jnp.pad(w, ((0, 0), (0, n - w.shape[1])))


def moe(h, hn, id0, id1, g0, g1, counts, w_gate_up, w_down, layer, final_norm_w=None):
    T, D = h.shape
    rb = ROW_BLOCK
    n_blocks = -(-(T * TOP_K + N_EXPERTS * (rb - 1)) // rb) + GATHER_AHEAD
    cnt = counts[0, :N_EXPERTS].astype(I32)
    padded = ((cnt + rb - 1) // rb) * rb
    pends = jnp.cumsum(padded)
    pstart_row = _pad_cols((pends - padded).astype(F32).reshape(1, N_EXPERTS), LANES)
    d0b, d1b = rank(id0, id1, pstart_row)
    d0, d1 = d0b[:, 0], d1b[:, 0]
    fill_start = jnp.concatenate([pends - padded + cnt, pends[-1:]])
    fill_end = jnp.concatenate([pends, jnp.full((1,), n_blocks * rb, I32)])
    row_tok = invert(d0, d1, jnp.stack([fill_start, fill_end], axis=1).reshape(-1), n_blocks * rb)
    block_start = jnp.arange(n_blocks, dtype=I32) * rb
    block_exp = jnp.minimum(jnp.sum(pends[None, :] <= block_start[:, None], axis=1), N_EXPERTS - 1).astype(I32)
    n_used = (pends[-1:] // rb).astype(I32)
    yb = experts(block_exp, row_tok, n_used, hn, w_gate_up, w_down, layer)
    return combine(d0, d1, yb, h, g0, g1, final_norm_w)


def _router_weights(w_group, b_group, w_router, b_router):
    w = _pad_cols(jnp.concatenate([w_router, w_group], axis=1), LANES)
    b = _pad_cols(jnp.concatenate([b_router, b_group]).reshape(1, -1), LANES)
    hi = w.astype(BF16)
    lo = (w - hi.astype(F32)).astype(BF16)
    return jnp.concatenate([hi, lo], axis=1), b


def kernel(x, mix_norm, gla_w_in, gla_w_gk2, gla_b_gk, gla_head_norm, gla_w_out, conv_w_in, conv_w,
           conv_w_out, ffn_norm, w_group, b_group, w_router, b_router, w_gate_up, w_down, final_norm):
    B, T, D = x.shape
    assert B == 1, "the recurrence state is carried across the whole row axis"
    h = x.reshape(T, D)
    kd, vd = D // 2, D
    dk, dv = kd // GLA_HEADS, vd // GLA_HEADS

    w_in = gla_w_in[0]
    n_main = 2 * kd + 2 * vd
    proj, gl = norm_proj(h, mix_norm[0], w_in[:, :n_main].astype(BF16),
                         _pad_cols(w_in[:, n_main:], LANES).astype(BF16))
    w_gk2 = jnp.pad(gla_w_gk2[0], ((0, LANES - GATE_RANK), (0, 0))).astype(BF16)
    o = gla(proj, gl, w_gk2, gla_b_gk[0].reshape(1, kd), gla_head_norm[0].reshape(1, dv), dk=dk, dv=dv)
    routed = out_router(o, gla_w_out[0].astype(BF16), h, ffn_norm[0],
                        *_router_weights(w_group[0], b_group[0], w_router[0], b_router[0]))
    h = moe(*routed, w_gate_up, w_down, 0)

    proj = norm_proj(h, mix_norm[1], conv_w_in[0].astype(BF16))
    routed = out_router(proj, conv_w_out[0].astype(BF16), h, ffn_norm[1],
                        *_router_weights(w_group[1], b_group[1], w_router[1], b_router[1]),
                        conv_w=conv_w[0])
    out = moe(*routed, w_gate_up, w_down, 1, final_norm_w=final_norm)
    return out.reshape(B, T, D)
```

```python
import functools

import jax
import jax.numpy as jnp
from jax import lax
from jax.experimental import pallas as pl
from jax.experimental.pallas import tpu as pltpu

F32 = jnp.float32
BF16 = jnp.bfloat16
I32 = jnp.int32

EPS = 1e-6
GLA_HEADS = 4
GATE_RANK = 16
GATE_NORMALIZER = 16.0
CONV_WIDTH = 3
N_GROUPS = 4
EXPERTS_PER_GROUP = 8
N_EXPERTS = N_GROUPS * EXPERTS_PER_GROUP
TOP_K = 2

LANES = 128
SUBLANES = 8
BF16_ROWS = 16
HIGH_HALF = -65536
VMEM_LIMIT = 56 * 1024 * 1024

GLA_CHUNK = 128
GLA_SUB = 16
GLA_BREF_ROWS = -(-(GLA_CHUNK // GLA_SUB + 1) // SUBLANES) * SUBLANES
GLA_GROUP = 128
GLA_SAFE_DECAY = 60.0
ROW_BLOCK = 256
GATHER_AHEAD = 2
ROW_QUEUE, WEIGHT_QUEUE = 0, 1


def _tile(n, pref):
    t = min(n, pref)
    while n % t:
        t -= LANES
    assert t > 0, (n, pref)
    return t


def _params(*sem):
    return pltpu.CompilerParams(dimension_semantics=sem, vmem_limit_bytes=VMEM_LIMIT)


def _dot(a, b):
    return jnp.dot(a, b, preferred_element_type=F32)


def _dot_nt(a, b):
    return lax.dot_general(a, b, (((1,), (1,)), ((), ())), preferred_element_type=F32)


def _dot_tn(a, b):
    return lax.dot_general(a, b, (((0,), (0,)), ((), ())), preferred_element_type=F32)


def _rms(x, w):
    return x * lax.rsqrt(jnp.mean(x * x, axis=-1, keepdims=True) + EPS) * w


def _packed_sublanes(d):
    assert d % (2 * LANES) == 0
    return d // (2 * LANES)


def _store_packed(ref, x):
    rows, d = x.shape
    half = d // 2
    sub = half // LANES
    lo = lax.shift_right_logical(lax.bitcast_convert_type(x[:, :half].astype(BF16).astype(F32), I32), 16)
    hi = lax.bitcast_convert_type(x[:, half:].astype(BF16).astype(F32), I32) & HIGH_HALF
    words = hi | lo
    for s in range(sub):
        ref[pl.ds(s, rows, stride=sub), :] = words[:, s * LANES:(s + 1) * LANES]


def _load_packed(ref, rows, sub):
    los, his = [], []
    for s in range(sub):
        w = ref[pl.ds(s, rows, stride=sub), :]
        los.append(lax.bitcast_convert_type(lax.shift_left(w, 16), F32))
        his.append(lax.bitcast_convert_type(w & HIGH_HALF, F32))
    return los, his


def _norm_proj_kernel(*refs, with_extra):
    if with_extra:
        x_ref, nw_ref, w_ref, we_ref, o_ref, e_ref, xn_ref = refs
    else:
        x_ref, nw_ref, w_ref, o_ref, xn_ref = refs

    @pl.when(pl.program_id(1) == 0)
    def _():
        xn_ref[...] = _rms(x_ref[...], nw_ref[...]).astype(BF16)
        if with_extra:
            e_ref[...] = _dot(xn_ref[...], we_ref[...])

    o_ref[...] = _dot(xn_ref[...], w_ref[...]).astype(o_ref.dtype)


def norm_proj(x, norm_w, w, w_extra=None, n_cols=None):
    T, D = x.shape
    N = w.shape[1] if n_cols is None else n_cols
    tm, tn = _tile(T, 1024), _tile(N, 2048)
    in_specs = [
        pl.BlockSpec((tm, D), lambda i, j: (i, 0)),
        pl.BlockSpec((1, D), lambda i, j: (0, 0)),
        pl.BlockSpec((D, tn), lambda i, j: (0, j)),
    ]
    out_shape = [jax.ShapeDtypeStruct((T, N), BF16)]
    out_specs = [pl.BlockSpec((tm, tn), lambda i, j: (i, j))]
    args = [x, norm_w.reshape(1, D), w]
    if w_extra is not None:
        in_specs.append(pl.BlockSpec((D, LANES), lambda i, j: (0, 0)))
        out_shape.append(jax.ShapeDtypeStruct((T, LANES), F32))
        out_specs.append(pl.BlockSpec((tm, LANES), lambda i, j: (i, 0)))
        args.append(w_extra)
    out = pl.pallas_call(
        functools.partial(_norm_proj_kernel, with_extra=w_extra is not None),
        grid=(T // tm, N // tn),
        in_specs=in_specs,
        out_specs=out_specs,
        out_shape=out_shape,
        scratch_shapes=[pltpu.VMEM((tm, D), BF16)],
        compiler_params=_params("arbitrary", "arbitrary"),
        name="norm_proj",
    )(*args)
    return out if w_extra is not None else out[0]


def _gla_kernel(q_ref, k_ref, v_ref, g_ref, gl_ref, wgk_ref, bgk_ref, hnorm_ref, o_ref,
                st_ref, bs_ref, qt_ref, kh_ref, bref_ref, *, dk, dv, tb):
    C, SUB, GRP, H, BR = GLA_CHUNK, GLA_SUB, GLA_GROUP, GLA_HEADS, GLA_BREF_ROWS
    NS = C // SUB
    CPG = GRP // C
    kd = H * dk
    scale = dk ** -0.5

    @pl.when(pl.program_id(0) == 0)
    def _():
        st_ref[...] = jnp.zeros_like(st_ref)

    r = lax.broadcasted_iota(I32, (GRP, GRP), 0)
    c = lax.broadcasted_iota(I32, (GRP, GRP), 1)
    same = (r // SUB) == (c // SUB)
    low = jnp.where(same & (c <= r), 1.0, 0.0).astype(BF16)
    upp = jnp.where(same & (c > r), 1.0, 0.0).astype(BF16)
    rr = lax.broadcasted_iota(I32, (CPG * BR, GRP), 0)
    cc = lax.broadcasted_iota(I32, (CPG * BR, GRP), 1)
    blk = rr % BR
    mref = jnp.where(((cc // C) == (rr // BR)) & ((cc % C) < SUB * blk) & (blk <= NS),
                     1.0, 0.0).astype(BF16)

    for gi in range(tb // GRP):
        sl = pl.ds(gi * GRP, GRP)
        z = _dot(gl_ref[sl, :].astype(BF16), wgk_ref[...]) + bgk_ref[...]
        la = (jnp.minimum(z, 0.0) - jnp.log(1.0 + jnp.exp(-jnp.abs(z)))) * (1.0 / GATE_NORMALIZER)
        hi = la.astype(BF16)
        lo = (la - hi.astype(F32)).astype(BF16)
        bs = _dot(low, hi) + _dot(low, lo)
        ru = _dot(upp, hi) + _dot(upp, lo)
        br = _dot(mref, hi) + _dot(mref, lo)
        bs_ref[sl, :] = bs
        qt_ref[sl, :] = q_ref[sl, :].astype(F32) * jnp.exp(bs) * scale
        kh_ref[sl, :] = k_ref[sl, :].astype(F32) * jnp.exp(ru)
        bref_ref[pl.ds(gi * CPG, CPG)] = br.reshape(CPG, BR, kd)

    lane = lax.broadcasted_iota(I32, (SUB, C), 1)
    subrow = lax.broadcasted_iota(I32, (SUB, dk), 0)

    def scores_pivoted(qin, kraw, bs, bn):
        neg_b = jnp.concatenate([-(bs[J * SUB:(J + 1) * SUB] + bn[J:J + 1]) for J in range(NS)], axis=0)
        s = _dot_nt(qin.astype(BF16), (kraw * jnp.exp(neg_b)).astype(BF16))
        ri = lax.broadcasted_iota(I32, (C, C), 0)
        ci = lax.broadcasted_iota(I32, (C, C), 1)
        return jnp.where(ci <= ri, s, 0.0)

    def scores_stable(qt, kh, qraw, kraw, bs, bn):
        a_rows = []
        for I in range(NS):
            blk_rows = slice(I * SUB, (I + 1) * SUB)
            q_blk, k_blk, bs_blk = qraw[blk_rows], kraw[blk_rows], bs[blk_rows]
            acc = jnp.zeros((SUB, C), F32)
            for j in range(SUB):
                e = jnp.where(subrow >= j, jnp.exp(bs_blk - bs_blk[j:j + 1]), 0.0)
                p = (q_blk * k_blk[j:j + 1]) * e
                acc = jnp.where(lane == I * SUB + j, jnp.sum(p, axis=-1, keepdims=True), acc)
            if I > 0:
                parts = []
                for J in range(NS):
                    if J < I:
                        parts.append(kh[J * SUB:(J + 1) * SUB] * jnp.exp(bn[I:I + 1] - bn[J + 1:J + 2]))
                    else:
                        parts.append(jnp.zeros((SUB, dk), F32))
                k_dec = jnp.concatenate(parts, axis=0)
                acc = acc + _dot_nt(qt[blk_rows].astype(BF16), k_dec.astype(BF16))
            a_rows.append(acc)
        return jnp.concatenate(a_rows, axis=0)

    def chunk(n, carry, *, pivoted):
        r0 = pl.multiple_of(n * C, C)
        rows = pl.ds(r0, C)
        outs = []
        for h in range(H):
            kl = slice(h * dk, (h + 1) * dk)
            vl = slice(h * dv, (h + 1) * dv)
            bn = bref_ref[n, :, kl]
            eb = jnp.exp(bn)
            st = st_ref[h]
            qt = qt_ref[rows, kl]
            kh = kh_ref[rows, kl]
            bs = bs_ref[rows, kl]
            v = v_ref[rows, vl]
            kraw = k_ref[rows, kl].astype(F32)

            qin = jnp.concatenate([qt[I * SUB:(I + 1) * SUB] * eb[I:I + 1] for I in range(NS)], axis=0)
            o = _dot_nt(qin.astype(BF16), st.astype(BF16))
            if pivoted:
                a = scores_pivoted(qin, kraw, bs, bn)
            else:
                a = scores_stable(qt, kh, q_ref[rows, kl].astype(F32) * scale, kraw, bs, bn)
            o = o + _dot(a.astype(BF16), v)

            k_st = jnp.concatenate(
                [kh[J * SUB:(J + 1) * SUB] * jnp.exp(bn[NS:NS + 1] - bn[J + 1:J + 2]) for J in range(NS)],
                axis=0)
            st_ref[h] = st * eb[NS:NS + 1] + _dot_tn(v, k_st.astype(BF16))
            outs.append(_rms(o, hnorm_ref[...]))

        g = g_ref[rows, :].astype(F32)
        o_ref[rows, :] = (jnp.concatenate(outs, axis=1) * (g * jax.nn.sigmoid(g))).astype(o_ref.dtype)
        return carry

    span = jnp.max(-bref_ref[...])

    @pl.when(span < GLA_SAFE_DECAY)
    def _():
        lax.fori_loop(0, tb // C, functools.partial(chunk, pivoted=True), 0, unroll=2)

    @pl.when(jnp.logical_not(span < GLA_SAFE_DECAY))
    def _():
        lax.fori_loop(0, tb // C, functools.partial(chunk, pivoted=False), 0)


def gla(proj, gl, w_gk2, b_gk, head_norm, *, dk, dv):
    T = proj.shape[0]
    H = GLA_HEADS
    tb = _tile(T, 512)
    assert tb % GLA_GROUP == 0 and dk % LANES == 0 and dv % LANES == 0
    kd, vd = H * dk, H * dv
    assert (2 * kd) % vd == 0
    v_off = (2 * kd) // vd
    fixed = lambda t: (0, 0)
    return pl.pallas_call(
        functools.partial(_gla_kernel, dk=dk, dv=dv, tb=tb),
        grid=(T // tb,),
        in_specs=[
            pl.BlockSpec((tb, kd), lambda t: (t, 0)),
            pl.BlockSpec((tb, kd), lambda t: (t, 1)),
            pl.BlockSpec((tb, vd), lambda t: (t, v_off)),
            pl.BlockSpec((tb, vd), lambda t: (t, v_off + 1)),
            pl.BlockSpec((tb, LANES), lambda t: (t, 0)),
            pl.BlockSpec((LANES, kd), fixed),
            pl.BlockSpec((1, kd), fixed),
            pl.BlockSpec((1, dv), fixed),
        ],
        out_specs=pl.BlockSpec((tb, vd), lambda t: (t, 0)),
        out_shape=jax.ShapeDtypeStruct((T, vd), BF16),
        scratch_shapes=[
            pltpu.VMEM((H, dv, dk), F32),
            pltpu.VMEM((tb, kd), F32),
            pltpu.VMEM((tb, kd), F32),
            pltpu.VMEM((tb, kd), F32),
            pltpu.VMEM((tb // GLA_CHUNK, GLA_BREF_ROWS, kd), F32),
        ],
        compiler_params=_params("arbitrary"),
        name="gla",
    )(proj, proj, proj, proj, gl, w_gk2, b_gk, head_norm)


def _route(hn, rw_ref, rb_ref):
    tm = hn.shape[0]
    hi = hn.astype(BF16)
    lo = (hn - hi.astype(F32)).astype(BF16)
    both = _dot(hi, rw_ref[...])
    logits = both[:, :LANES] + both[:, LANES:] + _dot(lo, rw_ref[:, :LANES]) + rb_ref[...]
    lane = lax.broadcasted_iota(I32, (tm, LANES), 1)
    lane_f = lane.astype(F32)
    neg = -jnp.inf
    far = float(LANES)

    def first_max(vals):
        m = jnp.max(vals, axis=-1, keepdims=True)
        idx = jnp.min(jnp.where(vals == m, lane_f, far), axis=-1, keepdims=True)
        return m, idx

    lg = jnp.where((lane >= N_EXPERTS) & (lane < N_EXPERTS + N_GROUPS), logits, neg)
    mg, gidx = first_max(lg)
    top_gp = 1.0 / jnp.sum(jnp.exp(lg - mg), axis=-1, keepdims=True)
    grp = gidx.astype(I32) - N_EXPERTS
    le = jnp.where((lane < N_EXPERTS) & ((lane // EXPERTS_PER_GROUP) == grp), logits, neg)
    m1, i1 = first_max(le)
    m2, i2 = first_max(jnp.where(lane_f == i1, neg, le))
    e2 = jnp.exp(m2 - m1)
    w1 = 1.0 / (1.0 + e2)
    return i1.astype(I32), i2.astype(I32), top_gp * w1, top_gp * (e2 * w1)


def _out_router_kernel(*refs, conv):
    if conv:
        (pb_ref, pc_ref, ph_ref, hc_ref, hh_ref, cw_ref, w_ref, hin_ref, fnw_ref, rw_ref,
         rb_ref, h_ref, hn_ref, id0_ref, id1_ref, g0_ref, g1_ref, cnt_ref) = refs
    else:
        (y_ref, w_ref, hin_ref, fnw_ref, rw_ref,
         rb_ref, h_ref, hn_ref, id0_ref, id1_ref, g0_ref, g1_ref, cnt_ref) = refs
    i = pl.program_id(0)

    if conv:
        u = pc_ref[...].astype(F32) * ph_ref[...].astype(F32)
        prev = hc_ref[...].astype(F32) * hh_ref[...].astype(F32)
        prev = jnp.where(i > 0, prev, 0.0)
        last, last2 = prev[BF16_ROWS - 1:BF16_ROWS], prev[BF16_ROWS - 2:BF16_ROWS - 1]
        r1, r2 = pltpu.roll(u, 1, 0), pltpu.roll(u, 2, 0)
        row = lax.broadcasted_iota(I32, (SUBLANES, u.shape[1]), 0)
        top1 = jnp.where(row == 0, last, r1[:SUBLANES])
        top2 = jnp.where(row == 0, last2, jnp.where(row == 1, last, r2[:SUBLANES]))
        u1 = jnp.concatenate([top1, r1[SUBLANES:]], axis=0)
        u2 = jnp.concatenate([top2, r2[SUBLANES:]], axis=0)
        cw = cw_ref[...]
        y = (pb_ref[...].astype(F32) * (cw[0:1] * u2 + cw[1:2] * u1 + cw[2:3] * u)).astype(BF16)
    else:
        y = y_ref[...]

    h = hin_ref[...] + _dot(y, w_ref[...])
    h_ref[...] = h
    hn = _rms(h, fnw_ref[...])
    _store_packed(hn_ref, hn)
    i1, i2, g1, g2 = _route(hn, rw_ref, rb_ref)
    shape = id0_ref.shape
    id0_ref[...] = jnp.broadcast_to(i1, shape)
    id1_ref[...] = jnp.broadcast_to(i2, shape)
    g0_ref[...] = jnp.broadcast_to(g1, shape)
    g1_ref[...] = jnp.broadcast_to(g2, shape)

    @pl.when(i == 0)
    def _():
        cnt_ref[...] = jnp.zeros_like(cnt_ref)

    lane = lax.broadcasted_iota(I32, shape, 1)
    hit = jnp.where((lane == i1) | (lane == i2), 1.0, 0.0)
    cnt_ref[...] += jnp.sum(hit, axis=0, keepdims=True)


def out_router(y, w_out, h_in, ffn_norm_w, r_w, r_b, conv_w=None):
    T, D = h_in.shape
    K = w_out.shape[0]
    tm = _tile(T, 256)
    conv = conv_w is not None
    row = lambda i: (i, 0)
    fixed = lambda i: (0, 0)
    if conv:
        hb = tm // BF16_ROWS
        halo = lambda col: (lambda i: (jnp.maximum(i * hb - 1, 0), col))
        in_specs = [
            pl.BlockSpec((tm, K), lambda i: (i, 0)),
            pl.BlockSpec((tm, K), lambda i: (i, 1)),
            pl.BlockSpec((tm, K), lambda i: (i, 2)),
            pl.BlockSpec((BF16_ROWS, K), halo(1)),
            pl.BlockSpec((BF16_ROWS, K), halo(2)),
            pl.BlockSpec((SUBLANES, K), fixed),
        ]
        cw = jnp.zeros((SUBLANES, K), F32).at[:CONV_WIDTH].set(conv_w)
        args = [y, y, y, y, y, cw]
    else:
        in_specs = [pl.BlockSpec((tm, K), row)]
        args = [y]
    in_specs += [
        pl.BlockSpec((K, D), fixed),
        pl.BlockSpec((tm, D), row),
        pl.BlockSpec((1, D), fixed),
        pl.BlockSpec((D, 2 * LANES), fixed),
        pl.BlockSpec((1, LANES), fixed),
    ]
    args += [w_out, h_in, ffn_norm_w.reshape(1, D), r_w, r_b]
    wide = lambda dt: jax.ShapeDtypeStruct((T, LANES), dt)
    sub = _packed_sublanes(D)
    return pl.pallas_call(
        functools.partial(_out_router_kernel, conv=conv),
        grid=(T // tm,),
        in_specs=in_specs,
        out_specs=[pl.BlockSpec((tm, D), row), pl.BlockSpec((tm * sub, LANES), row)]
        + [pl.BlockSpec((tm, LANES), row)] * 4 + [pl.BlockSpec((1, LANES), fixed)],
        out_shape=[jax.ShapeDtypeStruct((T, D), F32), jax.ShapeDtypeStruct((T * sub, LANES), I32),
                   wide(I32), wide(I32), wide(F32), wide(F32), jax.ShapeDtypeStruct((1, LANES), F32)],
        compiler_params=_params("arbitrary"),
        name="out_router_conv" if conv else "out_router",
    )(*args)


def _rank_kernel(id0_ref, id1_ref, pst_ref, d0_ref, d1_ref, carry_ref):
    @pl.when(pl.program_id(0) == 0)
    def _():
        carry_ref[...] = jnp.zeros_like(carry_ref)

    tb = id0_ref.shape[0]
    lane = lax.broadcasted_iota(I32, (tb, LANES), 1)
    oh0 = lane == id0_ref[...]
    oh1 = lane == id1_ref[...]
    hit = jnp.where(oh0 | oh1, 1.0, 0.0)
    r = lax.broadcasted_iota(I32, (tb, tb), 0)
    c = lax.broadcasted_iota(I32, (tb, tb), 1)
    before = jnp.where(c < r, 1.0, 0.0).astype(BF16)
    base = _dot(before, hit.astype(BF16)) + carry_ref[...] + pst_ref[...]
    d0 = jnp.sum(jnp.where(oh0, base, 0.0), axis=-1, keepdims=True)
    d1 = jnp.sum(jnp.where(oh1, base, 0.0), axis=-1, keepdims=True)
    d0_ref[...] = jnp.broadcast_to(d0.astype(I32), (tb, LANES))
    d1_ref[...] = jnp.broadcast_to(d1.astype(I32), (tb, LANES))
    carry_ref[...] += jnp.sum(hit, axis=0, keepdims=True)


def rank(id0, id1, pstart_row):
    T = id0.shape[0]
    tb = _tile(T, 1024)
    row = lambda i: (i, 0)
    return pl.pallas_call(
        _rank_kernel,
        grid=(T // tb,),
        in_specs=[pl.BlockSpec((tb, LANES), row), pl.BlockSpec((tb, LANES), row),
                  pl.BlockSpec((1, LANES), lambda i: (0, 0))],
        out_specs=[pl.BlockSpec((tb, LANES), row)] * 2,
        out_shape=[jax.ShapeDtypeStruct((T, LANES), I32)] * 2,
        scratch_shapes=[pltpu.VMEM((1, LANES), F32)],
        compiler_params=_params("arbitrary"),
        name="rank",
    )(id0, id1, pstart_row)


def _invert_kernel(d0_ref, d1_ref, fill_ref, rt_ref):
    width = SUBLANES

    def clear(p, carry):
        for k in range(width):
            rt_ref[p * width + k] = 0
        return carry

    for e in range(fill_ref.shape[0] // 2):
        lax.fori_loop(fill_ref[2 * e] // width, fill_ref[2 * e + 1] // width, clear, 0)

    def place(t, carry):
        rt_ref[d0_ref[t]] = t
        rt_ref[d1_ref[t]] = t
        return carry

    lax.fori_loop(0, d0_ref.shape[0], place, 0, unroll=16)


def invert(d0, d1, fill_ranges, n_rows):
    smem = pl.BlockSpec(memory_space=pltpu.SMEM)
    return pl.pallas_call(
        _invert_kernel,
        in_specs=[smem, smem, smem],
        out_specs=smem,
        out_shape=jax.ShapeDtypeStruct((n_rows,), I32),
        name="invert",
    )(d0, d1, fill_ranges)


def _row_copy(src_hbm, dst_ref, sem, src_row, dst_row, sub):
    src = src_hbm.at[pl.ds(pl.multiple_of(src_row * sub, sub), sub), :]
    dst = dst_ref.at[pl.ds(pl.multiple_of(dst_row * sub, sub), sub), :]
    return pltpu.make_async_copy(src, dst, sem)


def _experts_kernel(bexp_ref, first_ref, wslot_ref, next_ref, rtok_ref, nused_ref, hn_hbm, wgu_hbm, wd_hbm, y_ref,
                    xbuf0, xbuf1, xbuf2, wgu_f32, wd_f32, wgu_bf, wd_bf, sem, wsem, *, layer, ff, rb, sub):
    i = pl.program_id(0)
    n_used = nused_ref[0]
    bufs = (xbuf0, xbuf1, xbuf2)
    n_buf = len(bufs)
    assert n_buf == GATHER_AHEAD + 1

    def weight_copies(expert, slot):
        return (pltpu.make_async_copy(wgu_hbm.at[layer, expert], wgu_f32.at[slot], wsem.at[0, slot]),
                pltpu.make_async_copy(wd_hbm.at[layer, expert], wd_f32.at[slot], wsem.at[1, slot]))

    def wait_rows(s):
        pltpu.make_async_copy(hn_hbm.at[pl.ds(0, rb * sub), :], bufs[s], sem.at[s]).wait()

    def start_rows(blk, s):
        for r in range(rb):
            _row_copy(hn_hbm, bufs[s], sem.at[s], rtok_ref[blk * rb + r], r, sub).start(priority=ROW_QUEUE)

    @pl.when(i == 0)
    def _():
        for cp in weight_copies(bexp_ref[0], 0):
            cp.start(priority=WEIGHT_QUEUE)
        for blk in range(GATHER_AHEAD):
            start_rows(blk, blk)

    @pl.when(first_ref[i] == 1)
    def _():
        slot = wslot_ref[i]
        for cp in weight_copies(bexp_ref[i], slot):
            cp.wait()
        nxt = next_ref[i]

        @pl.when(nxt >= 0)
        def _():
            for cp in weight_copies(nxt, 1 - slot):
                cp.start(priority=WEIGHT_QUEUE)

        wgu_bf[...] = wgu_f32[slot].astype(BF16)
        wd_bf[...] = wd_f32[slot].astype(BF16)

    for s in range(n_buf):
        @pl.when((i < n_used) & (i % n_buf == s))
        def _():
            wait_rows(s)
            start_rows(i + GATHER_AHEAD, (s + GATHER_AHEAD) % n_buf)
            los, his = _load_packed(bufs[s], rb, sub)
            x = jnp.concatenate([p.astype(BF16) for p in los + his], axis=1)
            gu = _dot(x, wgu_bf[...])
            gt, up = gu[:, :ff], gu[:, ff:]
            act = (gt * jax.nn.sigmoid(gt) * up).astype(BF16)
            _store_packed(y_ref, _dot(act, wd_bf[...]))

        @pl.when((i >= n_used) & (i < n_used + GATHER_AHEAD) & (i % n_buf == s))
        def _():
            wait_rows(s)

    @pl.when(i >= n_used)
    def _():
        y_ref[...] = jnp.zeros_like(y_ref)


def experts(block_exp, row_tok, n_used, hn, w_gate_up, w_down, layer):
    D, ff2 = w_gate_up.shape[-2:]
    ff = ff2 // 2
    sub = _packed_sublanes(D)
    P = row_tok.shape[0]
    rb = ROW_BLOCK
    n_blocks = P // rb

    blk = jnp.arange(n_blocks, dtype=I32)
    prev = jnp.concatenate([jnp.full((1,), -1, I32), block_exp[:-1]])
    first = (blk < n_used[0]) & (block_exp != prev)
    wslot = (jnp.cumsum(first.astype(I32)) - 1) % 2
    first_at = lax.cummin(jnp.where(first, blk, n_blocks), reverse=True)
    next_first = jnp.concatenate([first_at[1:], jnp.full((1,), n_blocks, I32)])
    next_exp = jnp.where(next_first < n_blocks, block_exp[jnp.minimum(next_first, n_blocks - 1)], -1)

    grid_spec = pltpu.PrefetchScalarGridSpec(
        num_scalar_prefetch=6,
        grid=(n_blocks,),
        in_specs=[pl.BlockSpec(memory_space=pl.ANY)] * 3,
        out_specs=pl.BlockSpec((rb * sub, LANES), lambda i, *_: (i, 0)),
        scratch_shapes=[pltpu.VMEM((rb * sub, LANES), I32)] * (GATHER_AHEAD + 1)
        + [pltpu.VMEM((2, D, ff2), F32), pltpu.VMEM((2, ff, D), F32),
           pltpu.VMEM((D, ff2), BF16), pltpu.VMEM((ff, D), BF16),
           pltpu.SemaphoreType.DMA((GATHER_AHEAD + 1,)), pltpu.SemaphoreType.DMA((2, 2))],
    )
    return pl.pallas_call(
        functools.partial(_experts_kernel, layer=layer, ff=ff, rb=rb, sub=sub),
        grid_spec=grid_spec,
        out_shape=jax.ShapeDtypeStruct((P * sub, LANES), I32),
        compiler_params=_params("arbitrary"),
        name="experts",
    )(block_exp, first.astype(I32), wslot.astype(I32), next_exp.astype(I32), row_tok, n_used,
      hn, w_gate_up, w_down)


def _combine_kernel(d0_ref, d1_ref, yb_hbm, h_ref, g0_ref, g1_ref, nw_ref, o_ref, b00, b01, b10, b11, sem,
                    *, final_norm, sub):
    i = pl.program_id(0)
    n = pl.num_programs(0)
    tc = o_ref.shape[0]
    bufs = ((b00, b01), (b10, b11))
    d_refs = (d0_ref, d1_ref)

    def wait_rows(s):
        for k in range(TOP_K):
            pltpu.make_async_copy(yb_hbm.at[pl.ds(0, tc * sub), :], bufs[s][k], sem.at[s]).wait()

    def start_rows(blk, s):
        for r in range(tc):
            for k in range(TOP_K):
                _row_copy(yb_hbm, bufs[s][k], sem.at[s], d_refs[k][blk * tc + r], r, sub).start(priority=k)

    @pl.when(i == 0)
    def _():
        start_rows(0, 0)

    for s in range(2):
        @pl.when(i % 2 == s)
        def _():
            wait_rows(s)
            start_rows(i + 1, 1 - s)
            lo0, hi0 = _load_packed(bufs[s][0], tc, sub)
            lo1, hi1 = _load_packed(bufs[s][1], tc, sub)
            g0, g1 = g0_ref[...], g1_ref[...]
            pieces = [h_ref[:, j * LANES:(j + 1) * LANES] + g0 * y0 + g1 * y1
                      for j, (y0, y1) in enumerate(zip(lo0 + hi0, lo1 + hi1))]
            if final_norm:
                o_ref[...] = _rms(jnp.concatenate(pieces, axis=1), nw_ref[...])
            else:
                for j, piece in enumerate(pieces):
                    o_ref[:, j * LANES:(j + 1) * LANES] = piece

            @pl.when(i == n - 1)
            def _():
                wait_rows(1 - s)


def combine(d0, d1, yb, h, g0, g1, norm_w=None):
    T, D = h.shape
    tc = _tile(T, 256)
    sub = _packed_sublanes(D)
    final_norm = norm_w is not None
    nw = (norm_w if final_norm else jnp.ones((D,), F32)).reshape(1, D)
    spare = jnp.zeros((tc,), I32)
    row = lambda i, a, b: (i, 0)
    grid_spec = pltpu.PrefetchScalarGridSpec(
        num_scalar_prefetch=2,
        grid=(T // tc,),
        in_specs=[
            pl.BlockSpec(memory_space=pl.ANY),
            pl.BlockSpec((tc, D), row),
            pl.BlockSpec((tc, LANES), row),
            pl.BlockSpec((tc, LANES), row),
            pl.BlockSpec((1, D), lambda i, a, b: (0, 0)),
        ],
        out_specs=pl.BlockSpec((tc, D), row),
        scratch_shapes=[pltpu.VMEM((tc * sub, LANES), I32)] * (2 * TOP_K) + [pltpu.SemaphoreType.DMA((2,))],
    )
    return pl.pallas_call(
        functools.partial(_combine_kernel, final_norm=final_norm, sub=sub),
        grid_spec=grid_spec,
        out_shape=jax.ShapeDtypeStruct((T, D), F32),
        compiler_params=_params("arbitrary"),
        name="combine_norm" if final_norm else "combine",
    )(jnp.concatenate([d0, spare]), jnp.concatenate([d1, spare]), yb, h, g0, g1, nw)


def _pad_cols(w, n):
    return jnp.pad(w, ((0, 0), (0, n - w.shape[1])))


def moe(h, hn, id0, id1, g0, g1, counts, w_gate_up, w_down, layer, final_norm_w=None):
    T, D = h.shape
    rb = ROW_BLOCK
    n_blocks = -(-(T * TOP_K + N_EXPERTS * (rb - 1)) // rb) + GATHER_AHEAD
    cnt = counts[0, :N_EXPERTS].astype(I32)
    padded = ((cnt + rb - 1) // rb) * rb
    pends = jnp.cumsum(padded)
    pstart_row = _pad_cols((pends - padded).astype(F32).reshape(1, N_EXPERTS), LANES)
    d0b, d1b = rank(id0, id1, pstart_row)
    d0, d1 = d0b[:, 0], d1b[:, 0]
    fill_start = jnp.concatenate([pends - padded + cnt, pends[-1:]])
    fill_end = jnp.concatenate([pends, jnp.full((1,), n_blocks * rb, I32)])
    row_tok = invert(d0, d1, jnp.stack([fill_start, fill_end], axis=1).reshape(-1), n_blocks * rb)
    block_start = jnp.arange(n_blocks, dtype=I32) * rb
    block_exp = jnp.minimum(jnp.sum(pends[None, :] <= block_start[:, None], axis=1), N_EXPERTS - 1).astype(I32)
    n_used = (pends[-1:] // rb).astype(I32)
    yb = experts(block_exp, row_tok, n_used, hn, w_gate_up, w_down, layer)
    return combine(d0, d1, yb, h, g0, g1, final_norm_w)


def _router_weights(w_group, b_group, w_router, b_router):
    w = _pad_cols(jnp.concatenate([w_router, w_group], axis=1), LANES)
    b = _pad_cols(jnp.concatenate([b_router, b_group]).reshape(1, -1), LANES)
    hi = w.astype(BF16)
    lo = (w - hi.astype(F32)).astype(BF16)
    return jnp.concatenate([hi, lo], axis=1), b


def kernel(x, mix_norm, gla_w_in, gla_w_gk2, gla_b_gk, gla_head_norm, gla_w_out, conv_w_in, conv_w,
           conv_w_out, ffn_norm, w_group, b_group, w_router, b_router, w_gate_up, w_down, final_norm):
    B, T, D = x.shape
    assert B == 1, "the recurrence state is carried across the whole row axis"
    h = x.reshape(T, D)
    kd, vd = D // 2, D
    dk, dv = kd // GLA_HEADS, vd // GLA_HEADS

    w_in = gla_w_in[0]
    n_main = 2 * kd + 2 * vd
    proj, gl = norm_proj(h, mix_norm[0], w_in.astype(BF16),
                         _pad_cols(w_in[:, n_main:], LANES).astype(BF16), n_cols=n_main)
    w_gk2 = jnp.pad(gla_w_gk2[0], ((0, LANES - GATE_RANK), (0, 0))).astype(BF16)
    o = gla(proj, gl, w_gk2, gla_b_gk[0].reshape(1, kd), gla_head_norm[0].reshape(1, dv), dk=dk, dv=dv)
    routed = out_router(o, gla_w_out[0].astype(BF16), h, ffn_norm[0],
                        *_router_weights(w_group[0], b_group[0], w_router[0], b_router[0]))
    h = moe(*routed, w_gate_up, w_down, 0)

    proj = norm_proj(h, mix_norm[1], conv_w_in[0].astype(BF16))
    routed = out_router(proj, conv_w_out[0].astype(BF16), h, ffn_norm[1],
                        *_router_weights(w_group[1], b_group[1], w_router[1], b_router[1]),
                        conv_w=conv_w[0])
    out = moe(*routed, w_gate_up, w_down, 1, final_norm_w=final_norm)
    return out.reshape(B, T, D)
```

```python
import functools

import jax
import jax.numpy as jnp
from jax import lax
from jax.experimental import pallas as pl
from jax.experimental.pallas import tpu as pltpu

F32 = jnp.float32
BF16 = jnp.bfloat16
I32 = jnp.int32

EPS = 1e-6
GLA_HEADS = 4
GATE_RANK = 16
GATE_NORMALIZER = 16.0
CONV_WIDTH = 3
N_GROUPS = 4
EXPERTS_PER_GROUP = 8
N_EXPERTS = N_GROUPS * EXPERTS_PER_GROUP
TOP_K = 2

LANES = 128
SUBLANES = 8
BF16_ROWS = 16
HIGH_HALF = -65536
VMEM_LIMIT = 56 * 1024 * 1024

GLA_CHUNK = 128
GLA_SUB = 16
GLA_BREF_ROWS = -(-(GLA_CHUNK // GLA_SUB + 1) // SUBLANES) * SUBLANES
GLA_GROUP = 128
GLA_SAFE_DECAY = 60.0
ROW_BLOCK = 256
GATHER_AHEAD = 2
ROW_QUEUE, WEIGHT_QUEUE = 0, 1


def _tile(n, pref):
    t = min(n, pref)
    while n % t:
        t -= LANES
    assert t > 0, (n, pref)
    return t


def _params(*sem):
    return pltpu.CompilerParams(dimension_semantics=sem, vmem_limit_bytes=VMEM_LIMIT)


def _dot(a, b):
    return jnp.dot(a, b, preferred_element_type=F32)


def _dot_nt(a, b):
    return lax.dot_general(a, b, (((1,), (1,)), ((), ())), preferred_element_type=F32)


def _dot_tn(a, b):
    return lax.dot_general(a, b, (((0,), (0,)), ((), ())), preferred_element_type=F32)


def _rms(x, w):
    return x * lax.rsqrt(jnp.mean(x * x, axis=-1, keepdims=True) + EPS) * w


def _packed_sublanes(d):
    assert d % (2 * LANES) == 0
    return d // (2 * LANES)


def _store_packed(ref, x):
    rows, d = x.shape
    half = d // 2
    sub = half // LANES
    lo = lax.shift_right_logical(lax.bitcast_convert_type(x[:, :half].astype(BF16).astype(F32), I32), 16)
    hi = lax.bitcast_convert_type(x[:, half:].astype(BF16).astype(F32), I32) & HIGH_HALF
    words = hi | lo
    for s in range(sub):
        ref[pl.ds(s, rows, stride=sub), :] = words[:, s * LANES:(s + 1) * LANES]


def _load_packed(ref, rows, sub):
    los, his = [], []
    for s in range(sub):
        w = ref[pl.ds(s, rows, stride=sub), :]
        los.append(lax.bitcast_convert_type(lax.shift_left(w, 16), F32))
        his.append(lax.bitcast_convert_type(w & HIGH_HALF, F32))
    return los, his


def _norm_proj_kernel(*refs, with_extra):
    if with_extra:
        x_ref, nw_ref, w_ref, we_ref, o_ref, e_ref, xn_ref = refs
    else:
        x_ref, nw_ref, w_ref, o_ref, xn_ref = refs

    @pl.when(pl.program_id(1) == 0)
    def _():
        xn_ref[...] = _rms(x_ref[...], nw_ref[...]).astype(BF16)
        if with_extra:
            e_ref[...] = _dot(xn_ref[...], we_ref[...])

    o_ref[...] = _dot(xn_ref[...], w_ref[...]).astype(o_ref.dtype)


def norm_proj(x, norm_w, w, w_extra=None, n_cols=None):
    T, D = x.shape
    N = w.shape[1] if n_cols is None else n_cols
    tm, tn = _tile(T, 1024), _tile(N, 2048)
    in_specs = [
        pl.BlockSpec((tm, D), lambda i, j: (i, 0)),
        pl.BlockSpec((1, D), lambda i, j: (0, 0)),
        pl.BlockSpec((D, tn), lambda i, j: (0, j)),
    ]
    out_shape = [jax.ShapeDtypeStruct((T, N), BF16)]
    out_specs = [pl.BlockSpec((tm, tn), lambda i, j: (i, j))]
    args = [x, norm_w.reshape(1, D), w]
    if w_extra is not None:
        in_specs.append(pl.BlockSpec((D, LANES), lambda i, j: (0, 0)))
        out_shape.append(jax.ShapeDtypeStruct((T, LANES), F32))
        out_specs.append(pl.BlockSpec((tm, LANES), lambda i, j: (i, 0)))
        args.append(w_extra)
    out = pl.pallas_call(
        functools.partial(_norm_proj_kernel, with_extra=w_extra is not None),
        grid=(T // tm, N // tn),
        in_specs=in_specs,
        out_specs=out_specs,
        out_shape=out_shape,
        scratch_shapes=[pltpu.VMEM((tm, D), BF16)],
        compiler_params=_params("arbitrary", "arbitrary"),
        name="norm_proj",
    )(*args)
    return out if w_extra is not None else out[0]


def _gla_kernel(q_ref, k_ref, v_ref, g_ref, gl_ref, wgk_ref, bgk_ref, hnorm_ref, o_ref,
                st_ref, bs_ref, qt_ref, kh_ref, bref_ref, *, dk, dv, tb):
    C, SUB, GRP, H, BR = GLA_CHUNK, GLA_SUB, GLA_GROUP, GLA_HEADS, GLA_BREF_ROWS
    NS = C // SUB
    CPG = GRP // C
    kd = H * dk
    scale = dk ** -0.5

    @pl.when(pl.program_id(0) == 0)
    def _():
        st_ref[...] = jnp.zeros_like(st_ref)

    r = lax.broadcasted_iota(I32, (GRP, GRP), 0)
    c = lax.broadcasted_iota(I32, (GRP, GRP), 1)
    same = (r // SUB) == (c // SUB)
    low = jnp.where(same & (c <= r), 1.0, 0.0).astype(BF16)
    upp = jnp.where(same & (c > r), 1.0, 0.0).astype(BF16)
    rr = lax.broadcasted_iota(I32, (CPG * BR, GRP), 0)
    cc = lax.broadcasted_iota(I32, (CPG * BR, GRP), 1)
    blk = rr % BR
    mref = jnp.where(((cc // C) == (rr // BR)) & ((cc % C) < SUB * blk) & (blk <= NS),
                     1.0, 0.0).astype(BF16)

    for gi in range(tb // GRP):
        sl = pl.ds(gi * GRP, GRP)
        z = _dot(gl_ref[sl, :].astype(BF16), wgk_ref[...]) + bgk_ref[...]
        la = (jnp.minimum(z, 0.0) - jnp.log(1.0 + jnp.exp(-jnp.abs(z)))) * (1.0 / GATE_NORMALIZER)
        hi = la.astype(BF16)
        lo = (la - hi.astype(F32)).astype(BF16)
        bs = _dot(low, hi) + _dot(low, lo)
        ru = _dot(upp, hi) + _dot(upp, lo)
        br = _dot(mref, hi) + _dot(mref, lo)
        bs_ref[sl, :] = bs
        qt_ref[sl, :] = q_ref[sl, :].astype(F32) * jnp.exp(bs) * scale
        kh_ref[sl, :] = k_ref[sl, :].astype(F32) * jnp.exp(ru)
        bref_ref[pl.ds(gi * CPG, CPG)] = br.reshape(CPG, BR, kd)

    lane = lax.broadcasted_iota(I32, (SUB, C), 1)
    subrow = lax.broadcasted_iota(I32, (SUB, dk), 0)

    def scores_pivoted(qin, kraw, bs, bn):
        neg_b = jnp.concatenate([-(bs[J * SUB:(J + 1) * SUB] + bn[J:J + 1]) for J in range(NS)], axis=0)
        s = _dot_nt(qin.astype(BF16), (kraw * jnp.exp(neg_b)).astype(BF16))
        ri = lax.broadcasted_iota(I32, (C, C), 0)
        ci = lax.broadcasted_iota(I32, (C, C), 1)
        return jnp.where(ci <= ri, s, 0.0)

    def scores_stable(qt, kh, qraw, kraw, bs, bn):
        a_rows = []
        for I in range(NS):
            blk_rows = slice(I * SUB, (I + 1) * SUB)
            q_blk, k_blk, bs_blk = qraw[blk_rows], kraw[blk_rows], bs[blk_rows]
            acc = jnp.zeros((SUB, C), F32)
            for j in range(SUB):
                e = jnp.where(subrow >= j, jnp.exp(bs_blk - bs_blk[j:j + 1]), 0.0)
                p = (q_blk * k_blk[j:j + 1]) * e
                acc = jnp.where(lane == I * SUB + j, jnp.sum(p, axis=-1, keepdims=True), acc)
            if I > 0:
                parts = []
                for J in range(NS):
                    if J < I:
                        parts.append(kh[J * SUB:(J + 1) * SUB] * jnp.exp(bn[I:I + 1] - bn[J + 1:J + 2]))
                    else:
                        parts.append(jnp.zeros((SUB, dk), F32))
                k_dec = jnp.concatenate(parts, axis=0)
                acc = acc + _dot_nt(qt[blk_rows].astype(BF16), k_dec.astype(BF16))
            a_rows.append(acc)
        return jnp.concatenate(a_rows, axis=0)

    def chunk(n, carry, *, pivoted):
        r0 = pl.multiple_of(n * C, C)
        rows = pl.ds(r0, C)
        outs = []
        for h in range(H):
            kl = slice(h * dk, (h + 1) * dk)
            vl = slice(h * dv, (h + 1) * dv)
            bn = bref_ref[n, :, kl]
            eb = jnp.exp(bn)
            st = st_ref[h]
            qt = qt_ref[rows, kl]
            kh = kh_ref[rows, kl]
            bs = bs_ref[rows, kl]
            v = v_ref[rows, vl]
            kraw = k_ref[rows, kl].astype(F32)

            qin = jnp.concatenate([qt[I * SUB:(I + 1) * SUB] * eb[I:I + 1] for I in range(NS)], axis=0)
            o = _dot_nt(qin.astype(BF16), st.astype(BF16))
            if pivoted:
                a = scores_pivoted(qin, kraw, bs, bn)
            else:
                a = scores_stable(qt, kh, q_ref[rows, kl].astype(F32) * scale, kraw, bs, bn)
            o = o + _dot(a.astype(BF16), v)

            k_st = jnp.concatenate(
                [kh[J * SUB:(J + 1) * SUB] * jnp.exp(bn[NS:NS + 1] - bn[J + 1:J + 2]) for J in range(NS)],
                axis=0)
            st_ref[h] = st * eb[NS:NS + 1] + _dot_tn(v, k_st.astype(BF16))
            outs.append(_rms(o, hnorm_ref[...]))

        g = g_ref[rows, :].astype(F32)
        o_ref[rows, :] = (jnp.concatenate(outs, axis=1) * (g * jax.nn.sigmoid(g))).astype(o_ref.dtype)
        return carry

    span = jnp.max(-bref_ref[...])

    @pl.when(span < GLA_SAFE_DECAY)
    def _():
        lax.fori_loop(0, tb // C, functools.partial(chunk, pivoted=True), 0, unroll=2)

    @pl.when(jnp.logical_not(span < GLA_SAFE_DECAY))
    def _():
        lax.fori_loop(0, tb // C, functools.partial(chunk, pivoted=False), 0)


def gla(proj, gl, w_gk2, b_gk, head_norm, *, dk, dv):
    T = proj.shape[0]
    H = GLA_HEADS
    tb = _tile(T, 512)
    assert tb % GLA_GROUP == 0 and dk % LANES == 0 and dv % LANES == 0
    kd, vd = H * dk, H * dv
    assert (2 * kd) % vd == 0
    v_off = (2 * kd) // vd
    fixed = lambda t: (0, 0)
    return pl.pallas_call(
        functools.partial(_gla_kernel, dk=dk, dv=dv, tb=tb),
        grid=(T // tb,),
        in_specs=[
            pl.BlockSpec((tb, kd), lambda t: (t, 0)),
            pl.BlockSpec((tb, kd), lambda t: (t, 1)),
            pl.BlockSpec((tb, vd), lambda t: (t, v_off)),
            pl.BlockSpec((tb, vd), lambda t: (t, v_off + 1)),
            pl.BlockSpec((tb, LANES), lambda t: (t, 0)),
            pl.BlockSpec((LANES, kd), fixed),
            pl.BlockSpec((1, kd), fixed),
            pl.BlockSpec((1, dv), fixed),
        ],
        out_specs=pl.BlockSpec((tb, vd), lambda t: (t, 0)),
        out_shape=jax.ShapeDtypeStruct((T, vd), BF16),
        scratch_shapes=[
            pltpu.VMEM((H, dv, dk), F32),
            pltpu.VMEM((tb, kd), F32),
            pltpu.VMEM((tb, kd), F32),
            pltpu.VMEM((tb, kd), F32),
            pltpu.VMEM((tb // GLA_CHUNK, GLA_BREF_ROWS, kd), F32),
        ],
        compiler_params=_params("arbitrary"),
        name="gla",
    )(proj, proj, proj, proj, gl, w_gk2, b_gk, head_norm)


def _route(hn, rw_ref, rb_ref):
    tm = hn.shape[0]
    hi = hn.astype(BF16)
    lo = (hn - hi.astype(F32)).astype(BF16)
    both = _dot(hi, rw_ref[...])
    logits = both[:, :LANES] + both[:, LANES:] + _dot(lo, rw_ref[:, :LANES]) + rb_ref[...]
    lane = lax.broadcasted_iota(I32, (tm, LANES), 1)
    lane_f = lane.astype(F32)
    neg = -jnp.inf
    far = float(LANES)

    def first_max(vals):
        m = jnp.max(vals, axis=-1, keepdims=True)
        idx = jnp.min(jnp.where(vals == m, lane_f, far), axis=-1, keepdims=True)
        return m, idx

    lg = jnp.where((lane >= N_EXPERTS) & (lane < N_EXPERTS + N_GROUPS), logits, neg)
    mg, gidx = first_max(lg)
    top_gp = 1.0 / jnp.sum(jnp.exp(lg - mg), axis=-1, keepdims=True)
    grp = gidx.astype(I32) - N_EXPERTS
    le = jnp.where((lane < N_EXPERTS) & ((lane // EXPERTS_PER_GROUP) == grp), logits, neg)
    m1, i1 = first_max(le)
    m2, i2 = first_max(jnp.where(lane_f == i1, neg, le))
    e2 = jnp.exp(m2 - m1)
    w1 = 1.0 / (1.0 + e2)
    return i1.astype(I32), i2.astype(I32), top_gp * w1, top_gp * (e2 * w1)


def _out_router_kernel(*refs, conv):
    if conv:
        (pb_ref, pc_ref, ph_ref, hc_ref, hh_ref, cw_ref, w_ref, hin_ref, fnw_ref, rw_ref,
         rb_ref, h_ref, hn_ref, id0_ref, id1_ref, g0_ref, g1_ref, cnt_ref) = refs
    else:
        (y_ref, w_ref, hin_ref, fnw_ref, rw_ref,
         rb_ref, h_ref, hn_ref, id0_ref, id1_ref, g0_ref, g1_ref, cnt_ref) = refs
    i = pl.program_id(0)

    if conv:
        u = pc_ref[...].astype(F32) * ph_ref[...].astype(F32)
        prev = hc_ref[...].astype(F32) * hh_ref[...].astype(F32)
        prev = jnp.where(i > 0, prev, 0.0)
        last, last2 = prev[BF16_ROWS - 1:BF16_ROWS], prev[BF16_ROWS - 2:BF16_ROWS - 1]
        r1, r2 = pltpu.roll(u, 1, 0), pltpu.roll(u, 2, 0)
        row = lax.broadcasted_iota(I32, (SUBLANES, u.shape[1]), 0)
        top1 = jnp.where(row == 0, last, r1[:SUBLANES])
        top2 = jnp.where(row == 0, last2, jnp.where(row == 1, last, r2[:SUBLANES]))
        u1 = jnp.concatenate([top1, r1[SUBLANES:]], axis=0)
        u2 = jnp.concatenate([top2, r2[SUBLANES:]], axis=0)
        cw = cw_ref[...]
        y = (pb_ref[...].astype(F32) * (cw[0:1] * u2 + cw[1:2] * u1 + cw[2:3] * u)).astype(BF16)
    else:
        y = y_ref[...]

    h = hin_ref[...] + _dot(y, w_ref[...])
    h_ref[...] = h
    hn = _rms(h, fnw_ref[...])
    _store_packed(hn_ref, hn)
    i1, i2, g1, g2 = _route(hn, rw_ref, rb_ref)
    shape = id0_ref.shape
    id0_ref[...] = jnp.broadcast_to(i1, shape)
    id1_ref[...] = jnp.broadcast_to(i2, shape)
    g0_ref[...] = jnp.broadcast_to(g1, shape)
    g1_ref[...] = jnp.broadcast_to(g2, shape)

    @pl.when(i == 0)
    def _():
        cnt_ref[...] = jnp.zeros_like(cnt_ref)

    lane = lax.broadcasted_iota(I32, shape, 1)
    hit = jnp.where((lane == i1) | (lane == i2), 1.0, 0.0)
    cnt_ref[...] += jnp.sum(hit, axis=0, keepdims=True)


def out_router(y, w_out, h_in, ffn_norm_w, r_w, r_b, conv_w=None):
    T, D = h_in.shape
    K = w_out.shape[0]
    tm = _tile(T, 256)
    conv = conv_w is not None
    row = lambda i: (i, 0)
    fixed = lambda i: (0, 0)
    if conv:
        hb = tm // BF16_ROWS
        halo = lambda col: (lambda i: (jnp.maximum(i * hb - 1, 0), col))
        in_specs = [
            pl.BlockSpec((tm, K), lambda i: (i, 0)),
            pl.BlockSpec((tm, K), lambda i: (i, 1)),
            pl.BlockSpec((tm, K), lambda i: (i, 2)),
            pl.BlockSpec((BF16_ROWS, K), halo(1)),
            pl.BlockSpec((BF16_ROWS, K), halo(2)),
            pl.BlockSpec((SUBLANES, K), fixed),
        ]
        cw = jnp.zeros((SUBLANES, K), F32).at[:CONV_WIDTH].set(conv_w)
        args = [y, y, y, y, y, cw]
    else:
        in_specs = [pl.BlockSpec((tm, K), row)]
        args = [y]
    in_specs += [
        pl.BlockSpec((K, D), fixed),
        pl.BlockSpec((tm, D), row),
        pl.BlockSpec((1, D), fixed),
        pl.BlockSpec((D, 2 * LANES), fixed),
        pl.BlockSpec((1, LANES), fixed),
    ]
    args += [w_out, h_in, ffn_norm_w.reshape(1, D), r_w, r_b]
    wide = lambda dt: jax.ShapeDtypeStruct((T, LANES), dt)
    sub = _packed_sublanes(D)
    return pl.pallas_call(
        functools.partial(_out_router_kernel, conv=conv),
        grid=(T // tm,),
        in_specs=in_specs,
        out_specs=[pl.BlockSpec((tm, D), row), pl.BlockSpec((tm * sub, LANES), row)]
        + [pl.BlockSpec((tm, LANES), row)] * 4 + [pl.BlockSpec((1, LANES), fixed)],
        out_shape=[jax.ShapeDtypeStruct((T, D), F32), jax.ShapeDtypeStruct((T * sub, LANES), I32),
                   wide(I32), wide(I32), wide(F32), wide(F32), jax.ShapeDtypeStruct((1, LANES), F32)],
        compiler_params=_params("arbitrary"),
        name="out_router_conv" if conv else "out_router",
    )(*args)


def _rank_kernel(id0_ref, id1_ref, pst_ref, d0_ref, d1_ref, carry_ref):
    @pl.when(pl.program_id(0) == 0)
    def _():
        carry_ref[...] = jnp.zeros_like(carry_ref)

    tb = id0_ref.shape[0]
    lane = lax.broadcasted_iota(I32, (tb, LANES), 1)
    oh0 = lane == id0_ref[...]
    oh1 = lane == id1_ref[...]
    hit = jnp.where(oh0 | oh1, 1.0, 0.0)
    r = lax.broadcasted_iota(I32, (tb, tb), 0)
    c = lax.broadcasted_iota(I32, (tb, tb), 1)
    before = jnp.where(c < r, 1.0, 0.0).astype(BF16)
    base = _dot(before, hit.astype(BF16)) + carry_ref[...] + pst_ref[...]
    d0 = jnp.sum(jnp.where(oh0, base, 0.0), axis=-1, keepdims=True)
    d1 = jnp.sum(jnp.where(oh1, base, 0.0), axis=-1, keepdims=True)
    d0_ref[...] = jnp.broadcast_to(d0.astype(I32), (tb, LANES))
    d1_ref[...] = jnp.broadcast_to(d1.astype(I32), (tb, LANES))
    carry_ref[...] += jnp.sum(hit, axis=0, keepdims=True)


def rank(id0, id1, pstart_row):
    T = id0.shape[0]
    tb = _tile(T, 1024)
    row = lambda i: (i, 0)
    return pl.pallas_call(
        _rank_kernel,
        grid=(T // tb,),
        in_specs=[pl.BlockSpec((tb, LANES), row), pl.BlockSpec((tb, LANES), row),
                  pl.BlockSpec((1, LANES), lambda i: (0, 0))],
        out_specs=[pl.BlockSpec((tb, LANES), row)] * 2,
        out_shape=[jax.ShapeDtypeStruct((T, LANES), I32)] * 2,
        scratch_shapes=[pltpu.VMEM((1, LANES), F32)],
        compiler_params=_params("arbitrary"),
        name="rank",
    )(id0, id1, pstart_row)


def _invert_kernel(d0_ref, d1_ref, fill_ref, rt_ref):
    width = SUBLANES

    def clear(p, carry):
        for k in range(width):
            rt_ref[p * width + k] = 0
        return carry

    for e in range(fill_ref.shape[0] // 2):
        lax.fori_loop(fill_ref[2 * e] // width, fill_ref[2 * e + 1] // width, clear, 0)

    def place(t, carry):
        rt_ref[d0_ref[t]] = t
        rt_ref[d1_ref[t]] = t
        return carry

    lax.fori_loop(0, d0_ref.shape[0], place, 0, unroll=16)


def invert(d0, d1, fill_ranges, n_rows):
    smem = pl.BlockSpec(memory_space=pltpu.SMEM)
    return pl.pallas_call(
        _invert_kernel,
        in_specs=[smem, smem, smem],
        out_specs=smem,
        out_shape=jax.ShapeDtypeStruct((n_rows,), I32),
        name="invert",
    )(d0, d1, fill_ranges)


def _row_copy(src_hbm, dst_ref, sem, src_row, dst_row, sub):
    src = src_hbm.at[pl.ds(pl.multiple_of(src_row * sub, sub), sub), :]
    dst = dst_ref.at[pl.ds(pl.multiple_of(dst_row * sub, sub), sub), :]
    return pltpu.make_async_copy(src, dst, sem)


def _experts_kernel(bexp_ref, first_ref, wslot_ref, next_ref, rtok_ref, nused_ref, hn_hbm, wgu_hbm, wd_hbm, y_ref,
                    xbuf0, xbuf1, xbuf2, wgu_f32, wd_f32, wgu_bf, wd_bf, sem, wsem, *, layer, ff, rb, sub):
    i = pl.program_id(0)
    n_used = nused_ref[0]
    bufs = (xbuf0, xbuf1, xbuf2)
    n_buf = len(bufs)
    assert n_buf == GATHER_AHEAD + 1

    def weight_copies(expert, slot):
        return (pltpu.make_async_copy(wgu_hbm.at[layer, expert], wgu_f32.at[slot], wsem.at[0, slot]),
                pltpu.make_async_copy(wd_hbm.at[layer, expert], wd_f32.at[slot], wsem.at[1, slot]))

    def wait_rows(s):
        pltpu.make_async_copy(hn_hbm.at[pl.ds(0, rb * sub), :], bufs[s], sem.at[s]).wait()

    def start_rows(blk, s):
        for r in range(rb):
            _row_copy(hn_hbm, bufs[s], sem.at[s], rtok_ref[blk * rb + r], r, sub).start(priority=ROW_QUEUE)

    @pl.when(i == 0)
    def _():
        for cp in weight_copies(bexp_ref[0], 0):
            cp.start(priority=WEIGHT_QUEUE)
        for blk in range(GATHER_AHEAD):
            start_rows(blk, blk)

    @pl.when(first_ref[i] == 1)
    def _():
        slot = wslot_ref[i]
        for cp in weight_copies(bexp_ref[i], slot):
            cp.wait()
        nxt = next_ref[i]

        @pl.when(nxt >= 0)
        def _():
            for cp in weight_copies(nxt, 1 - slot):
                cp.start(priority=WEIGHT_QUEUE)

        wgu_bf[...] = wgu_f32[slot].astype(BF16)
        wd_bf[...] = wd_f32[slot].astype(BF16)

    for s in range(n_buf):
        @pl.when((i < n_used) & (i % n_buf == s))
        def _():
            wait_rows(s)
            start_rows(i + GATHER_AHEAD, (s + GATHER_AHEAD) % n_buf)
            los, his = _load_packed(bufs[s], rb, sub)
            x = jnp.concatenate([p.astype(BF16) for p in los + his], axis=1)
            gu = _dot(x, wgu_bf[...])
            gt, up = gu[:, :ff], gu[:, ff:]
            act = (gt * jax.nn.sigmoid(gt) * up).astype(BF16)
            _store_packed(y_ref, _dot(act, wd_bf[...]))

        @pl.when((i >= n_used) & (i < n_used + GATHER_AHEAD) & (i % n_buf == s))
        def _():
            wait_rows(s)

    @pl.when(i >= n_used)
    def _():
        y_ref[...] = jnp.zeros_like(y_ref)


def experts(block_exp, row_tok, n_used, hn, w_gate_up, w_down, layer):
    D, ff2 = w_gate_up.shape[-2:]
    ff = ff2 // 2
    sub = _packed_sublanes(D)
    P = row_tok.shape[0]
    rb = ROW_BLOCK
    n_blocks = P // rb

    blk = jnp.arange(n_blocks, dtype=I32)
    prev = jnp.concatenate([jnp.full((1,), -1, I32), block_exp[:-1]])
    first = (blk < n_used[0]) & (block_exp != prev)
    wslot = (jnp.cumsum(first.astype(I32)) - 1) % 2
    first_at = lax.cummin(jnp.where(first, blk, n_blocks), reverse=True)
    next_first = jnp.concatenate([first_at[1:], jnp.full((1,), n_blocks, I32)])
    next_exp = jnp.where(next_first < n_blocks, block_exp[jnp.minimum(next_first, n_blocks - 1)], -1)

    grid_spec = pltpu.PrefetchScalarGridSpec(
        num_scalar_prefetch=6,
        grid=(n_blocks,),
        in_specs=[pl.BlockSpec(memory_space=pl.ANY)] * 3,
        out_specs=pl.BlockSpec((rb * sub, LANES), lambda i, *_: (i, 0)),
        scratch_shapes=[pltpu.VMEM((rb * sub, LANES), I32)] * (GATHER_AHEAD + 1)
        + [pltpu.VMEM((2, D, ff2), F32), pltpu.VMEM((2, ff, D), F32),
           pltpu.VMEM((D, ff2), BF16), pltpu.VMEM((ff, D), BF16),
           pltpu.SemaphoreType.DMA((GATHER_AHEAD + 1,)), pltpu.SemaphoreType.DMA((2, 2))],
    )
    return pl.pallas_call(
        functools.partial(_experts_kernel, layer=layer, ff=ff, rb=rb, sub=sub),
        grid_spec=grid_spec,
        out_shape=jax.ShapeDtypeStruct((P * sub, LANES), I32),
        compiler_params=_params("arbitrary"),
        name="experts",
    )(block_exp, first.astype(I32), wslot.astype(I32), next_exp.astype(I32), row_tok, n_used,
      hn, w_gate_up, w_down)


def _combine_kernel(d0_ref, d1_ref, yb_hbm, h_ref, g0_ref, g1_ref, nw_ref, o_ref, b00, b01, b10, b11, sem,
                    *, final_norm, sub):
    i = pl.program_id(0)
    n = pl.num_programs(0)
    tc = o_ref.shape[0]
    bufs = ((b00, b01), (b10, b11))
    d_refs = (d0_ref, d1_ref)

    def wait_rows(s):
        for k in range(TOP_K):
            pltpu.make_async_copy(yb_hbm.at[pl.ds(0, tc * sub), :], bufs[s][k], sem.at[s]).wait()

    def start_rows(blk, s):
        for r in range(tc):
            for k in range(TOP_K):
                _row_copy(yb_hbm, bufs[s][k], sem.at[s], d_refs[k][blk * tc + r], r, sub).start(priority=k)

    @pl.when(i == 0)
    def _():
        start_rows(0, 0)

    for s in range(2):
        @pl.when(i % 2 == s)
        def _():
            wait_rows(s)
            start_rows(i + 1, 1 - s)
            lo0, hi0 = _load_packed(bufs[s][0], tc, sub)
            lo1, hi1 = _load_packed(bufs[s][1], tc, sub)
            g0, g1 = g0_ref[...], g1_ref[...]
            pieces = [h_ref[:, j * LANES:(j + 1) * LANES] + g0 * y0 + g1 * y1
                      for j, (y0, y1) in enumerate(zip(lo0 + hi0, lo1 + hi1))]
            if final_norm:
                o_ref[...] = _rms(jnp.concatenate(pieces, axis=1), nw_ref[...])
            else:
                for j, piece in enumerate(pieces):
                    o_ref[:, j * LANES:(j + 1) * LANES] = piece

            @pl.when(i == n - 1)
            def _():
                wait_rows(1 - s)


def combine(d0, d1, yb, h, g0, g1, norm_w=None):
    T, D = h.shape
    tc = _tile(T, 512)
    sub = _packed_sublanes(D)
    final_norm = norm_w is not None
    nw = (norm_w if final_norm else jnp.ones((D,), F32)).reshape(1, D)
    spare = jnp.zeros((tc,), I32)
    row = lambda i, a, b: (i, 0)
    grid_spec = pltpu.PrefetchScalarGridSpec(
        num_scalar_prefetch=2,
        grid=(T // tc,),
        in_specs=[
            pl.BlockSpec(memory_space=pl.ANY),
            pl.BlockSpec((tc, D), row),
            pl.BlockSpec((tc, LANES), row),
            pl.BlockSpec((tc, LANES), row),
            pl.BlockSpec((1, D), lambda i, a, b: (0, 0)),
        ],
        out_specs=pl.BlockSpec((tc, D), row),
        scratch_shapes=[pltpu.VMEM((tc * sub, LANES), I32)] * (2 * TOP_K) + [pltpu.SemaphoreType.DMA((2,))],
    )
    return pl.pallas_call(
        functools.partial(_combine_kernel, final_norm=final_norm, sub=sub),
        grid_spec=grid_spec,
        out_shape=jax.ShapeDtypeStruct((T, D), F32),
        compiler_params=_params("arbitrary"),
        name="combine_norm" if final_norm else "combine",
    )(jnp.concatenate([d0, spare]), jnp.concatenate([d1, spare]), yb, h, g0, g1, nw)


def _pad_cols(w, n):
    return jnp.pad(w, ((0, 0), (0, n - w.shape[1])))


def moe(h, hn, id0, id1, g0, g1, counts, w_gate_up, w_down, layer, final_norm_w=None):
    T, D = h.shape
    rb = ROW_BLOCK
    n_blocks = -(-(T * TOP_K + N_EXPERTS * (rb - 1)) // rb) + GATHER_AHEAD
    cnt = counts[0, :N_EXPERTS].astype(I32)
    padded = ((cnt + rb - 1) // rb) * rb
    pends = jnp.cumsum(padded)
    pstart_row = _pad_cols((pends - padded).astype(F32).reshape(1, N_EXPERTS), LANES)
    d0b, d1b = rank(id0, id1, pstart_row)
    d0, d1 = d0b[:, 0], d1b[:, 0]
    fill_start = jnp.concatenate([pends - padded + cnt, pends[-1:]])
    fill_end = jnp.concatenate([pends, jnp.full((1,), n_blocks * rb, I32)])
    row_tok = invert(d0, d1, jnp.stack([fill_start, fill_end], axis=1).reshape(-1), n_blocks * rb)
    block_start = jnp.arange(n_blocks, dtype=I32) * rb
    block_exp = jnp.minimum(jnp.sum(pends[None, :] <= block_start[:, None], axis=1), N_EXPERTS - 1).astype(I32)
    n_used = (pends[-1:] // rb).astype(I32)
    yb = experts(block_exp, row_tok, n_used, hn, w_gate_up, w_down, layer)
    return combine(d0, d1, yb, h, g0, g1, final_norm_w)


def _router_weights(w_group, b_group, w_router, b_router):
    w = _pad_cols(jnp.concatenate([w_router, w_group], axis=1), LANES)
    b = _pad_cols(jnp.concatenate([b_router, b_group]).reshape(1, -1), LANES)
    hi = w.astype(BF16)
    lo = (w - hi.astype(F32)).astype(BF16)
    return jnp.concatenate([hi, lo], axis=1), b


def kernel(x, mix_norm, gla_w_in, gla_w_gk2, gla_b_gk, gla_head_norm, gla_w_out, conv_w_in, conv_w,
           conv_w_out, ffn_norm, w_group, b_group, w_router, b_router, w_gate_up, w_down, final_norm):
    B, T, D = x.shape
    assert B == 1, "the recurrence state is carried across the whole row axis"
    h = x.reshape(T, D)
    kd, vd = D // 2, D
    dk, dv = kd // GLA_HEADS, vd // GLA_HEADS

    w_in = gla_w_in[0]
    n_main = 2 * kd + 2 * vd
    proj, gl = norm_proj(h, mix_norm[0], w_in.astype(BF16),
                         _pad_cols(w_in[:, n_main:], LANES).astype(BF16), n_cols=n_main)
    w_gk2 = jnp.pad(gla_w_gk2[0], ((0, LANES - GATE_RANK), (0, 0))).astype(BF16)
    o = gla(proj, gl, w_gk2, gla_b_gk[0].reshape(1, kd), gla_head_norm[0].reshape(1, dv), dk=dk, dv=dv)
    routed = out_router(o, gla_w_out[0].astype(BF16), h, ffn_norm[0],
                        *_router_weights(w_group[0], b_group[0], w_router[0], b_router[0]))
    h = moe(*routed, w_gate_up, w_down, 0)

    proj = norm_proj(h, mix_norm[1], conv_w_in[0].astype(BF16))
    routed = out_router(proj, conv_w_out[0].astype(BF16), h, ffn_norm[1],
                        *_router_weights(w_group[1], b_group[1], w_router[1], b_router[1]),
                        conv_w=conv_w[0])
    out = moe(*routed, w_gate_up, w_down, 1, final_norm_w=final_norm)
    return out.reshape(B, T, D)
```

```python
import functools

import jax
import jax.numpy as jnp
from jax import lax
from jax.experimental import pallas as pl
from jax.experimental.pallas import tpu as pltpu

F32 = jnp.float32
BF16 = jnp.bfloat16
I32 = jnp.int32

EPS = 1e-6
GLA_HEADS = 4
GATE_RANK = 16
GATE_NORMALIZER = 16.0
CONV_WIDTH = 3
N_GROUPS = 4
EXPERTS_PER_GROUP = 8
N_EXPERTS = N_GROUPS * EXPERTS_PER_GROUP
TOP_K = 2

LANES = 128
SUBLANES = 8
BF16_ROWS = 16
HIGH_HALF = -65536
VMEM_LIMIT = 56 * 1024 * 1024

GLA_CHUNK = 128
GLA_SUB = 16
GLA_BREF_ROWS = -(-(GLA_CHUNK // GLA_SUB + 1) // SUBLANES) * SUBLANES
GLA_GROUP = 128
GLA_SAFE_DECAY = 60.0
ROW_BLOCK = 256
GATHER_AHEAD = 2
ROW_QUEUE, WEIGHT_QUEUE = 0, 1


def _tile(n, pref):
    t = min(n, pref)
    while n % t:
        t -= LANES
    assert t > 0, (n, pref)
    return t


def _params(*sem):
    return pltpu.CompilerParams(dimension_semantics=sem, vmem_limit_bytes=VMEM_LIMIT)


def _dot(a, b):
    return jnp.dot(a, b, preferred_element_type=F32)


def _dot_nt(a, b):
    return lax.dot_general(a, b, (((1,), (1,)), ((), ())), preferred_element_type=F32)


def _dot_tn(a, b):
    return lax.dot_general(a, b, (((0,), (0,)), ((), ())), preferred_element_type=F32)


def _rms(x, w):
    return x * lax.rsqrt(jnp.mean(x * x, axis=-1, keepdims=True) + EPS) * w


def _packed_sublanes(d):
    assert d % (2 * LANES) == 0
    return d // (2 * LANES)


def _store_packed(ref, x):
    rows, d = x.shape
    half = d // 2
    sub = half // LANES
    lo = lax.shift_right_logical(lax.bitcast_convert_type(x[:, :half].astype(BF16).astype(F32), I32), 16)
    hi = lax.bitcast_convert_type(x[:, half:].astype(BF16).astype(F32), I32) & HIGH_HALF
    words = hi | lo
    for s in range(sub):
        ref[pl.ds(s, rows, stride=sub), :] = words[:, s * LANES:(s + 1) * LANES]


def _load_packed(ref, rows, sub):
    los, his = [], []
    for s in range(sub):
        w = ref[pl.ds(s, rows, stride=sub), :]
        los.append(lax.bitcast_convert_type(lax.shift_left(w, 16), F32))
        his.append(lax.bitcast_convert_type(w & HIGH_HALF, F32))
    return los, his


def _norm_proj_kernel(*refs, with_extra):
    if with_extra:
        x_ref, nw_ref, w_ref, we_ref, o_ref, e_ref, xn_ref = refs
    else:
        x_ref, nw_ref, w_ref, o_ref, xn_ref = refs

    @pl.when(pl.program_id(1) == 0)
    def _():
        xn_ref[...] = _rms(x_ref[...], nw_ref[...]).astype(BF16)
        if with_extra:
            e_ref[...] = _dot(xn_ref[...], we_ref[...])

    o_ref[...] = _dot(xn_ref[...], w_ref[...]).astype(o_ref.dtype)


def norm_proj(x, norm_w, w, w_extra=None, n_cols=None):
    T, D = x.shape
    N = w.shape[1] if n_cols is None else n_cols
    tm, tn = _tile(T, 1024), _tile(N, 2048)
    in_specs = [
        pl.BlockSpec((tm, D), lambda i, j: (i, 0)),
        pl.BlockSpec((1, D), lambda i, j: (0, 0)),
        pl.BlockSpec((D, tn), lambda i, j: (0, j)),
    ]
    out_shape = [jax.ShapeDtypeStruct((T, N), BF16)]
    out_specs = [pl.BlockSpec((tm, tn), lambda i, j: (i, j))]
    args = [x, norm_w.reshape(1, D), w]
    if w_extra is not None:
        in_specs.append(pl.BlockSpec((D, LANES), lambda i, j: (0, 0)))
        out_shape.append(jax.ShapeDtypeStruct((T, LANES), F32))
        out_specs.append(pl.BlockSpec((tm, LANES), lambda i, j: (i, 0)))
        args.append(w_extra)
    out = pl.pallas_call(
        functools.partial(_norm_proj_kernel, with_extra=w_extra is not None),
        grid=(T // tm, N // tn),
        in_specs=in_specs,
        out_specs=out_specs,
        out_shape=out_shape,
        scratch_shapes=[pltpu.VMEM((tm, D), BF16)],
        compiler_params=_params("arbitrary", "arbitrary"),
        name="norm_proj",
    )(*args)
    return out if w_extra is not None else out[0]


def _gla_kernel(q_ref, k_ref, v_ref, g_ref, gl_ref, wgk_ref, bgk_ref, hnorm_ref, o_ref,
                st_ref, bs_ref, qt_ref, kh_ref, bref_ref, *, dk, dv, tb):
    C, SUB, GRP, H, BR = GLA_CHUNK, GLA_SUB, GLA_GROUP, GLA_HEADS, GLA_BREF_ROWS
    NS = C // SUB
    CPG = GRP // C
    kd = H * dk
    scale = dk ** -0.5

    @pl.when(pl.program_id(0) == 0)
    def _():
        st_ref[...] = jnp.zeros_like(st_ref)

    r = lax.broadcasted_iota(I32, (GRP, GRP), 0)
    c = lax.broadcasted_iota(I32, (GRP, GRP), 1)
    same = (r // SUB) == (c // SUB)
    low = jnp.where(same & (c <= r), 1.0, 0.0).astype(BF16)
    upp = jnp.where(same & (c > r), 1.0, 0.0).astype(BF16)
    rr = lax.broadcasted_iota(I32, (CPG * BR, GRP), 0)
    cc = lax.broadcasted_iota(I32, (CPG * BR, GRP), 1)
    blk = rr % BR
    mref = jnp.where(((cc // C) == (rr // BR)) & ((cc % C) < SUB * blk) & (blk <= NS),
                     1.0, 0.0).astype(BF16)

    for gi in range(tb // GRP):
        sl = pl.ds(gi * GRP, GRP)
        z = _dot(gl_ref[sl, :].astype(BF16), wgk_ref[...]) + bgk_ref[...]
        la = (jnp.minimum(z, 0.0) - jnp.log(1.0 + jnp.exp(-jnp.abs(z)))) * (1.0 / GATE_NORMALIZER)
        hi = la.astype(BF16)
        lo = (la - hi.astype(F32)).astype(BF16)
        bs = _dot(low, hi) + _dot(low, lo)
        ru = _dot(upp, hi) + _dot(upp, lo)
        br = _dot(mref, hi) + _dot(mref, lo)
        bs_ref[sl, :] = bs
        qt_ref[sl, :] = q_ref[sl, :].astype(F32) * jnp.exp(bs) * scale
        kh_ref[sl, :] = k_ref[sl, :].astype(F32) * jnp.exp(ru)
        bref_ref[pl.ds(gi * CPG, CPG)] = br.reshape(CPG, BR, kd)

    lane = lax.broadcasted_iota(I32, (SUB, C), 1)
    subrow = lax.broadcasted_iota(I32, (SUB, dk), 0)

    def scores_pivoted(qin, kraw, bs, bn):
        neg_b = jnp.concatenate([-(bs[J * SUB:(J + 1) * SUB] + bn[J:J + 1]) for J in range(NS)], axis=0)
        s = _dot_nt(qin.astype(BF16), (kraw * jnp.exp(neg_b)).astype(BF16))
        ri = lax.broadcasted_iota(I32, (C, C), 0)
        ci = lax.broadcasted_iota(I32, (C, C), 1)
        return jnp.where(ci <= ri, s, 0.0)

    def scores_stable(qt, kh, qraw, kraw, bs, bn):
        a_rows = []
        for I in range(NS):
            blk_rows = slice(I * SUB, (I + 1) * SUB)
            q_blk, k_blk, bs_blk = qraw[blk_rows], kraw[blk_rows], bs[blk_rows]
            acc = jnp.zeros((SUB, C), F32)
            for j in range(SUB):
                e = jnp.where(subrow >= j, jnp.exp(bs_blk - bs_blk[j:j + 1]), 0.0)
                p = (q_blk * k_blk[j:j + 1]) * e
                acc = jnp.where(lane == I * SUB + j, jnp.sum(p, axis=-1, keepdims=True), acc)
            if I > 0:
                parts = []
                for J in range(NS):
                    if J < I:
                        parts.append(kh[J * SUB:(J + 1) * SUB] * jnp.exp(bn[I:I + 1] - bn[J + 1:J + 2]))
                    else:
                        parts.append(jnp.zeros((SUB, dk), F32))
                k_dec = jnp.concatenate(parts, axis=0)
                acc = acc + _dot_nt(qt[blk_rows].astype(BF16), k_dec.astype(BF16))
            a_rows.append(acc)
        return jnp.concatenate(a_rows, axis=0)

    def chunk(n, carry, *, pivoted):
        r0 = pl.multiple_of(n * C, C)
        rows = pl.ds(r0, C)
        outs = []
        for h in range(H):
            kl = slice(h * dk, (h + 1) * dk)
            vl = slice(h * dv, (h + 1) * dv)
            bn = bref_ref[n, :, kl]
            eb = jnp.exp(bn)
            st = st_ref[h]
            qt = qt_ref[rows, kl]
            kh = kh_ref[rows, kl]
            bs = bs_ref[rows, kl]
            v = v_ref[rows, vl]
            kraw = k_ref[rows, kl].astype(F32)

            qin = jnp.concatenate([qt[I * SUB:(I + 1) * SUB] * eb[I:I + 1] for I in range(NS)], axis=0)
            o = _dot_nt(qin.astype(BF16), st.astype(BF16))
            if pivoted:
                a = scores_pivoted(qin, kraw, bs, bn)
            else:
                a = scores_stable(qt, kh, q_ref[rows, kl].astype(F32) * scale, kraw, bs, bn)
            o = o + _dot(a.astype(BF16), v)

            k_st = jnp.concatenate(
                [kh[J * SUB:(J + 1) * SUB] * jnp.exp(bn[NS:NS + 1] - bn[J + 1:J + 2]) for J in range(NS)],
                axis=0)
            st_ref[h] = st * eb[NS:NS + 1] + _dot_tn(v, k_st.astype(BF16))
            outs.append(_rms(o, hnorm_ref[...]))

        g = g_ref[rows, :].astype(F32)
        o_ref[rows, :] = (jnp.concatenate(outs, axis=1) * (g * jax.nn.sigmoid(g))).astype(o_ref.dtype)
        return carry

    span = jnp.max(-bref_ref[...])

    @pl.when(span < GLA_SAFE_DECAY)
    def _():
        lax.fori_loop(0, tb // C, functools.partial(chunk, pivoted=True), 0, unroll=2)

    @pl.when(jnp.logical_not(span < GLA_SAFE_DECAY))
    def _():
        lax.fori_loop(0, tb // C, functools.partial(chunk, pivoted=False), 0)


def gla(proj, gl, w_gk2, b_gk, head_norm, *, dk, dv):
    T = proj.shape[0]
    H = GLA_HEADS
    tb = _tile(T, 512)
    assert tb % GLA_GROUP == 0 and dk % LANES == 0 and dv % LANES == 0
    kd, vd = H * dk, H * dv
    assert (2 * kd) % vd == 0
    v_off = (2 * kd) // vd
    fixed = lambda t: (0, 0)
    return pl.pallas_call(
        functools.partial(_gla_kernel, dk=dk, dv=dv, tb=tb),
        grid=(T // tb,),
        in_specs=[
            pl.BlockSpec((tb, kd), lambda t: (t, 0)),
            pl.BlockSpec((tb, kd), lambda t: (t, 1)),
            pl.BlockSpec((tb, vd), lambda t: (t, v_off)),
            pl.BlockSpec((tb, vd), lambda t: (t, v_off + 1)),
            pl.BlockSpec((tb, LANES), lambda t: (t, 0)),
            pl.BlockSpec((LANES, kd), fixed),
            pl.BlockSpec((1, kd), fixed),
            pl.BlockSpec((1, dv), fixed),
        ],
        out_specs=pl.BlockSpec((tb, vd), lambda t: (t, 0)),
        out_shape=jax.ShapeDtypeStruct((T, vd), BF16),
        scratch_shapes=[
            pltpu.VMEM((H, dv, dk), F32),
            pltpu.VMEM((tb, kd), F32),
            pltpu.VMEM((tb, kd), F32),
            pltpu.VMEM((tb, kd), F32),
            pltpu.VMEM((tb // GLA_CHUNK, GLA_BREF_ROWS, kd), F32),
        ],
        compiler_params=_params("arbitrary"),
        name="gla",
    )(proj, proj, proj, proj, gl, w_gk2, b_gk, head_norm)


def _route(hn, rw_ref, rb_ref):
    tm = hn.shape[0]
    hi = hn.astype(BF16)
    lo = (hn - hi.astype(F32)).astype(BF16)
    both = _dot(hi, rw_ref[...])
    logits = both[:, :LANES] + both[:, LANES:] + _dot(lo, rw_ref[:, :LANES]) + rb_ref[...]
    lane = lax.broadcasted_iota(I32, (tm, LANES), 1)
    lane_f = lane.astype(F32)
    neg = -jnp.inf
    far = float(LANES)

    def first_max(vals):
        m = jnp.max(vals, axis=-1, keepdims=True)
        idx = jnp.min(jnp.where(vals == m, lane_f, far), axis=-1, keepdims=True)
        return m, idx

    lg = jnp.where((lane >= N_EXPERTS) & (lane < N_EXPERTS + N_GROUPS), logits, neg)
    mg, gidx = first_max(lg)
    top_gp = 1.0 / jnp.sum(jnp.exp(lg - mg), axis=-1, keepdims=True)
    grp = gidx.astype(I32) - N_EXPERTS
    le = jnp.where((lane < N_EXPERTS) & ((lane // EXPERTS_PER_GROUP) == grp), logits, neg)
    m1, i1 = first_max(le)
    m2, i2 = first_max(jnp.where(lane_f == i1, neg, le))
    e2 = jnp.exp(m2 - m1)
    w1 = 1.0 / (1.0 + e2)
    return i1.astype(I32), i2.astype(I32), top_gp * w1, top_gp * (e2 * w1)


def _out_router_kernel(*refs, conv):
    if conv:
        (pb_ref, pc_ref, ph_ref, hc_ref, hh_ref, cw_ref, w_ref, hin_ref, fnw_ref, rw_ref,
         rb_ref, h_ref, hn_ref, id0_ref, id1_ref, g0_ref, g1_ref, cnt_ref, hbuf0, hbuf1) = refs
    else:
        (y_ref, w_ref, hin_ref, fnw_ref, rw_ref,
         rb_ref, h_ref, hn_ref, id0_ref, id1_ref, g0_ref, g1_ref, cnt_ref, hbuf0, hbuf1) = refs
    i = pl.program_id(0)
    hbufs = (hbuf0, hbuf1)

    @pl.when(i == 0)
    def _():
        cnt_ref[...] = jnp.zeros_like(cnt_ref)
        hbuf1[...] = jnp.zeros_like(hbuf1)

    def project(dst):
        if conv:
            u = pc_ref[...].astype(F32) * ph_ref[...].astype(F32)
            prev = hc_ref[...].astype(F32) * hh_ref[...].astype(F32)
            prev = jnp.where(i > 0, prev, 0.0)
            last, last2 = prev[BF16_ROWS - 1:BF16_ROWS], prev[BF16_ROWS - 2:BF16_ROWS - 1]
            r1, r2 = pltpu.roll(u, 1, 0), pltpu.roll(u, 2, 0)
            row = lax.broadcasted_iota(I32, (SUBLANES, u.shape[1]), 0)
            top1 = jnp.where(row == 0, last, r1[:SUBLANES])
            top2 = jnp.where(row == 0, last2, jnp.where(row == 1, last, r2[:SUBLANES]))
            u1 = jnp.concatenate([top1, r1[SUBLANES:]], axis=0)
            u2 = jnp.concatenate([top2, r2[SUBLANES:]], axis=0)
            cw = cw_ref[...]
            y = (pb_ref[...].astype(F32) * (cw[0:1] * u2 + cw[1:2] * u1 + cw[2:3] * u)).astype(BF16)
        else:
            y = y_ref[...]
        h = hin_ref[...] + _dot(y, w_ref[...])
        h_ref[...] = h
        dst[...] = h

    def route(src):
        hn = _rms(src[...], fnw_ref[...])
        _store_packed(hn_ref, hn)
        i1, i2, g1, g2 = _route(hn, rw_ref, rb_ref)
        shape = id0_ref.shape
        id0_ref[...] = jnp.broadcast_to(i1, shape)
        id1_ref[...] = jnp.broadcast_to(i2, shape)
        g0_ref[...] = jnp.broadcast_to(g1, shape)
        g1_ref[...] = jnp.broadcast_to(g2, shape)
        lane = lax.broadcasted_iota(I32, shape, 1)
        hit = jnp.where(((lane == i1) | (lane == i2)) & (i > 0), 1.0, 0.0)
        cnt_ref[...] += jnp.sum(hit, axis=0, keepdims=True)

    for s in range(2):
        @pl.when(i % 2 == s)
        def _():
            route(hbufs[1 - s])
            project(hbufs[s])


def out_router(y, w_out, h_in, ffn_norm_w, r_w, r_b, conv_w=None):
    T, D = h_in.shape
    K = w_out.shape[0]
    tm = _tile(T, 256)
    n_tiles = T // tm
    conv = conv_w is not None
    tile = lambda i: jnp.minimum(i, n_tiles - 1)
    row = lambda i: (tile(i), 0)
    routed = lambda i: (jnp.maximum(i - 1, 0), 0)
    fixed = lambda i: (0, 0)
    if conv:
        hb = tm // BF16_ROWS
        halo = lambda col: (lambda i: (jnp.maximum(tile(i) * hb - 1, 0), col))
        in_specs = [
            pl.BlockSpec((tm, K), lambda i: (tile(i), 0)),
            pl.BlockSpec((tm, K), lambda i: (tile(i), 1)),
            pl.BlockSpec((tm, K), lambda i: (tile(i), 2)),
            pl.BlockSpec((BF16_ROWS, K), halo(1)),
            pl.BlockSpec((BF16_ROWS, K), halo(2)),
            pl.BlockSpec((SUBLANES, K), fixed),
        ]
        cw = jnp.zeros((SUBLANES, K), F32).at[:CONV_WIDTH].set(conv_w)
        args = [y, y, y, y, y, cw]
    else:
        in_specs = [pl.BlockSpec((tm, K), row)]
        args = [y]
    in_specs += [
        pl.BlockSpec((K, D), fixed),
        pl.BlockSpec((tm, D), row),
        pl.BlockSpec((1, D), fixed),
        pl.BlockSpec((D, 2 * LANES), fixed),
        pl.BlockSpec((1, LANES), fixed),
    ]
    args += [w_out, h_in, ffn_norm_w.reshape(1, D), r_w, r_b]
    wide = lambda dt: jax.ShapeDtypeStruct((T, LANES), dt)
    sub = _packed_sublanes(D)
    return pl.pallas_call(
        functools.partial(_out_router_kernel, conv=conv),
        grid=(n_tiles + 1,),
        in_specs=in_specs,
        out_specs=[pl.BlockSpec((tm, D), row), pl.BlockSpec((tm * sub, LANES), routed)]
        + [pl.BlockSpec((tm, LANES), routed)] * 4 + [pl.BlockSpec((1, LANES), fixed)],
        out_shape=[jax.ShapeDtypeStruct((T, D), F32), jax.ShapeDtypeStruct((T * sub, LANES), I32),
                   wide(I32), wide(I32), wide(F32), wide(F32), jax.ShapeDtypeStruct((1, LANES), F32)],
        scratch_shapes=[pltpu.VMEM((tm, D), F32)] * 2,
        compiler_params=_params("arbitrary"),
        name="out_router_conv" if conv else "out_router",
    )(*args)


def _rank_kernel(id0_ref, id1_ref, pst_ref, d0_ref, d1_ref, carry_ref):
    @pl.when(pl.program_id(0) == 0)
    def _():
        carry_ref[...] = jnp.zeros_like(carry_ref)

    tb = id0_ref.shape[0]
    lane = lax.broadcasted_iota(I32, (tb, LANES), 1)
    oh0 = lane == id0_ref[...]
    oh1 = lane == id1_ref[...]
    hit = jnp.where(oh0 | oh1, 1.0, 0.0)
    r = lax.broadcasted_iota(I32, (tb, tb), 0)
    c = lax.broadcasted_iota(I32, (tb, tb), 1)
    before = jnp.where(c < r, 1.0, 0.0).astype(BF16)
    base = _dot(before, hit.astype(BF16)) + carry_ref[...] + pst_ref[...]
    d0 = jnp.sum(jnp.where(oh0, base, 0.0), axis=-1, keepdims=True)
    d1 = jnp.sum(jnp.where(oh1, base, 0.0), axis=-1, keepdims=True)
    d0_ref[...] = jnp.broadcast_to(d0.astype(I32), (tb, LANES))
    d1_ref[...] = jnp.broadcast_to(d1.astype(I32), (tb, LANES))
    carry_ref[...] += jnp.sum(hit, axis=0, keepdims=True)


def rank(id0, id1, pstart_row):
    T = id0.shape[0]
    tb = _tile(T, 1024)
    row = lambda i: (i, 0)
    return pl.pallas_call(
        _rank_kernel,
        grid=(T // tb,),
        in_specs=[pl.BlockSpec((tb, LANES), row), pl.BlockSpec((tb, LANES), row),
                  pl.BlockSpec((1, LANES), lambda i: (0, 0))],
        out_specs=[pl.BlockSpec((tb, LANES), row)] * 2,
        out_shape=[jax.ShapeDtypeStruct((T, LANES), I32)] * 2,
        scratch_shapes=[pltpu.VMEM((1, LANES), F32)],
        compiler_params=_params("arbitrary"),
        name="rank",
    )(id0, id1, pstart_row)


def _invert_kernel(d0_ref, d1_ref, fill_ref, rt_ref):
    width = SUBLANES

    def clear(p, carry):
        for k in range(width):
            rt_ref[p * width + k] = 0
        return carry

    for e in range(fill_ref.shape[0] // 2):
        lax.fori_loop(fill_ref[2 * e] // width, fill_ref[2 * e + 1] // width, clear, 0)

    def place(t, carry):
        rt_ref[d0_ref[t]] = t
        rt_ref[d1_ref[t]] = t
        return carry

    lax.fori_loop(0, d0_ref.shape[0], place, 0, unroll=16)


def invert(d0, d1, fill_ranges, n_rows):
    smem = pl.BlockSpec(memory_space=pltpu.SMEM)
    return pl.pallas_call(
        _invert_kernel,
        in_specs=[smem, smem, smem],
        out_specs=smem,
        out_shape=jax.ShapeDtypeStruct((n_rows,), I32),
        name="invert",
    )(d0, d1, fill_ranges)


def _row_copy(src_hbm, dst_ref, sem, src_row, dst_row, sub):
    src = src_hbm.at[pl.ds(pl.multiple_of(src_row * sub, sub), sub), :]
    dst = dst_ref.at[pl.ds(pl.multiple_of(dst_row * sub, sub), sub), :]
    return pltpu.make_async_copy(src, dst, sem)


def _experts_kernel(bexp_ref, first_ref, wslot_ref, next_ref, rtok_ref, nused_ref, hn_hbm, wgu_hbm, wd_hbm, y_ref,
                    xbuf0, xbuf1, xbuf2, wgu_f32, wd_f32, wgu_bf, wd_bf, sem, wsem, *, layer, ff, rb, sub):
    i = pl.program_id(0)
    n_used = nused_ref[0]
    bufs = (xbuf0, xbuf1, xbuf2)
    n_buf = len(bufs)
    assert n_buf == GATHER_AHEAD + 1

    def weight_copies(expert, slot):
        return (pltpu.make_async_copy(wgu_hbm.at[layer, expert], wgu_f32.at[slot], wsem.at[0, slot]),
                pltpu.make_async_copy(wd_hbm.at[layer, expert], wd_f32.at[slot], wsem.at[1, slot]))

    def wait_rows(s):
        pltpu.make_async_copy(hn_hbm.at[pl.ds(0, rb * sub), :], bufs[s], sem.at[s]).wait()

    def start_rows(blk, s):
        for r in range(rb):
            _row_copy(hn_hbm, bufs[s], sem.at[s], rtok_ref[blk * rb + r], r, sub).start(priority=ROW_QUEUE)

    @pl.when(i == 0)
    def _():
        for cp in weight_copies(bexp_ref[0], 0):
            cp.start(priority=WEIGHT_QUEUE)
        for blk in range(GATHER_AHEAD):
            start_rows(blk, blk)

    @pl.when(first_ref[i] == 1)
    def _():
        slot = wslot_ref[i]
        for cp in weight_copies(bexp_ref[i], slot):
            cp.wait()
        nxt = next_ref[i]

        @pl.when(nxt >= 0)
        def _():
            for cp in weight_copies(nxt, 1 - slot):
                cp.start(priority=WEIGHT_QUEUE)

        wgu_bf[...] = wgu_f32[slot].astype(BF16)
        wd_bf[...] = wd_f32[slot].astype(BF16)

    for s in range(n_buf):
        @pl.when((i < n_used) & (i % n_buf == s))
        def _():
            wait_rows(s)
            start_rows(i + GATHER_AHEAD, (s + GATHER_AHEAD) % n_buf)
            los, his = _load_packed(bufs[s], rb, sub)
            x = jnp.concatenate([p.astype(BF16) for p in los + his], axis=1)
            gu = _dot(x, wgu_bf[...])
            gt, up = gu[:, :ff], gu[:, ff:]
            act = (gt * jax.nn.sigmoid(gt) * up).astype(BF16)
            _store_packed(y_ref, _dot(act, wd_bf[...]))

        @pl.when((i >= n_used) & (i < n_used + GATHER_AHEAD) & (i % n_buf == s))
        def _():
            wait_rows(s)

    @pl.when(i >= n_used)
    def _():
        y_ref[...] = jnp.zeros_like(y_ref)


def experts(block_exp, row_tok, n_used, hn, w_gate_up, w_down, layer):
    D, ff2 = w_gate_up.shape[-2:]
    ff = ff2 // 2
    sub = _packed_sublanes(D)
    P = row_tok.shape[0]
    rb = ROW_BLOCK
    n_blocks = P // rb

    blk = jnp.arange(n_blocks, dtype=I32)
    prev = jnp.concatenate([jnp.full((1,), -1, I32), block_exp[:-1]])
    first = (blk < n_used[0]) & (block_exp != prev)
    wslot = (jnp.cumsum(first.astype(I32)) - 1) % 2
    first_at = lax.cummin(jnp.where(first, blk, n_blocks), reverse=True)
    next_first = jnp.concatenate([first_at[1:], jnp.full((1,), n_blocks, I32)])
    next_exp = jnp.where(next_first < n_blocks, block_exp[jnp.minimum(next_first, n_blocks - 1)], -1)

    grid_spec = pltpu.PrefetchScalarGridSpec(
        num_scalar_prefetch=6,
        grid=(n_blocks,),
        in_specs=[pl.BlockSpec(memory_space=pl.ANY)] * 3,
        out_specs=pl.BlockSpec((rb * sub, LANES), lambda i, *_: (i, 0)),
        scratch_shapes=[pltpu.VMEM((rb * sub, LANES), I32)] * (GATHER_AHEAD + 1)
        + [pltpu.VMEM((2, D, ff2), F32), pltpu.VMEM((2, ff, D), F32),
           pltpu.VMEM((D, ff2), BF16), pltpu.VMEM((ff, D), BF16),
           pltpu.SemaphoreType.DMA((GATHER_AHEAD + 1,)), pltpu.SemaphoreType.DMA((2, 2))],
    )
    return pl.pallas_call(
        functools.partial(_experts_kernel, layer=layer, ff=ff, rb=rb, sub=sub),
        grid_spec=grid_spec,
        out_shape=jax.ShapeDtypeStruct((P * sub, LANES), I32),
        compiler_params=_params("arbitrary"),
        name="experts",
    )(block_exp, first.astype(I32), wslot.astype(I32), next_exp.astype(I32), row_tok, n_used,
      hn, w_gate_up, w_down)


def _combine_kernel(d0_ref, d1_ref, yb_hbm, h_ref, g0_ref, g1_ref, nw_ref, o_ref, b00, b01, b10, b11, sem,
                    *, final_norm, sub):
    i = pl.program_id(0)
    n = pl.num_programs(0)
    tc = o_ref.shape[0]
    bufs = ((b00, b01), (b10, b11))
    d_refs = (d0_ref, d1_ref)

    def wait_rows(s):
        for k in range(TOP_K):
            pltpu.make_async_copy(yb_hbm.at[pl.ds(0, tc * sub), :], bufs[s][k], sem.at[s]).wait()

    def start_rows(blk, s):
        for r in range(tc):
            for k in range(TOP_K):
                _row_copy(yb_hbm, bufs[s][k], sem.at[s], d_refs[k][blk * tc + r], r, sub).start(priority=k)

    @pl.when(i == 0)
    def _():
        start_rows(0, 0)

    for s in range(2):
        @pl.when(i % 2 == s)
        def _():
            wait_rows(s)
            start_rows(i + 1, 1 - s)
            lo0, hi0 = _load_packed(bufs[s][0], tc, sub)
            lo1, hi1 = _load_packed(bufs[s][1], tc, sub)
            g0, g1 = g0_ref[...], g1_ref[...]
            pieces = [h_ref[:, j * LANES:(j + 1) * LANES] + g0 * y0 + g1 * y1
                      for j, (y0, y1) in enumerate(zip(lo0 + hi0, lo1 + hi1))]
            if final_norm:
                o_ref[...] = _rms(jnp.concatenate(pieces, axis=1), nw_ref[...])
            else:
                for j, piece in enumerate(pieces):
                    o_ref[:, j * LANES:(j + 1) * LANES] = piece

            @pl.when(i == n - 1)
            def _():
                wait_rows(1 - s)


def combine(d0, d1, yb, h, g0, g1, norm_w=None):
    T, D = h.shape
    tc = _tile(T, 256)
    sub = _packed_sublanes(D)
    final_norm = norm_w is not None
    nw = (norm_w if final_norm else jnp.ones((D,), F32)).reshape(1, D)
    spare = jnp.zeros((tc,), I32)
    row = lambda i, a, b: (i, 0)
    grid_spec = pltpu.PrefetchScalarGridSpec(
        num_scalar_prefetch=2,
        grid=(T // tc,),
        in_specs=[
            pl.BlockSpec(memory_space=pl.ANY),
            pl.BlockSpec((tc, D), row),
            pl.BlockSpec((tc, LANES), row),
            pl.BlockSpec((tc, LANES), row),
            pl.BlockSpec((1, D), lambda i, a, b: (0, 0)),
        ],
        out_specs=pl.BlockSpec((tc, D), row),
        scratch_shapes=[pltpu.VMEM((tc * sub, LANES), I32)] * (2 * TOP_K) + [pltpu.SemaphoreType.DMA((2,))],
    )
    return pl.pallas_call(
        functools.partial(_combine_kernel, final_norm=final_norm, sub=sub),
        grid_spec=grid_spec,
        out_shape=jax.ShapeDtypeStruct((T, D), F32),
        compiler_params=_params("arbitrary"),
        name="combine_norm" if final_norm else "combine",
    )(jnp.concatenate([d0, spare]), jnp.concatenate([d1, spare]), yb, h, g0, g1, nw)


def _pad_cols(w, n):
    return jnp.pad(w, ((0, 0), (0, n - w.shape[1])))


def moe(h, hn, id0, id1, g0, g1, counts, w_gate_up, w_down, layer, final_norm_w=None):
    T, D = h.shape
    rb = ROW_BLOCK
    n_blocks = -(-(T * TOP_K + N_EXPERTS * (rb - 1)) // rb) + GATHER_AHEAD
    cnt = counts[0, :N_EXPERTS].astype(I32)
    padded = ((cnt + rb - 1) // rb) * rb
    pends = jnp.cumsum(padded)
    pstart_row = _pad_cols((pends - padded).astype(F32).reshape(1, N_EXPERTS), LANES)
    d0b, d1b = rank(id0, id1, pstart_row)
    d0, d1 = d0b[:, 0], d1b[:, 0]
    fill_start = jnp.concatenate([pends - padded + cnt, pends[-1:]])
    fill_end = jnp.concatenate([pends, jnp.full((1,), n_blocks * rb, I32)])
    row_tok = invert(d0, d1, jnp.stack([fill_start, fill_end], axis=1).reshape(-1), n_blocks * rb)
    block_start = jnp.arange(n_blocks, dtype=I32) * rb
    block_exp = jnp.minimum(jnp.sum(pends[None, :] <= block_start[:, None], axis=1), N_EXPERTS - 1).astype(I32)
    n_used = (pends[-1:] // rb).astype(I32)
    yb = experts(block_exp, row_tok, n_used, hn, w_gate_up, w_down, layer)
    return combine(d0, d1, yb, h, g0, g1, final_norm_w)


def _router_weights(w_group, b_group, w_router, b_router):
    w = _pad_cols(jnp.concatenate([w_router, w_group], axis=1), LANES)
    b = _pad_cols(jnp.concatenate([b_router, b_group]).reshape(1, -1), LANES)
    hi = w.astype(BF16)
    lo = (w - hi.astype(F32)).astype(BF16)
    return jnp.concatenate([hi, lo], axis=1), b


def kernel(x, mix_norm, gla_w_in, gla_w_gk2, gla_b_gk, gla_head_norm, gla_w_out, conv_w_in, conv_w,
           conv_w_out, ffn_norm, w_group, b_group, w_router, b_router, w_gate_up, w_down, final_norm):
    B, T, D = x.shape
    assert B == 1, "the recurrence state is carried across the whole row axis"
    h = x.reshape(T, D)
    kd, vd = D // 2, D
    dk, dv = kd // GLA_HEADS, vd // GLA_HEADS

    w_in = gla_w_in[0]
    n_main = 2 * kd + 2 * vd
    proj, gl = norm_proj(h, mix_norm[0], w_in.astype(BF16),
                         _pad_cols(w_in[:, n_main:], LANES).astype(BF16), n_cols=n_main)
    w_gk2 = jnp.pad(gla_w_gk2[0], ((0, LANES - GATE_RANK), (0, 0))).astype(BF16)
    o = gla(proj, gl, w_gk2, gla_b_gk[0].reshape(1, kd), gla_head_norm[0].reshape(1, dv), dk=dk, dv=dv)
    routed = out_router(o, gla_w_out[0].astype(BF16), h, ffn_norm[0],
                        *_router_weights(w_group[0], b_group[0], w_router[0], b_router[0]))
    h = moe(*routed, w_gate_up, w_down, 0)

    proj = norm_proj(h, mix_norm[1], conv_w_in[0].astype(BF16))
    routed = out_router(proj, conv_w_out[0].astype(BF16), h, ffn_norm[1],
                        *_router_weights(w_group[1], b_group[1], w_router[1], b_router[1]),
                        conv_w=conv_w[0])
    out = moe(*routed, w_gate_up, w_down, 1, final_norm_w=final_norm)
    return out.reshape(B, T, D)
```

```python
import functools

import jax
import jax.numpy as jnp
from jax import lax
from jax.experimental import pallas as pl
from jax.experimental.pallas import tpu as pltpu

F32 = jnp.float32
BF16 = jnp.bfloat16
I32 = jnp.int32

EPS = 1e-6
GLA_HEADS = 4
GATE_RANK = 16
GATE_NORMALIZER = 16.0
CONV_WIDTH = 3
N_GROUPS = 4
EXPERTS_PER_GROUP = 8
N_EXPERTS = N_GROUPS * EXPERTS_PER_GROUP
TOP_K = 2

LANES = 128
SUBLANES = 8
BF16_ROWS = 16
HIGH_HALF = -65536
VMEM_LIMIT = 56 * 1024 * 1024

GLA_CHUNK = 128
GLA_SUB = 16
GLA_BREF_ROWS = -(-(GLA_CHUNK // GLA_SUB + 1) // SUBLANES) * SUBLANES
GLA_GROUP = 128
GLA_SAFE_DECAY = 60.0
ROW_BLOCK = 256
GATHER_AHEAD = 2
ROW_QUEUE, WEIGHT_QUEUE = 0, 1


def _tile(n, pref):
    t = min(n, pref)
    while n % t:
        t -= LANES
    assert t > 0, (n, pref)
    return t


def _params(*sem):
    return pltpu.CompilerParams(dimension_semantics=sem, vmem_limit_bytes=VMEM_LIMIT)


def _dot(a, b):
    return jnp.dot(a, b, preferred_element_type=F32)


def _dot_nt(a, b):
    return lax.dot_general(a, b, (((1,), (1,)), ((), ())), preferred_element_type=F32)


def _dot_tn(a, b):
    return lax.dot_general(a, b, (((0,), (0,)), ((), ())), preferred_element_type=F32)


def _rms(x, w):
    return x * lax.rsqrt(jnp.mean(x * x, axis=-1, keepdims=True) + EPS) * w


def _packed_sublanes(d):
    assert d % (2 * LANES) == 0
    return d // (2 * LANES)


def _store_packed(ref, x):
    rows, d = x.shape
    half = d // 2
    sub = half // LANES
    lo = lax.shift_right_logical(lax.bitcast_convert_type(x[:, :half].astype(BF16).astype(F32), I32), 16)
    hi = lax.bitcast_convert_type(x[:, half:].astype(BF16).astype(F32), I32) & HIGH_HALF
    words = hi | lo
    for s in range(sub):
        ref[pl.ds(s, rows, stride=sub), :] = words[:, s * LANES:(s + 1) * LANES]


def _load_packed(ref, rows, sub):
    los, his = [], []
    for s in range(sub):
        w = ref[pl.ds(s, rows, stride=sub), :]
        los.append(lax.bitcast_convert_type(lax.shift_left(w, 16), F32))
        his.append(lax.bitcast_convert_type(w & HIGH_HALF, F32))
    return los, his


def _norm_proj_kernel(*refs, with_extra):
    if with_extra:
        x_ref, nw_ref, w_ref, we_ref, o_ref, e_ref, xn_ref = refs
    else:
        x_ref, nw_ref, w_ref, o_ref, xn_ref = refs

    @pl.when(pl.program_id(1) == 0)
    def _():
        xn_ref[...] = _rms(x_ref[...], nw_ref[...]).astype(BF16)
        if with_extra:
            e_ref[...] = _dot(xn_ref[...], we_ref[...])

    o_ref[...] = _dot(xn_ref[...], w_ref[...]).astype(o_ref.dtype)


def norm_proj(x, norm_w, w, w_extra=None, n_cols=None):
    T, D = x.shape
    N = w.shape[1] if n_cols is None else n_cols
    tm, tn = _tile(T, 1024), _tile(N, 2048)
    in_specs = [
        pl.BlockSpec((tm, D), lambda i, j: (i, 0)),
        pl.BlockSpec((1, D), lambda i, j: (0, 0)),
        pl.BlockSpec((D, tn), lambda i, j: (0, j)),
    ]
    out_shape = [jax.ShapeDtypeStruct((T, N), BF16)]
    out_specs = [pl.BlockSpec((tm, tn), lambda i, j: (i, j))]
    args = [x, norm_w.reshape(1, D), w]
    if w_extra is not None:
        in_specs.append(pl.BlockSpec((D, LANES), lambda i, j: (0, 0)))
        out_shape.append(jax.ShapeDtypeStruct((T, LANES), F32))
        out_specs.append(pl.BlockSpec((tm, LANES), lambda i, j: (i, 0)))
        args.append(w_extra)
    out = pl.pallas_call(
        functools.partial(_norm_proj_kernel, with_extra=w_extra is not None),
        grid=(T // tm, N // tn),
        in_specs=in_specs,
        out_specs=out_specs,
        out_shape=out_shape,
        scratch_shapes=[pltpu.VMEM((tm, D), BF16)],
        compiler_params=_params("arbitrary", "arbitrary"),
        name="norm_proj",
    )(*args)
    return out if w_extra is not None else out[0]


def _proj_kernel(x_ref, w_ref, o_ref):
    o_ref[...] = _dot(x_ref[...], w_ref[...]).astype(o_ref.dtype)


def proj_only(xn, w):
    T, D = xn.shape
    N = w.shape[1]
    tm, tn = _tile(T, 1024), _tile(N, 2048)
    return pl.pallas_call(
        _proj_kernel,
        grid=(T // tm, N // tn),
        in_specs=[pl.BlockSpec((tm, D), lambda i, j: (i, 0)), pl.BlockSpec((D, tn), lambda i, j: (0, j))],
        out_specs=pl.BlockSpec((tm, tn), lambda i, j: (i, j)),
        out_shape=jax.ShapeDtypeStruct((T, N), BF16),
        compiler_params=_params("arbitrary", "arbitrary"),
        name="proj",
    )(xn, w)


def _gla_kernel(q_ref, k_ref, v_ref, g_ref, gl_ref, wgk_ref, bgk_ref, hnorm_ref, o_ref,
                st_ref, bs_ref, qt_ref, kh_ref, bref_ref, *, dk, dv, tb):
    C, SUB, GRP, H, BR = GLA_CHUNK, GLA_SUB, GLA_GROUP, GLA_HEADS, GLA_BREF_ROWS
    NS = C // SUB
    CPG = GRP // C
    kd = H * dk
    scale = dk ** -0.5

    @pl.when(pl.program_id(0) == 0)
    def _():
        st_ref[...] = jnp.zeros_like(st_ref)

    r = lax.broadcasted_iota(I32, (GRP, GRP), 0)
    c = lax.broadcasted_iota(I32, (GRP, GRP), 1)
    same = (r // SUB) == (c // SUB)
    low = jnp.where(same & (c <= r), 1.0, 0.0).astype(BF16)
    upp = jnp.where(same & (c > r), 1.0, 0.0).astype(BF16)
    rr = lax.broadcasted_iota(I32, (CPG * BR, GRP), 0)
    cc = lax.broadcasted_iota(I32, (CPG * BR, GRP), 1)
    blk = rr % BR
    mref = jnp.where(((cc // C) == (rr // BR)) & ((cc % C) < SUB * blk) & (blk <= NS),
                     1.0, 0.0).astype(BF16)

    for gi in range(tb // GRP):
        sl = pl.ds(gi * GRP, GRP)
        z = _dot(gl_ref[sl, :].astype(BF16), wgk_ref[...]) + bgk_ref[...]
        la = (jnp.minimum(z, 0.0) - jnp.log(1.0 + jnp.exp(-jnp.abs(z)))) * (1.0 / GATE_NORMALIZER)
        hi = la.astype(BF16)
        lo = (la - hi.astype(F32)).astype(BF16)
        bs = _dot(low, hi) + _dot(low, lo)
        ru = _dot(upp, hi) + _dot(upp, lo)
        br = _dot(mref, hi) + _dot(mref, lo)
        bs_ref[sl, :] = bs
        qt_ref[sl, :] = q_ref[sl, :].astype(F32) * jnp.exp(bs) * scale
        kh_ref[sl, :] = k_ref[sl, :].astype(F32) * jnp.exp(ru)
        bref_ref[pl.ds(gi * CPG, CPG)] = br.reshape(CPG, BR, kd)

    lane = lax.broadcasted_iota(I32, (SUB, C), 1)
    subrow = lax.broadcasted_iota(I32, (SUB, dk), 0)

    def scores_pivoted(qin, kraw, bs, bn):
        neg_b = jnp.concatenate([-(bs[J * SUB:(J + 1) * SUB] + bn[J:J + 1]) for J in range(NS)], axis=0)
        s = _dot_nt(qin.astype(BF16), (kraw * jnp.exp(neg_b)).astype(BF16))
        ri = lax.broadcasted_iota(I32, (C, C), 0)
        ci = lax.broadcasted_iota(I32, (C, C), 1)
        return jnp.where(ci <= ri, s, 0.0)

    def scores_stable(qt, kh, qraw, kraw, bs, bn):
        a_rows = []
        for I in range(NS):
            blk_rows = slice(I * SUB, (I + 1) * SUB)
            q_blk, k_blk, bs_blk = qraw[blk_rows], kraw[blk_rows], bs[blk_rows]
            acc = jnp.zeros((SUB, C), F32)
            for j in range(SUB):
                e = jnp.where(subrow >= j, jnp.exp(bs_blk - bs_blk[j:j + 1]), 0.0)
                p = (q_blk * k_blk[j:j + 1]) * e
                acc = jnp.where(lane == I * SUB + j, jnp.sum(p, axis=-1, keepdims=True), acc)
            if I > 0:
                parts = []
                for J in range(NS):
                    if J < I:
                        parts.append(kh[J * SUB:(J + 1) * SUB] * jnp.exp(bn[I:I + 1] - bn[J + 1:J + 2]))
                    else:
                        parts.append(jnp.zeros((SUB, dk), F32))
                k_dec = jnp.concatenate(parts, axis=0)
                acc = acc + _dot_nt(qt[blk_rows].astype(BF16), k_dec.astype(BF16))
            a_rows.append(acc)
        return jnp.concatenate(a_rows, axis=0)

    def chunk(n, carry, *, pivoted):
        r0 = pl.multiple_of(n * C, C)
        rows = pl.ds(r0, C)
        outs = []
        for h in range(H):
            kl = slice(h * dk, (h + 1) * dk)
            vl = slice(h * dv, (h + 1) * dv)
            bn = bref_ref[n, :, kl]
            eb = jnp.exp(bn)
            st = st_ref[h]
            qt = qt_ref[rows, kl]
            kh = kh_ref[rows, kl]
            bs = bs_ref[rows, kl]
            v = v_ref[rows, vl]
            kraw = k_ref[rows, kl].astype(F32)

            qin = jnp.concatenate([qt[I * SUB:(I + 1) * SUB] * eb[I:I + 1] for I in range(NS)], axis=0)
            o = _dot_nt(qin.astype(BF16), st.astype(BF16))
            if pivoted:
                a = scores_pivoted(qin, kraw, bs, bn)
            else:
                a = scores_stable(qt, kh, q_ref[rows, kl].astype(F32) * scale, kraw, bs, bn)
            o = o + _dot(a.astype(BF16), v)

            k_st = jnp.concatenate(
                [kh[J * SUB:(J + 1) * SUB] * jnp.exp(bn[NS:NS + 1] - bn[J + 1:J + 2]) for J in range(NS)],
                axis=0)
            st_ref[h] = st * eb[NS:NS + 1] + _dot_tn(v, k_st.astype(BF16))
            outs.append(_rms(o, hnorm_ref[...]))

        g = g_ref[rows, :].astype(F32)
        o_ref[rows, :] = (jnp.concatenate(outs, axis=1) * (g * jax.nn.sigmoid(g))).astype(o_ref.dtype)
        return carry

    span = jnp.max(-bref_ref[...])

    @pl.when(span < GLA_SAFE_DECAY)
    def _():
        lax.fori_loop(0, tb // C, functools.partial(chunk, pivoted=True), 0, unroll=2)

    @pl.when(jnp.logical_not(span < GLA_SAFE_DECAY))
    def _():
        lax.fori_loop(0, tb // C, functools.partial(chunk, pivoted=False), 0)


def gla(proj, gl, w_gk2, b_gk, head_norm, *, dk, dv):
    T = proj.shape[0]
    H = GLA_HEADS
    tb = _tile(T, 512)
    assert tb % GLA_GROUP == 0 and dk % LANES == 0 and dv % LANES == 0
    kd, vd = H * dk, H * dv
    assert (2 * kd) % vd == 0
    v_off = (2 * kd) // vd
    fixed = lambda t: (0, 0)
    return pl.pallas_call(
        functools.partial(_gla_kernel, dk=dk, dv=dv, tb=tb),
        grid=(T // tb,),
        in_specs=[
            pl.BlockSpec((tb, kd), lambda t: (t, 0)),
            pl.BlockSpec((tb, kd), lambda t: (t, 1)),
            pl.BlockSpec((tb, vd), lambda t: (t, v_off)),
            pl.BlockSpec((tb, vd), lambda t: (t, v_off + 1)),
            pl.BlockSpec((tb, LANES), lambda t: (t, 0)),
            pl.BlockSpec((LANES, kd), fixed),
            pl.BlockSpec((1, kd), fixed),
            pl.BlockSpec((1, dv), fixed),
        ],
        out_specs=pl.BlockSpec((tb, vd), lambda t: (t, 0)),
        out_shape=jax.ShapeDtypeStruct((T, vd), BF16),
        scratch_shapes=[
            pltpu.VMEM((H, dv, dk), F32),
            pltpu.VMEM((tb, kd), F32),
            pltpu.VMEM((tb, kd), F32),
            pltpu.VMEM((tb, kd), F32),
            pltpu.VMEM((tb // GLA_CHUNK, GLA_BREF_ROWS, kd), F32),
        ],
        compiler_params=_params("arbitrary"),
        name="gla",
    )(proj, proj, proj, proj, gl, w_gk2, b_gk, head_norm)


def _route(hn, rw_ref, rb_ref):
    tm = hn.shape[0]
    hi = hn.astype(BF16)
    lo = (hn - hi.astype(F32)).astype(BF16)
    both = _dot(hi, rw_ref[...])
    logits = both[:, :LANES] + both[:, LANES:] + _dot(lo, rw_ref[:, :LANES]) + rb_ref[...]
    lane = lax.broadcasted_iota(I32, (tm, LANES), 1)
    lane_f = lane.astype(F32)
    neg = -jnp.inf
    far = float(LANES)

    def first_max(vals):
        m = jnp.max(vals, axis=-1, keepdims=True)
        idx = jnp.min(jnp.where(vals == m, lane_f, far), axis=-1, keepdims=True)
        return m, idx

    lg = jnp.where((lane >= N_EXPERTS) & (lane < N_EXPERTS + N_GROUPS), logits, neg)
    mg, gidx = first_max(lg)
    top_gp = 1.0 / jnp.sum(jnp.exp(lg - mg), axis=-1, keepdims=True)
    grp = gidx.astype(I32) - N_EXPERTS
    le = jnp.where((lane < N_EXPERTS) & ((lane // EXPERTS_PER_GROUP) == grp), logits, neg)
    m1, i1 = first_max(le)
    m2, i2 = first_max(jnp.where(lane_f == i1, neg, le))
    e2 = jnp.exp(m2 - m1)
    w1 = 1.0 / (1.0 + e2)
    return i1.astype(I32), i2.astype(I32), top_gp * w1, top_gp * (e2 * w1)


def _out_router_kernel(*refs, conv):
    if conv:
        (pb_ref, pc_ref, ph_ref, hc_ref, hh_ref, cw_ref, w_ref, hin_ref, fnw_ref, rw_ref,
         rb_ref, h_ref, hn_ref, id0_ref, id1_ref, g0_ref, g1_ref, cnt_ref, hbuf0, hbuf1) = refs
    else:
        (y_ref, w_ref, hin_ref, fnw_ref, rw_ref,
         rb_ref, h_ref, hn_ref, id0_ref, id1_ref, g0_ref, g1_ref, cnt_ref, hbuf0, hbuf1) = refs
    i = pl.program_id(0)
    hbufs = (hbuf0, hbuf1)

    @pl.when(i == 0)
    def _():
        cnt_ref[...] = jnp.zeros_like(cnt_ref)
        hbuf1[...] = jnp.zeros_like(hbuf1)

    def project(dst):
        if conv:
            u = pc_ref[...].astype(F32) * ph_ref[...].astype(F32)
            prev = hc_ref[...].astype(F32) * hh_ref[...].astype(F32)
            prev = jnp.where(i > 0, prev, 0.0)
            last, last2 = prev[BF16_ROWS - 1:BF16_ROWS], prev[BF16_ROWS - 2:BF16_ROWS - 1]
            r1, r2 = pltpu.roll(u, 1, 0), pltpu.roll(u, 2, 0)
            row = lax.broadcasted_iota(I32, (SUBLANES, u.shape[1]), 0)
            top1 = jnp.where(row == 0, last, r1[:SUBLANES])
            top2 = jnp.where(row == 0, last2, jnp.where(row == 1, last, r2[:SUBLANES]))
            u1 = jnp.concatenate([top1, r1[SUBLANES:]], axis=0)
            u2 = jnp.concatenate([top2, r2[SUBLANES:]], axis=0)
            cw = cw_ref[...]
            y = (pb_ref[...].astype(F32) * (cw[0:1] * u2 + cw[1:2] * u1 + cw[2:3] * u)).astype(BF16)
        else:
            y = y_ref[...]
        h = hin_ref[...] + _dot(y, w_ref[...])
        h_ref[...] = h
        dst[...] = h

    def route(src):
        hn = _rms(src[...], fnw_ref[...])
        _store_packed(hn_ref, hn)
        i1, i2, g1, g2 = _route(hn, rw_ref, rb_ref)
        shape = id0_ref.shape
        id0_ref[...] = jnp.broadcast_to(i1, shape)
        id1_ref[...] = jnp.broadcast_to(i2, shape)
        g0_ref[...] = jnp.broadcast_to(g1, shape)
        g1_ref[...] = jnp.broadcast_to(g2, shape)
        lane = lax.broadcasted_iota(I32, shape, 1)
        hit = jnp.where(((lane == i1) | (lane == i2)) & (i > 0), 1.0, 0.0)
        cnt_ref[...] += jnp.sum(hit, axis=0, keepdims=True)

    for s in range(2):
        @pl.when(i % 2 == s)
        def _():
            route(hbufs[1 - s])
            project(hbufs[s])


def out_router(y, w_out, h_in, ffn_norm_w, r_w, r_b, conv_w=None):
    T, D = h_in.shape
    K = w_out.shape[0]
    tm = _tile(T, 256)
    n_tiles = T // tm
    conv = conv_w is not None
    tile = lambda i: jnp.minimum(i, n_tiles - 1)
    row = lambda i: (tile(i), 0)
    routed = lambda i: (jnp.maximum(i - 1, 0), 0)
    fixed = lambda i: (0, 0)
    if conv:
        hb = tm // BF16_ROWS
        halo = lambda col: (lambda i: (jnp.maximum(tile(i) * hb - 1, 0), col))
        in_specs = [
            pl.BlockSpec((tm, K), lambda i: (tile(i), 0)),
            pl.BlockSpec((tm, K), lambda i: (tile(i), 1)),
            pl.BlockSpec((tm, K), lambda i: (tile(i), 2)),
            pl.BlockSpec((BF16_ROWS, K), halo(1)),
            pl.BlockSpec((BF16_ROWS, K), halo(2)),
            pl.BlockSpec((SUBLANES, K), fixed),
        ]
        cw = jnp.zeros((SUBLANES, K), F32).at[:CONV_WIDTH].set(conv_w)
        args = [y, y, y, y, y, cw]
    else:
        in_specs = [pl.BlockSpec((tm, K), row)]
        args = [y]
    in_specs += [
        pl.BlockSpec((K, D), fixed),
        pl.BlockSpec((tm, D), row),
        pl.BlockSpec((1, D), fixed),
        pl.BlockSpec((D, 2 * LANES), fixed),
        pl.BlockSpec((1, LANES), fixed),
    ]
    args += [w_out, h_in, ffn_norm_w.reshape(1, D), r_w, r_b]
    wide = lambda dt: jax.ShapeDtypeStruct((T, LANES), dt)
    sub = _packed_sublanes(D)
    return pl.pallas_call(
        functools.partial(_out_router_kernel, conv=conv),
        grid=(n_tiles + 1,),
        in_specs=in_specs,
        out_specs=[pl.BlockSpec((tm, D), row), pl.BlockSpec((tm * sub, LANES), routed)]
        + [pl.BlockSpec((tm, LANES), routed)] * 4 + [pl.BlockSpec((1, LANES), fixed)],
        out_shape=[jax.ShapeDtypeStruct((T, D), F32), jax.ShapeDtypeStruct((T * sub, LANES), I32),
                   wide(I32), wide(I32), wide(F32), wide(F32), jax.ShapeDtypeStruct((1, LANES), F32)],
        scratch_shapes=[pltpu.VMEM((tm, D), F32)] * 2,
        compiler_params=_params("arbitrary"),
        name="out_router_conv" if conv else "out_router",
    )(*args)


def _rank_kernel(id0_ref, id1_ref, pst_ref, d0_ref, d1_ref, carry_ref):
    @pl.when(pl.program_id(0) == 0)
    def _():
        carry_ref[...] = jnp.zeros_like(carry_ref)

    tb = id0_ref.shape[0]
    lane = lax.broadcasted_iota(I32, (tb, LANES), 1)
    oh0 = lane == id0_ref[...]
    oh1 = lane == id1_ref[...]
    hit = jnp.where(oh0 | oh1, 1.0, 0.0)
    r = lax.broadcasted_iota(I32, (tb, tb), 0)
    c = lax.broadcasted_iota(I32, (tb, tb), 1)
    before = jnp.where(c < r, 1.0, 0.0).astype(BF16)
    base = _dot(before, hit.astype(BF16)) + carry_ref[...] + pst_ref[...]
    d0 = jnp.sum(jnp.where(oh0, base, 0.0), axis=-1, keepdims=True)
    d1 = jnp.sum(jnp.where(oh1, base, 0.0), axis=-1, keepdims=True)
    d0_ref[...] = jnp.broadcast_to(d0.astype(I32), (tb, LANES))
    d1_ref[...] = jnp.broadcast_to(d1.astype(I32), (tb, LANES))
    carry_ref[...] += jnp.sum(hit, axis=0, keepdims=True)


def rank(id0, id1, pstart_row):
    T = id0.shape[0]
    tb = _tile(T, 1024)
    row = lambda i: (i, 0)
    return pl.pallas_call(
        _rank_kernel,
        grid=(T // tb,),
        in_specs=[pl.BlockSpec((tb, LANES), row), pl.BlockSpec((tb, LANES), row),
                  pl.BlockSpec((1, LANES), lambda i: (0, 0))],
        out_specs=[pl.BlockSpec((tb, LANES), row)] * 2,
        out_shape=[jax.ShapeDtypeStruct((T, LANES), I32)] * 2,
        scratch_shapes=[pltpu.VMEM((1, LANES), F32)],
        compiler_params=_params("arbitrary"),
        name="rank",
    )(id0, id1, pstart_row)


def _invert_kernel(d0_ref, d1_ref, fill_ref, rt_ref):
    width = SUBLANES

    def clear(p, carry):
        for k in range(width):
            rt_ref[p * width + k] = 0
        return carry

    for e in range(fill_ref.shape[0] // 2):
        lax.fori_loop(fill_ref[2 * e] // width, fill_ref[2 * e + 1] // width, clear, 0)

    def place(t, carry):
        rt_ref[d0_ref[t]] = t
        rt_ref[d1_ref[t]] = t
        return carry

    lax.fori_loop(0, d0_ref.shape[0], place, 0, unroll=16)


def invert(d0, d1, fill_ranges, n_rows):
    smem = pl.BlockSpec(memory_space=pltpu.SMEM)
    return pl.pallas_call(
        _invert_kernel,
        in_specs=[smem, smem, smem],
        out_specs=smem,
        out_shape=jax.ShapeDtypeStruct((n_rows,), I32),
        name="invert",
    )(d0, d1, fill_ranges)


def _row_copy(src_hbm, dst_ref, sem, src_row, dst_row, sub):
    src = src_hbm.at[pl.ds(pl.multiple_of(src_row * sub, sub), sub), :]
    dst = dst_ref.at[pl.ds(pl.multiple_of(dst_row * sub, sub), sub), :]
    return pltpu.make_async_copy(src, dst, sem)


def _experts_kernel(bexp_ref, first_ref, wslot_ref, next_ref, rtok_ref, nused_ref, hn_hbm, wgu_hbm, wd_hbm, y_ref,
                    xbuf0, xbuf1, xbuf2, wgu_f32, wd_f32, wgu_bf, wd_bf, sem, wsem, *, layer, ff, rb, sub):
    i = pl.program_id(0)
    n_used = nused_ref[0]
    bufs = (xbuf0, xbuf1, xbuf2)
    n_buf = len(bufs)
    assert n_buf == GATHER_AHEAD + 1

    def weight_copies(expert, slot):
        return (pltpu.make_async_copy(wgu_hbm.at[layer, expert], wgu_f32.at[slot], wsem.at[0, slot]),
                pltpu.make_async_copy(wd_hbm.at[layer, expert], wd_f32.at[slot], wsem.at[1, slot]))

    def wait_rows(s):
        pltpu.make_async_copy(hn_hbm.at[pl.ds(0, rb * sub), :], bufs[s], sem.at[s]).wait()

    def start_rows(blk, s):
        for r in range(rb):
            _row_copy(hn_hbm, bufs[s], sem.at[s], rtok_ref[blk * rb + r], r, sub).start(priority=ROW_QUEUE)

    @pl.when(i == 0)
    def _():
        for cp in weight_copies(bexp_ref[0], 0):
            cp.start(priority=WEIGHT_QUEUE)
        for blk in range(GATHER_AHEAD):
            start_rows(blk, blk)

    @pl.when(first_ref[i] == 1)
    def _():
        slot = wslot_ref[i]
        for cp in weight_copies(bexp_ref[i], slot):
            cp.wait()
        nxt = next_ref[i]

        @pl.when(nxt >= 0)
        def _():
            for cp in weight_copies(nxt, 1 - slot):
                cp.start(priority=WEIGHT_QUEUE)

        wgu_bf[...] = wgu_f32[slot].astype(BF16)
        wd_bf[...] = wd_f32[slot].astype(BF16)

    for s in range(n_buf):
        @pl.when((i < n_used) & (i % n_buf == s))
        def _():
            wait_rows(s)
            start_rows(i + GATHER_AHEAD, (s + GATHER_AHEAD) % n_buf)
            los, his = _load_packed(bufs[s], rb, sub)
            x = jnp.concatenate([p.astype(BF16) for p in los + his], axis=1)
            gu = _dot(x, wgu_bf[...])
            gt, up = gu[:, :ff], gu[:, ff:]
            act = (gt * jax.nn.sigmoid(gt) * up).astype(BF16)
            _store_packed(y_ref, _dot(act, wd_bf[...]))

        @pl.when((i >= n_used) & (i < n_used + GATHER_AHEAD) & (i % n_buf == s))
        def _():
            wait_rows(s)

    @pl.when(i >= n_used)
    def _():
        y_ref[...] = jnp.zeros_like(y_ref)


def experts(block_exp, row_tok, n_used, hn, w_gate_up, w_down, layer):
    D, ff2 = w_gate_up.shape[-2:]
    ff = ff2 // 2
    sub = _packed_sublanes(D)
    P = row_tok.shape[0]
    rb = ROW_BLOCK
    n_blocks = P // rb

    blk = jnp.arange(n_blocks, dtype=I32)
    prev = jnp.concatenate([jnp.full((1,), -1, I32), block_exp[:-1]])
    first = (blk < n_used[0]) & (block_exp != prev)
    wslot = (jnp.cumsum(first.astype(I32)) - 1) % 2
    first_at = lax.cummin(jnp.where(first, blk, n_blocks), reverse=True)
    next_first = jnp.concatenate([first_at[1:], jnp.full((1,), n_blocks, I32)])
    next_exp = jnp.where(next_first < n_blocks, block_exp[jnp.minimum(next_first, n_blocks - 1)], -1)

    grid_spec = pltpu.PrefetchScalarGridSpec(
        num_scalar_prefetch=6,
        grid=(n_blocks,),
        in_specs=[pl.BlockSpec(memory_space=pl.ANY)] * 3,
        out_specs=pl.BlockSpec((rb * sub, LANES), lambda i, *_: (i, 0)),
        scratch_shapes=[pltpu.VMEM((rb * sub, LANES), I32)] * (GATHER_AHEAD + 1)
        + [pltpu.VMEM((2, D, ff2), F32), pltpu.VMEM((2, ff, D), F32),
           pltpu.VMEM((D, ff2), BF16), pltpu.VMEM((ff, D), BF16),
           pltpu.SemaphoreType.DMA((GATHER_AHEAD + 1,)), pltpu.SemaphoreType.DMA((2, 2))],
    )
    return pl.pallas_call(
        functools.partial(_experts_kernel, layer=layer, ff=ff, rb=rb, sub=sub),
        grid_spec=grid_spec,
        out_shape=jax.ShapeDtypeStruct((P * sub, LANES), I32),
        compiler_params=_params("arbitrary"),
        name="experts",
    )(block_exp, first.astype(I32), wslot.astype(I32), next_exp.astype(I32), row_tok, n_used,
      hn, w_gate_up, w_down)


def _combine_kernel(d0_ref, d1_ref, yb_hbm, h_ref, g0_ref, g1_ref, nw_ref, *refs, mode, sub):
    if mode == "both":
        o_ref, on_ref, b00, b01, b10, b11, sem = refs
    else:
        o_ref, b00, b01, b10, b11, sem = refs
    i = pl.program_id(0)
    n = pl.num_programs(0)
    tc = o_ref.shape[0]
    bufs = ((b00, b01), (b10, b11))
    d_refs = (d0_ref, d1_ref)

    def wait_rows(s):
        for k in range(TOP_K):
            pltpu.make_async_copy(yb_hbm.at[pl.ds(0, tc * sub), :], bufs[s][k], sem.at[s]).wait()

    def start_rows(blk, s):
        for r in range(tc):
            for k in range(TOP_K):
                _row_copy(yb_hbm, bufs[s][k], sem.at[s], d_refs[k][blk * tc + r], r, sub).start(priority=k)

    @pl.when(i == 0)
    def _():
        start_rows(0, 0)

    for s in range(2):
        @pl.when(i % 2 == s)
        def _():
            wait_rows(s)
            start_rows(i + 1, 1 - s)
            lo0, hi0 = _load_packed(bufs[s][0], tc, sub)
            lo1, hi1 = _load_packed(bufs[s][1], tc, sub)
            g0, g1 = g0_ref[...], g1_ref[...]
            pieces = [h_ref[:, j * LANES:(j + 1) * LANES] + g0 * y0 + g1 * y1
                      for j, (y0, y1) in enumerate(zip(lo0 + hi0, lo1 + hi1))]
            if mode != "norm":
                for j, piece in enumerate(pieces):
                    o_ref[:, j * LANES:(j + 1) * LANES] = piece
            if mode != "sum":
                normed = _rms(jnp.concatenate(pieces, axis=1), nw_ref[...])
                if mode == "norm":
                    o_ref[...] = normed
                else:
                    on_ref[...] = normed.astype(on_ref.dtype)

            @pl.when(i == n - 1)
            def _():
                wait_rows(1 - s)


def combine(d0, d1, yb, h, g0, g1, norm_w=None, mode="sum"):
    T, D = h.shape
    tc = _tile(T, 256)
    sub = _packed_sublanes(D)
    assert (norm_w is None) == (mode == "sum")
    nw = (jnp.ones((D,), F32) if norm_w is None else norm_w).reshape(1, D)
    spare = jnp.zeros((tc,), I32)
    row = lambda i, a, b: (i, 0)
    out_shape = [jax.ShapeDtypeStruct((T, D), F32)]
    if mode == "both":
        out_shape.append(jax.ShapeDtypeStruct((T, D), BF16))
    grid_spec = pltpu.PrefetchScalarGridSpec(
        num_scalar_prefetch=2,
        grid=(T // tc,),
        in_specs=[
            pl.BlockSpec(memory_space=pl.ANY),
            pl.BlockSpec((tc, D), row),
            pl.BlockSpec((tc, LANES), row),
            pl.BlockSpec((tc, LANES), row),
            pl.BlockSpec((1, D), lambda i, a, b: (0, 0)),
        ],
        out_specs=[pl.BlockSpec((tc, D), row)] * len(out_shape),
        scratch_shapes=[pltpu.VMEM((tc * sub, LANES), I32)] * (2 * TOP_K) + [pltpu.SemaphoreType.DMA((2,))],
    )
    out = pl.pallas_call(
        functools.partial(_combine_kernel, mode=mode, sub=sub),
        grid_spec=grid_spec,
        out_shape=out_shape,
        compiler_params=_params("arbitrary"),
        name="combine_" + mode,
    )(jnp.concatenate([d0, spare]), jnp.concatenate([d1, spare]), yb, h, g0, g1, nw)
    return out if mode == "both" else out[0]


def _pad_cols(w, n):
    return jnp.pad(w, ((0, 0), (0, n - w.shape[1])))


def moe(h, hn, id0, id1, g0, g1, counts, w_gate_up, w_down, layer, norm_w, norm_mode):
    T, D = h.shape
    rb = ROW_BLOCK
    n_blocks = -(-(T * TOP_K + N_EXPERTS * (rb - 1)) // rb) + GATHER_AHEAD
    cnt = counts[0, :N_EXPERTS].astype(I32)
    padded = ((cnt + rb - 1) // rb) * rb
    pends = jnp.cumsum(padded)
    pstart_row = _pad_cols((pends - padded).astype(F32).reshape(1, N_EXPERTS), LANES)
    d0b, d1b = rank(id0, id1, pstart_row)
    d0, d1 = d0b[:, 0], d1b[:, 0]
    fill_start = jnp.concatenate([pends - padded + cnt, pends[-1:]])
    fill_end = jnp.concatenate([pends, jnp.full((1,), n_blocks * rb, I32)])
    row_tok = invert(d0, d1, jnp.stack([fill_start, fill_end], axis=1).reshape(-1), n_blocks * rb)
    block_start = jnp.arange(n_blocks, dtype=I32) * rb
    block_exp = jnp.minimum(jnp.sum(pends[None, :] <= block_start[:, None], axis=1), N_EXPERTS - 1).astype(I32)
    n_used = (pends[-1:] // rb).astype(I32)
    yb = experts(block_exp, row_tok, n_used, hn, w_gate_up, w_down, layer)
    return combine(d0, d1, yb, h, g0, g1, norm_w, norm_mode)


def _router_weights(w_group, b_group, w_router, b_router):
    w = _pad_cols(jnp.concatenate([w_router, w_group], axis=1), LANES)
    b = _pad_cols(jnp.concatenate([b_router, b_group]).reshape(1, -1), LANES)
    hi = w.astype(BF16)
    lo = (w - hi.astype(F32)).astype(BF16)
    return jnp.concatenate([hi, lo], axis=1), b


def kernel(x, mix_norm, gla_w_in, gla_w_gk2, gla_b_gk, gla_head_norm, gla_w_out, conv_w_in, conv_w,
           conv_w_out, ffn_norm, w_group, b_group, w_router, b_router, w_gate_up, w_down, final_norm):
    B, T, D = x.shape
    assert B == 1, "the recurrence state is carried across the whole row axis"
    h = x.reshape(T, D)
    kd, vd = D // 2, D
    dk, dv = kd // GLA_HEADS, vd // GLA_HEADS

    w_in = gla_w_in[0]
    n_main = 2 * kd + 2 * vd
    proj, gl = norm_proj(h, mix_norm[0], w_in.astype(BF16),
                         _pad_cols(w_in[:, n_main:], LANES).astype(BF16), n_cols=n_main)
    w_gk2 = jnp.pad(gla_w_gk2[0], ((0, LANES - GATE_RANK), (0, 0))).astype(BF16)
    o = gla(proj, gl, w_gk2, gla_b_gk[0].reshape(1, kd), gla_head_norm[0].reshape(1, dv), dk=dk, dv=dv)
    routed = out_router(o, gla_w_out[0].astype(BF16), h, ffn_norm[0],
                        *_router_weights(w_group[0], b_group[0], w_router[0], b_router[0]))
    h, hn = moe(*routed, w_gate_up, w_down, 0, mix_norm[1], "both")

    proj = proj_only(hn, conv_w_in[0].astype(BF16))
    routed = out_router(proj, conv_w_out[0].astype(BF16), h, ffn_norm[1],
                        *_router_weights(w_group[1], b_group[1], w_router[1], b_router[1]),
                        conv_w=conv_w[0])
    out = moe(*routed, w_gate_up, w_down, 1, final_norm, "norm")
    return out.reshape(B, T, D)
```

```python
import functools

import jax
import jax.numpy as jnp
from jax import lax
from jax.experimental import pallas as pl
from jax.experimental.pallas import tpu as pltpu

F32 = jnp.float32
BF16 = jnp.bfloat16
I32 = jnp.int32

EPS = 1e-6
GLA_HEADS = 4
GATE_RANK = 16
GATE_NORMALIZER = 16.0
CONV_WIDTH = 3
N_GROUPS = 4
EXPERTS_PER_GROUP = 8
N_EXPERTS = N_GROUPS * EXPERTS_PER_GROUP
TOP_K = 2

LANES = 128
SUBLANES = 8
BF16_ROWS = 16
HIGH_HALF = -65536
VMEM_LIMIT = 56 * 1024 * 1024

GLA_CHUNK = 128
GLA_SUB = 16
GLA_BREF_ROWS = -(-(GLA_CHUNK // GLA_SUB + 1) // SUBLANES) * SUBLANES
GLA_GROUP = 128
GLA_SAFE_DECAY = 60.0
ROW_BLOCK = 256
GATHER_AHEAD = 2
ROW_QUEUE, WEIGHT_QUEUE = 0, 1


def _tile(n, pref):
    t = min(n, pref)
    while n % t:
        t -= LANES
    assert t > 0, (n, pref)
    return t


def _params(*sem):
    return pltpu.CompilerParams(dimension_semantics=sem, vmem_limit_bytes=VMEM_LIMIT)


def _dot(a, b):
    return jnp.dot(a, b, preferred_element_type=F32)


def _dot_nt(a, b):
    return lax.dot_general(a, b, (((1,), (1,)), ((), ())), preferred_element_type=F32)


def _dot_tn(a, b):
    return lax.dot_general(a, b, (((0,), (0,)), ((), ())), preferred_element_type=F32)


def _rms(x, w):
    return x * lax.rsqrt(jnp.mean(x * x, axis=-1, keepdims=True) + EPS) * w


def _packed_sublanes(d):
    assert d % (2 * LANES) == 0
    return d // (2 * LANES)


def _store_packed(ref, x):
    rows, d = x.shape
    half = d // 2
    sub = half // LANES
    lo = lax.shift_right_logical(lax.bitcast_convert_type(x[:, :half].astype(BF16).astype(F32), I32), 16)
    hi = lax.bitcast_convert_type(x[:, half:].astype(BF16).astype(F32), I32) & HIGH_HALF
    words = hi | lo
    for s in range(sub):
        ref[pl.ds(s, rows, stride=sub), :] = words[:, s * LANES:(s + 1) * LANES]


def _load_packed(ref, rows, sub):
    los, his = [], []
    for s in range(sub):
        w = ref[pl.ds(s, rows, stride=sub), :]
        los.append(lax.bitcast_convert_type(lax.shift_left(w, 16), F32))
        his.append(lax.bitcast_convert_type(w & HIGH_HALF, F32))
    return los, his


def _norm_proj_kernel(*refs, with_extra):
    if with_extra:
        x_ref, nw_ref, w_ref, we_ref, o_ref, e_ref, xn_ref = refs
    else:
        x_ref, nw_ref, w_ref, o_ref, xn_ref = refs

    @pl.when(pl.program_id(1) == 0)
    def _():
        xn_ref[...] = _rms(x_ref[...], nw_ref[...]).astype(BF16)
        if with_extra:
            e_ref[...] = _dot(xn_ref[...], we_ref[...])

    o_ref[...] = _dot(xn_ref[...], w_ref[...]).astype(o_ref.dtype)


def norm_proj(x, norm_w, w, w_extra=None, n_cols=None):
    T, D = x.shape
    N = w.shape[1] if n_cols is None else n_cols
    tm, tn = _tile(T, 1024), _tile(N, 2048)
    in_specs = [
        pl.BlockSpec((tm, D), lambda i, j: (i, 0)),
        pl.BlockSpec((1, D), lambda i, j: (0, 0)),
        pl.BlockSpec((D, tn), lambda i, j: (0, j)),
    ]
    out_shape = [jax.ShapeDtypeStruct((T, N), BF16)]
    out_specs = [pl.BlockSpec((tm, tn), lambda i, j: (i, j))]
    args = [x, norm_w.reshape(1, D), w]
    if w_extra is not None:
        in_specs.append(pl.BlockSpec((D, LANES), lambda i, j: (0, 0)))
        out_shape.append(jax.ShapeDtypeStruct((T, LANES), F32))
        out_specs.append(pl.BlockSpec((tm, LANES), lambda i, j: (i, 0)))
        args.append(w_extra)
    out = pl.pallas_call(
        functools.partial(_norm_proj_kernel, with_extra=w_extra is not None),
        grid=(T // tm, N // tn),
        in_specs=in_specs,
        out_specs=out_specs,
        out_shape=out_shape,
        scratch_shapes=[pltpu.VMEM((tm, D), BF16)],
        compiler_params=_params("arbitrary", "arbitrary"),
        name="norm_proj",
    )(*args)
    return out if w_extra is not None else out[0]


def _proj_kernel(x_ref, w_ref, o_ref):
    o_ref[...] = _dot(x_ref[...], w_ref[...]).astype(o_ref.dtype)


def proj_only(xn, w):
    T, D = xn.shape
    N = w.shape[1]
    tm, tn = _tile(T, 1024), _tile(N, 2048)
    return pl.pallas_call(
        _proj_kernel,
        grid=(T // tm, N // tn),
        in_specs=[pl.BlockSpec((tm, D), lambda i, j: (i, 0)), pl.BlockSpec((D, tn), lambda i, j: (0, j))],
        out_specs=pl.BlockSpec((tm, tn), lambda i, j: (i, j)),
        out_shape=jax.ShapeDtypeStruct((T, N), BF16),
        compiler_params=_params("arbitrary", "arbitrary"),
        name="proj",
    )(xn, w)


def _gla_kernel(q_ref, k_ref, v_ref, g_ref, gl_ref, wgk_ref, bgk_ref, hnorm_ref, o_ref,
                st_ref, bs_ref, qt_ref, kh_ref, bref_ref, *, dk, dv, tb):
    C, SUB, GRP, H, BR = GLA_CHUNK, GLA_SUB, GLA_GROUP, GLA_HEADS, GLA_BREF_ROWS
    NS = C // SUB
    CPG = GRP // C
    kd = H * dk
    scale = dk ** -0.5

    @pl.when(pl.program_id(0) == 0)
    def _():
        st_ref[...] = jnp.zeros_like(st_ref)

    r = lax.broadcasted_iota(I32, (GRP, GRP), 0)
    c = lax.broadcasted_iota(I32, (GRP, GRP), 1)
    same = (r // SUB) == (c // SUB)
    low = jnp.where(same & (c <= r), 1.0, 0.0).astype(BF16)
    upp = jnp.where(same & (c > r), 1.0, 0.0).astype(BF16)
    rr = lax.broadcasted_iota(I32, (CPG * BR, GRP), 0)
    cc = lax.broadcasted_iota(I32, (CPG * BR, GRP), 1)
    blk = rr % BR
    mref = jnp.where(((cc // C) == (rr // BR)) & ((cc % C) < SUB * blk) & (blk <= NS),
                     1.0, 0.0).astype(BF16)

    groups = [pl.ds(gi * GRP, GRP) for gi in range(tb // GRP)]
    zs = [_dot(gl_ref[sl, :].astype(BF16), wgk_ref[...]) + bgk_ref[...] for sl in groups]
    las = [(jnp.minimum(z, 0.0) - jnp.log(1.0 + jnp.exp(-jnp.abs(z)))) * (1.0 / GATE_NORMALIZER) for z in zs]
    his = [la.astype(BF16) for la in las]
    los = [(la - hi.astype(F32)).astype(BF16) for la, hi in zip(las, his)]
    prefix = lambda m: [_dot(m, hi) + _dot(m, lo) for hi, lo in zip(his, los)]
    bss, rus, brs = prefix(low), prefix(upp), prefix(mref)
    for gi, sl in enumerate(groups):
        bs_ref[sl, :] = bss[gi]
        qt_ref[sl, :] = q_ref[sl, :].astype(F32) * jnp.exp(bss[gi]) * scale
        kh_ref[sl, :] = k_ref[sl, :].astype(F32) * jnp.exp(rus[gi])
        bref_ref[pl.ds(gi * CPG, CPG)] = brs[gi].reshape(CPG, BR, kd)

    lane = lax.broadcasted_iota(I32, (SUB, C), 1)
    subrow = lax.broadcasted_iota(I32, (SUB, dk), 0)

    def scores_pivoted(qin, kraw, bs, bn):
        neg_b = jnp.concatenate([-(bs[J * SUB:(J + 1) * SUB] + bn[J:J + 1]) for J in range(NS)], axis=0)
        s = _dot_nt(qin.astype(BF16), (kraw * jnp.exp(neg_b)).astype(BF16))
        ri = lax.broadcasted_iota(I32, (C, C), 0)
        ci = lax.broadcasted_iota(I32, (C, C), 1)
        return jnp.where(ci <= ri, s, 0.0)

    def scores_stable(qt, kh, qraw, kraw, bs, bn):
        a_rows = []
        for I in range(NS):
            blk_rows = slice(I * SUB, (I + 1) * SUB)
            q_blk, k_blk, bs_blk = qraw[blk_rows], kraw[blk_rows], bs[blk_rows]
            acc = jnp.zeros((SUB, C), F32)
            for j in range(SUB):
                e = jnp.where(subrow >= j, jnp.exp(bs_blk - bs_blk[j:j + 1]), 0.0)
                p = (q_blk * k_blk[j:j + 1]) * e
                acc = jnp.where(lane == I * SUB + j, jnp.sum(p, axis=-1, keepdims=True), acc)
            if I > 0:
                parts = []
                for J in range(NS):
                    if J < I:
                        parts.append(kh[J * SUB:(J + 1) * SUB] * jnp.exp(bn[I:I + 1] - bn[J + 1:J + 2]))
                    else:
                        parts.append(jnp.zeros((SUB, dk), F32))
                k_dec = jnp.concatenate(parts, axis=0)
                acc = acc + _dot_nt(qt[blk_rows].astype(BF16), k_dec.astype(BF16))
            a_rows.append(acc)
        return jnp.concatenate(a_rows, axis=0)

    def chunk(n, carry, *, pivoted):
        r0 = pl.multiple_of(n * C, C)
        rows = pl.ds(r0, C)
        heads = range(H)
        kls = [slice(h * dk, (h + 1) * dk) for h in heads]
        bns = [bref_ref[n, :, kl] for kl in kls]
        ebs = [jnp.exp(bn) for bn in bns]
        sts = [st_ref[h] for h in heads]
        qts = [qt_ref[rows, kl] for kl in kls]
        khs = [kh_ref[rows, kl] for kl in kls]
        vs = [v_ref[rows, h * dv:(h + 1) * dv] for h in heads]
        qins = [jnp.concatenate([qt[I * SUB:(I + 1) * SUB] * eb[I:I + 1] for I in range(NS)], axis=0)
                for qt, eb in zip(qts, ebs)]
        os = [_dot_nt(qin.astype(BF16), st.astype(BF16)) for qin, st in zip(qins, sts)]
        if pivoted:
            scores = [scores_pivoted(qins[h], k_ref[rows, kls[h]].astype(F32), bs_ref[rows, kls[h]], bns[h])
                      for h in heads]
        else:
            scores = [scores_stable(qts[h], khs[h], q_ref[rows, kls[h]].astype(F32) * scale,
                                    k_ref[rows, kls[h]].astype(F32), bs_ref[rows, kls[h]], bns[h])
                      for h in heads]
        os = [o + _dot(a.astype(BF16), v) for o, a, v in zip(os, scores, vs)]
        k_sts = [jnp.concatenate(
            [kh[J * SUB:(J + 1) * SUB] * jnp.exp(bn[NS:NS + 1] - bn[J + 1:J + 2]) for J in range(NS)], axis=0)
            for kh, bn in zip(khs, bns)]
        for h in heads:
            st_ref[h] = sts[h] * ebs[h][NS:NS + 1] + _dot_tn(vs[h], k_sts[h].astype(BF16))
        outs = [_rms(o, hnorm_ref[...]) for o in os]

        g = g_ref[rows, :].astype(F32)
        o_ref[rows, :] = (jnp.concatenate(outs, axis=1) * (g * jax.nn.sigmoid(g))).astype(o_ref.dtype)
        return carry

    span = jnp.max(-bref_ref[...])

    @pl.when(span < GLA_SAFE_DECAY)
    def _():
        lax.fori_loop(0, tb // C, functools.partial(chunk, pivoted=True), 0, unroll=2)

    @pl.when(jnp.logical_not(span < GLA_SAFE_DECAY))
    def _():
        lax.fori_loop(0, tb // C, functools.partial(chunk, pivoted=False), 0)


def gla(proj, gl, w_gk2, b_gk, head_norm, *, dk, dv):
    T = proj.shape[0]
    H = GLA_HEADS
    tb = _tile(T, 512)
    assert tb % GLA_GROUP == 0 and dk % LANES == 0 and dv % LANES == 0
    kd, vd = H * dk, H * dv
    assert (2 * kd) % vd == 0
    v_off = (2 * kd) // vd
    fixed = lambda t: (0, 0)
    return pl.pallas_call(
        functools.partial(_gla_kernel, dk=dk, dv=dv, tb=tb),
        grid=(T // tb,),
        in_specs=[
            pl.BlockSpec((tb, kd), lambda t: (t, 0)),
            pl.BlockSpec((tb, kd), lambda t: (t, 1)),
            pl.BlockSpec((tb, vd), lambda t: (t, v_off)),
            pl.BlockSpec((tb, vd), lambda t: (t, v_off + 1)),
            pl.BlockSpec((tb, LANES), lambda t: (t, 0)),
            pl.BlockSpec((LANES, kd), fixed),
            pl.BlockSpec((1, kd), fixed),
            pl.BlockSpec((1, dv), fixed),
        ],
        out_specs=pl.BlockSpec((tb, vd), lambda t: (t, 0)),
        out_shape=jax.ShapeDtypeStruct((T, vd), BF16),
        scratch_shapes=[
            pltpu.VMEM((H, dv, dk), F32),
            pltpu.VMEM((tb, kd), F32),
            pltpu.VMEM((tb, kd), F32),
            pltpu.VMEM((tb, kd), F32),
            pltpu.VMEM((tb // GLA_CHUNK, GLA_BREF_ROWS, kd), F32),
        ],
        compiler_params=_params("arbitrary"),
        name="gla",
    )(proj, proj, proj, proj, gl, w_gk2, b_gk, head_norm)


def _route(hn, rw_ref, rb_ref):
    tm = hn.shape[0]
    hi = hn.astype(BF16)
    lo = (hn - hi.astype(F32)).astype(BF16)
    both = _dot(hi, rw_ref[...])
    logits = both[:, :LANES] + both[:, LANES:] + _dot(lo, rw_ref[:, :LANES]) + rb_ref[...]
    lane = lax.broadcasted_iota(I32, (tm, LANES), 1)
    lane_f = lane.astype(F32)
    neg = -jnp.inf
    far = float(LANES)

    def first_max(vals):
        m = jnp.max(vals, axis=-1, keepdims=True)
        idx = jnp.min(jnp.where(vals == m, lane_f, far), axis=-1, keepdims=True)
        return m, idx

    lg = jnp.where((lane >= N_EXPERTS) & (lane < N_EXPERTS + N_GROUPS), logits, neg)
    mg, gidx = first_max(lg)
    top_gp = 1.0 / jnp.sum(jnp.exp(lg - mg), axis=-1, keepdims=True)
    grp = gidx.astype(I32) - N_EXPERTS
    le = jnp.where((lane < N_EXPERTS) & ((lane // EXPERTS_PER_GROUP) == grp), logits, neg)
    m1, i1 = first_max(le)
    m2, i2 = first_max(jnp.where(lane_f == i1, neg, le))
    e2 = jnp.exp(m2 - m1)
    w1 = 1.0 / (1.0 + e2)
    return i1.astype(I32), i2.astype(I32), top_gp * w1, top_gp * (e2 * w1)


def _out_router_kernel(*refs, conv):
    if conv:
        (pb_ref, pc_ref, ph_ref, hc_ref, hh_ref, cw_ref, w_ref, hin_ref, fnw_ref, rw_ref,
         rb_ref, h_ref, hn_ref, id0_ref, id1_ref, g0_ref, g1_ref, cnt_ref, hbuf0, hbuf1) = refs
    else:
        (y_ref, w_ref, hin_ref, fnw_ref, rw_ref,
         rb_ref, h_ref, hn_ref, id0_ref, id1_ref, g0_ref, g1_ref, cnt_ref, hbuf0, hbuf1) = refs
    i = pl.program_id(0)
    hbufs = (hbuf0, hbuf1)

    @pl.when(i == 0)
    def _():
        cnt_ref[...] = jnp.zeros_like(cnt_ref)
        hbuf1[...] = jnp.zeros_like(hbuf1)

    def project(dst):
        if conv:
            u = pc_ref[...].astype(F32) * ph_ref[...].astype(F32)
            prev = hc_ref[...].astype(F32) * hh_ref[...].astype(F32)
            prev = jnp.where(i > 0, prev, 0.0)
            last, last2 = prev[BF16_ROWS - 1:BF16_ROWS], prev[BF16_ROWS - 2:BF16_ROWS - 1]
            r1, r2 = pltpu.roll(u, 1, 0), pltpu.roll(u, 2, 0)
            row = lax.broadcasted_iota(I32, (SUBLANES, u.shape[1]), 0)
            top1 = jnp.where(row == 0, last, r1[:SUBLANES])
            top2 = jnp.where(row == 0, last2, jnp.where(row == 1, last, r2[:SUBLANES]))
            u1 = jnp.concatenate([top1, r1[SUBLANES:]], axis=0)
            u2 = jnp.concatenate([top2, r2[SUBLANES:]], axis=0)
            cw = cw_ref[...]
            y = (pb_ref[...].astype(F32) * (cw[0:1] * u2 + cw[1:2] * u1 + cw[2:3] * u)).astype(BF16)
        else:
            y = y_ref[...]
        h = hin_ref[...] + _dot(y, w_ref[...])
        h_ref[...] = h
        dst[...] = h

    def route(src):
        hn = _rms(src[...], fnw_ref[...])
        _store_packed(hn_ref, hn)
        i1, i2, g1, g2 = _route(hn, rw_ref, rb_ref)
        shape = id0_ref.shape
        id0_ref[...] = jnp.broadcast_to(i1, shape)
        id1_ref[...] = jnp.broadcast_to(i2, shape)
        g0_ref[...] = jnp.broadcast_to(g1, shape)
        g1_ref[...] = jnp.broadcast_to(g2, shape)
        lane = lax.broadcasted_iota(I32, shape, 1)
        hit = jnp.where(((lane == i1) | (lane == i2)) & (i > 0), 1.0, 0.0)
        cnt_ref[...] += jnp.sum(hit, axis=0, keepdims=True)

    for s in range(2):
        @pl.when(i % 2 == s)
        def _():
            route(hbufs[1 - s])
            project(hbufs[s])


def out_router(y, w_out, h_in, ffn_norm_w, r_w, r_b, conv_w=None):
    T, D = h_in.shape
    K = w_out.shape[0]
    tm = _tile(T, 256)
    n_tiles = T // tm
    conv = conv_w is not None
    tile = lambda i: jnp.minimum(i, n_tiles - 1)
    row = lambda i: (tile(i), 0)
    routed = lambda i: (jnp.maximum(i - 1, 0), 0)
    fixed = lambda i: (0, 0)
    if conv:
        hb = tm // BF16_ROWS
        halo = lambda col: (lambda i: (jnp.maximum(tile(i) * hb - 1, 0), col))
        in_specs = [
            pl.BlockSpec((tm, K), lambda i: (tile(i), 0)),
            pl.BlockSpec((tm, K), lambda i: (tile(i), 1)),
            pl.BlockSpec((tm, K), lambda i: (tile(i), 2)),
            pl.BlockSpec((BF16_ROWS, K), halo(1)),
            pl.BlockSpec((BF16_ROWS, K), halo(2)),
            pl.BlockSpec((SUBLANES, K), fixed),
        ]
        cw = jnp.zeros((SUBLANES, K), F32).at[:CONV_WIDTH].set(conv_w)
        args = [y, y, y, y, y, cw]
    else:
        in_specs = [pl.BlockSpec((tm, K), row)]
        args = [y]
    in_specs += [
        pl.BlockSpec((K, D), fixed),
        pl.BlockSpec((tm, D), row),
        pl.BlockSpec((1, D), fixed),
        pl.BlockSpec((D, 2 * LANES), fixed),
        pl.BlockSpec((1, LANES), fixed),
    ]
    args += [w_out, h_in, ffn_norm_w.reshape(1, D), r_w, r_b]
    wide = lambda dt: jax.ShapeDtypeStruct((T, LANES), dt)
    sub = _packed_sublanes(D)
    return pl.pallas_call(
        functools.partial(_out_router_kernel, conv=conv),
        grid=(n_tiles + 1,),
        in_specs=in_specs,
        out_specs=[pl.BlockSpec((tm, D), row), pl.BlockSpec((tm * sub, LANES), routed)]
        + [pl.BlockSpec((tm, LANES), routed)] * 4 + [pl.BlockSpec((1, LANES), fixed)],
        out_shape=[jax.ShapeDtypeStruct((T, D), F32), jax.ShapeDtypeStruct((T * sub, LANES), I32),
                   wide(I32), wide(I32), wide(F32), wide(F32), jax.ShapeDtypeStruct((1, LANES), F32)],
        scratch_shapes=[pltpu.VMEM((tm, D), F32)] * 2,
        compiler_params=_params("arbitrary"),
        name="out_router_conv" if conv else "out_router",
    )(*args)


def _rank_kernel(id0_ref, id1_ref, pst_ref, d0_ref, d1_ref, carry_ref):
    @pl.when(pl.program_id(0) == 0)
    def _():
        carry_ref[...] = jnp.zeros_like(carry_ref)

    tb = id0_ref.shape[0]
    lane = lax.broadcasted_iota(I32, (tb, LANES), 1)
    oh0 = lane == id0_ref[...]
    oh1 = lane == id1_ref[...]
    hit = jnp.where(oh0 | oh1, 1.0, 0.0)
    r = lax.broadcasted_iota(I32, (tb, tb), 0)
    c = lax.broadcasted_iota(I32, (tb, tb), 1)
    before = jnp.where(c < r, 1.0, 0.0).astype(BF16)
    base = _dot(before, hit.astype(BF16)) + carry_ref[...] + pst_ref[...]
    d0 = jnp.sum(jnp.where(oh0, base, 0.0), axis=-1, keepdims=True)
    d1 = jnp.sum(jnp.where(oh1, base, 0.0), axis=-1, keepdims=True)
    d0_ref[...] = jnp.broadcast_to(d0.astype(I32), (tb, LANES))
    d1_ref[...] = jnp.broadcast_to(d1.astype(I32), (tb, LANES))
    carry_ref[...] += jnp.sum(hit, axis=0, keepdims=True)


def rank(id0, id1, pstart_row):
    T = id0.shape[0]
    tb = _tile(T, 1024)
    row = lambda i: (i, 0)
    return pl.pallas_call(
        _rank_kernel,
        grid=(T // tb,),
        in_specs=[pl.BlockSpec((tb, LANES), row), pl.BlockSpec((tb, LANES), row),
                  pl.BlockSpec((1, LANES), lambda i: (0, 0))],
        out_specs=[pl.BlockSpec((tb, LANES), row)] * 2,
        out_shape=[jax.ShapeDtypeStruct((T, LANES), I32)] * 2,
        scratch_shapes=[pltpu.VMEM((1, LANES), F32)],
        compiler_params=_params("arbitrary"),
        name="rank",
    )(id0, id1, pstart_row)


def _invert_kernel(d0_ref, d1_ref, fill_ref, rt_ref):
    width = SUBLANES

    def clear(p, carry):
        for k in range(width):
            rt_ref[p * width + k] = 0
        return carry

    for e in range(fill_ref.shape[0] // 2):
        lax.fori_loop(fill_ref[2 * e] // width, fill_ref[2 * e + 1] // width, clear, 0)

    def place(t, carry):
        rt_ref[d0_ref[t]] = t
        rt_ref[d1_ref[t]] = t
        return carry

    lax.fori_loop(0, d0_ref.shape[0], place, 0, unroll=16)


def invert(d0, d1, fill_ranges, n_rows):
    smem = pl.BlockSpec(memory_space=pltpu.SMEM)
    return pl.pallas_call(
        _invert_kernel,
        in_specs=[smem, smem, smem],
        out_specs=smem,
        out_shape=jax.ShapeDtypeStruct((n_rows,), I32),
        name="invert",
    )(d0, d1, fill_ranges)


def _row_copy(src_hbm, dst_ref, sem, src_row, dst_row, sub):
    src = src_hbm.at[pl.ds(pl.multiple_of(src_row * sub, sub), sub), :]
    dst = dst_ref.at[pl.ds(pl.multiple_of(dst_row * sub, sub), sub), :]
    return pltpu.make_async_copy(src, dst, sem)


def _experts_kernel(bexp_ref, first_ref, wslot_ref, next_ref, rtok_ref, nused_ref, hn_hbm, wgu_hbm, wd_hbm, y_ref,
                    xbuf0, xbuf1, xbuf2, wgu_f32, wd_f32, wgu_bf, wd_bf, sem, wsem, *, layer, ff, rb, sub):
    i = pl.program_id(0)
    n_used = nused_ref[0]
    bufs = (xbuf0, xbuf1, xbuf2)
    n_buf = len(bufs)
    assert n_buf == GATHER_AHEAD + 1

    def weight_copies(expert, slot):
        return (pltpu.make_async_copy(wgu_hbm.at[layer, expert], wgu_f32.at[slot], wsem.at[0, slot]),
                pltpu.make_async_copy(wd_hbm.at[layer, expert], wd_f32.at[slot], wsem.at[1, slot]))

    def wait_rows(s):
        pltpu.make_async_copy(hn_hbm.at[pl.ds(0, rb * sub), :], bufs[s], sem.at[s]).wait()

    def start_rows(blk, s):
        for r in range(rb):
            _row_copy(hn_hbm, bufs[s], sem.at[s], rtok_ref[blk * rb + r], r, sub).start(priority=ROW_QUEUE)

    @pl.when(i == 0)
    def _():
        for cp in weight_copies(bexp_ref[0], 0):
            cp.start(priority=WEIGHT_QUEUE)
        for blk in range(GATHER_AHEAD):
            start_rows(blk, blk)

    @pl.when(first_ref[i] == 1)
    def _():
        slot = wslot_ref[i]
        for cp in weight_copies(bexp_ref[i], slot):
            cp.wait()
        nxt = next_ref[i]

        @pl.when(nxt >= 0)
        def _():
            for cp in weight_copies(nxt, 1 - slot):
                cp.start(priority=WEIGHT_QUEUE)

        wgu_bf[...] = wgu_f32[slot].astype(BF16)
        wd_bf[...] = wd_f32[slot].astype(BF16)

    for s in range(n_buf):
        @pl.when((i < n_used) & (i % n_buf == s))
        def _():
            wait_rows(s)
            start_rows(i + GATHER_AHEAD, (s + GATHER_AHEAD) % n_buf)
            los, his = _load_packed(bufs[s], rb, sub)
            x = jnp.concatenate([p.astype(BF16) for p in los + his], axis=1)
            gu = _dot(x, wgu_bf[...])
            gt, up = gu[:, :ff], gu[:, ff:]
            act = (gt * jax.nn.sigmoid(gt) * up).astype(BF16)
            _store_packed(y_ref, _dot(act, wd_bf[...]))

        @pl.when((i >= n_used) & (i < n_used + GATHER_AHEAD) & (i % n_buf == s))
        def _():
            wait_rows(s)

    @pl.when(i >= n_used)
    def _():
        y_ref[...] = jnp.zeros_like(y_ref)


def experts(block_exp, row_tok, n_used, hn, w_gate_up, w_down, layer):
    D, ff2 = w_gate_up.shape[-2:]
    ff = ff2 // 2
    sub = _packed_sublanes(D)
    P = row_tok.shape[0]
    rb = ROW_BLOCK
    n_blocks = P // rb

    blk = jnp.arange(n_blocks, dtype=I32)
    prev = jnp.concatenate([jnp.full((1,), -1, I32), block_exp[:-1]])
    first = (blk < n_used[0]) & (block_exp != prev)
    wslot = (jnp.cumsum(first.astype(I32)) - 1) % 2
    first_at = lax.cummin(jnp.where(first, blk, n_blocks), reverse=True)
    next_first = jnp.concatenate([first_at[1:], jnp.full((1,), n_blocks, I32)])
    next_exp = jnp.where(next_first < n_blocks, block_exp[jnp.minimum(next_first, n_blocks - 1)], -1)

    grid_spec = pltpu.PrefetchScalarGridSpec(
        num_scalar_prefetch=6,
        grid=(n_blocks,),
        in_specs=[pl.BlockSpec(memory_space=pl.ANY)] * 3,
        out_specs=pl.BlockSpec((rb * sub, LANES), lambda i, *_: (i, 0)),
        scratch_shapes=[pltpu.VMEM((rb * sub, LANES), I32)] * (GATHER_AHEAD + 1)
        + [pltpu.VMEM((2, D, ff2), F32), pltpu.VMEM((2, ff, D), F32),
           pltpu.VMEM((D, ff2), BF16), pltpu.VMEM((ff, D), BF16),
           pltpu.SemaphoreType.DMA((GATHER_AHEAD + 1,)), pltpu.SemaphoreType.DMA((2, 2))],
    )
    return pl.pallas_call(
        functools.partial(_experts_kernel, layer=layer, ff=ff, rb=rb, sub=sub),
        grid_spec=grid_spec,
        out_shape=jax.ShapeDtypeStruct((P * sub, LANES), I32),
        compiler_params=_params("arbitrary"),
        name="experts",
    )(block_exp, first.astype(I32), wslot.astype(I32), next_exp.astype(I32), row_tok, n_used,
      hn, w_gate_up, w_down)


def _combine_kernel(d0_ref, d1_ref, yb_hbm, h_ref, g0_ref, g1_ref, nw_ref, *refs, mode, sub):
    if mode == "both":
        o_ref, on_ref, b00, b01, b10, b11, sem = refs
    else:
        o_ref, b00, b01, b10, b11, sem = refs
    i = pl.program_id(0)
    n = pl.num_programs(0)
    tc = o_ref.shape[0]
    bufs = ((b00, b01), (b10, b11))
    d_refs = (d0_ref, d1_ref)

    def wait_rows(s):
        for k in range(TOP_K):
            pltpu.make_async_copy(yb_hbm.at[pl.ds(0, tc * sub), :], bufs[s][k], sem.at[s]).wait()

    def start_rows(blk, s):
        for r in range(tc):
            for k in range(TOP_K):
                _row_copy(yb_hbm, bufs[s][k], sem.at[s], d_refs[k][blk * tc + r], r, sub).start(priority=k)

    @pl.when(i == 0)
    def _():
        start_rows(0, 0)

    for s in range(2):
        @pl.when(i % 2 == s)
        def _():
            wait_rows(s)
            start_rows(i + 1, 1 - s)
            lo0, hi0 = _load_packed(bufs[s][0], tc, sub)
            lo1, hi1 = _load_packed(bufs[s][1], tc, sub)
            g0, g1 = g0_ref[...], g1_ref[...]
            pieces = [h_ref[:, j * LANES:(j + 1) * LANES] + g0 * y0 + g1 * y1
                      for j, (y0, y1) in enumerate(zip(lo0 + hi0, lo1 + hi1))]
            if mode != "norm":
                for j, piece in enumerate(pieces):
                    o_ref[:, j * LANES:(j + 1) * LANES] = piece
            if mode != "sum":
                normed = _rms(jnp.concatenate(pieces, axis=1), nw_ref[...])
                if mode == "norm":
                    o_ref[...] = normed
                else:
                    on_ref[...] = normed.astype(on_ref.dtype)

            @pl.when(i == n - 1)
            def _():
                wait_rows(1 - s)


def combine(d0, d1, yb, h, g0, g1, norm_w=None, mode="sum"):
    T, D = h.shape
    tc = _tile(T, 256)
    sub = _packed_sublanes(D)
    assert (norm_w is None) == (mode == "sum")
    nw = (jnp.ones((D,), F32) if norm_w is None else norm_w).reshape(1, D)
    spare = jnp.zeros((tc,), I32)
    row = lambda i, a, b: (i, 0)
    out_shape = [jax.ShapeDtypeStruct((T, D), F32)]
    if mode == "both":
        out_shape.append(jax.ShapeDtypeStruct((T, D), BF16))
    grid_spec = pltpu.PrefetchScalarGridSpec(
        num_scalar_prefetch=2,
        grid=(T // tc,),
        in_specs=[
            pl.BlockSpec(memory_space=pl.ANY),
            pl.BlockSpec((tc, D), row),
            pl.BlockSpec((tc, LANES), row),
            pl.BlockSpec((tc, LANES), row),
            pl.BlockSpec((1, D), lambda i, a, b: (0, 0)),
        ],
        out_specs=[pl.BlockSpec((tc, D), row)] * len(out_shape),
        scratch_shapes=[pltpu.VMEM((tc * sub, LANES), I32)] * (2 * TOP_K) + [pltpu.SemaphoreType.DMA((2,))],
    )
    out = pl.pallas_call(
        functools.partial(_combine_kernel, mode=mode, sub=sub),
        grid_spec=grid_spec,
        out_shape=out_shape,
        compiler_params=_params("arbitrary"),
        name="combine_" + mode,
    )(jnp.concatenate([d0, spare]), jnp.concatenate([d1, spare]), yb, h, g0, g1, nw)
    return out if mode == "both" else out[0]


def _pad_cols(w, n):
    return jnp.pad(w, ((0, 0), (0, n - w.shape[1])))


def moe(h, hn, id0, id1, g0, g1, counts, w_gate_up, w_down, layer, norm_w, norm_mode):
    T, D = h.shape
    rb = ROW_BLOCK
    n_blocks = -(-(T * TOP_K + N_EXPERTS * (rb - 1)) // rb) + GATHER_AHEAD
    cnt = counts[0, :N_EXPERTS].astype(I32)
    padded = ((cnt + rb - 1) // rb) * rb
    pends = jnp.cumsum(padded)
    pstart_row = _pad_cols((pends - padded).astype(F32).reshape(1, N_EXPERTS), LANES)
    d0b, d1b = rank(id0, id1, pstart_row)
    d0, d1 = d0b[:, 0], d1b[:, 0]
    fill_start = jnp.concatenate([pends - padded + cnt, pends[-1:]])
    fill_end = jnp.concatenate([pends, jnp.full((1,), n_blocks * rb, I32)])
    row_tok = invert(d0, d1, jnp.stack([fill_start, fill_end], axis=1).reshape(-1), n_blocks * rb)
    block_start = jnp.arange(n_blocks, dtype=I32) * rb
    block_exp = jnp.minimum(jnp.sum(pends[None, :] <= block_start[:, None], axis=1), N_EXPERTS - 1).astype(I32)
    n_used = (pends[-1:] // rb).astype(I32)
    yb = experts(block_exp, row_tok, n_used, hn, w_gate_up, w_down, layer)
    return combine(d0, d1, yb, h, g0, g1, norm_w, norm_mode)


def _router_weights(w_group, b_group, w_router, b_router):
    w = _pad_cols(jnp.concatenate([w_router, w_group], axis=1), LANES)
    b = _pad_cols(jnp.concatenate([b_router, b_group]).reshape(1, -1), LANES)
    hi = w.astype(BF16)
    lo = (w - hi.astype(F32)).astype(BF16)
    return jnp.concatenate([hi, lo], axis=1), b


def kernel(x, mix_norm, gla_w_in, gla_w_gk2, gla_b_gk, gla_head_norm, gla_w_out, conv_w_in, conv_w,
           conv_w_out, ffn_norm, w_group, b_group, w_router, b_router, w_gate_up, w_down, final_norm):
    B, T, D = x.shape
    assert B == 1, "the recurrence state is carried across the whole row axis"
    h = x.reshape(T, D)
    kd, vd = D // 2, D
    dk, dv = kd // GLA_HEADS, vd // GLA_HEADS

    w_in = gla_w_in[0]
    n_main = 2 * kd + 2 * vd
    proj, gl = norm_proj(h, mix_norm[0], w_in.astype(BF16),
                         _pad_cols(w_in[:, n_main:], LANES).astype(BF16), n_cols=n_main)
    w_gk2 = jnp.pad(gla_w_gk2[0], ((0, LANES - GATE_RANK), (0, 0))).astype(BF16)
    o = gla(proj, gl, w_gk2, gla_b_gk[0].reshape(1, kd), gla_head_norm[0].reshape(1, dv), dk=dk, dv=dv)
    routed = out_router(o, gla_w_out[0].astype(BF16), h, ffn_norm[0],
                        *_router_weights(w_group[0], b_group[0], w_router[0], b_router[0]))
    h, hn = moe(*routed, w_gate_up, w_down, 0, mix_norm[1], "both")

    proj = proj_only(hn, conv_w_in[0].astype(BF16))
    routed = out_router(proj, conv_w_out[0].astype(BF16), h, ffn_norm[1],
                        *_router_weights(w_group[1], b_group[1], w_router[1], b_router[1]),
                        conv_w=conv_w[0])
    out = moe(*routed, w_gate_up, w_down, 1, final_norm, "norm")
    return out.reshape(B, T, D)
```

```python
import functools

import jax
import jax.numpy as jnp
from jax import lax
from jax.experimental import pallas as pl
from jax.experimental.pallas import tpu as pltpu

F32 = jnp.float32
BF16 = jnp.bfloat16
I32 = jnp.int32

EPS = 1e-6
GLA_HEADS = 4
GATE_RANK = 16
GATE_NORMALIZER = 16.0
CONV_WIDTH = 3
N_GROUPS = 4
EXPERTS_PER_GROUP = 8
N_EXPERTS = N_GROUPS * EXPERTS_PER_GROUP
TOP_K = 2

LANES = 128
SUBLANES = 8
BF16_ROWS = 16
HIGH_HALF = -65536
VMEM_LIMIT = 56 * 1024 * 1024

GLA_CHUNK = 128
GLA_SUB = 16
GLA_BREF_ROWS = -(-(GLA_CHUNK // GLA_SUB + 1) // SUBLANES) * SUBLANES
GLA_GROUP = 128
GLA_SAFE_DECAY = 60.0
ROW_BLOCK = 256
GATHER_AHEAD = 2
ROW_QUEUE, WEIGHT_QUEUE = 0, 1


def _tile(n, pref):
    t = min(n, pref)
    while n % t:
        t -= LANES
    assert t > 0, (n, pref)
    return t


def _params(*sem):
    return pltpu.CompilerParams(dimension_semantics=sem, vmem_limit_bytes=VMEM_LIMIT)


def _dot(a, b):
    return jnp.dot(a, b, preferred_element_type=F32)


def _dot_nt(a, b):
    return lax.dot_general(a, b, (((1,), (1,)), ((), ())), preferred_element_type=F32)


def _dot_tn(a, b):
    return lax.dot_general(a, b, (((0,), (0,)), ((), ())), preferred_element_type=F32)


def _rms(x, w):
    return x * lax.rsqrt(jnp.mean(x * x, axis=-1, keepdims=True) + EPS) * w


def _packed_sublanes(d):
    assert d % (2 * LANES) == 0
    return d // (2 * LANES)


def _store_packed(ref, x):
    rows, d = x.shape
    half = d // 2
    sub = half // LANES
    lo = lax.shift_right_logical(lax.bitcast_convert_type(x[:, :half].astype(BF16).astype(F32), I32), 16)
    hi = lax.bitcast_convert_type(x[:, half:].astype(BF16).astype(F32), I32) & HIGH_HALF
    words = hi | lo
    for s in range(sub):
        ref[pl.ds(s, rows, stride=sub), :] = words[:, s * LANES:(s + 1) * LANES]


def _load_packed(ref, rows, sub):
    los, his = [], []
    for s in range(sub):
        w = ref[pl.ds(s, rows, stride=sub), :]
        los.append(lax.bitcast_convert_type(lax.shift_left(w, 16), F32))
        his.append(lax.bitcast_convert_type(w & HIGH_HALF, F32))
    return los, his


def _norm_proj_kernel(*refs, with_extra):
    if with_extra:
        x_ref, nw_ref, w_ref, we_ref, o_ref, e_ref, xn_ref = refs
    else:
        x_ref, nw_ref, w_ref, o_ref, xn_ref = refs

    @pl.when(pl.program_id(1) == 0)
    def _():
        xn_ref[...] = _rms(x_ref[...], nw_ref[...]).astype(BF16)
        if with_extra:
            e_ref[...] = _dot(xn_ref[...], we_ref[...])

    o_ref[...] = _dot(xn_ref[...], w_ref[...]).astype(o_ref.dtype)


def norm_proj(x, norm_w, w, w_extra=None, n_cols=None):
    T, D = x.shape
    N = w.shape[1] if n_cols is None else n_cols
    tm, tn = _tile(T, 1024), _tile(N, 2048)
    in_specs = [
        pl.BlockSpec((tm, D), lambda i, j: (i, 0)),
        pl.BlockSpec((1, D), lambda i, j: (0, 0)),
        pl.BlockSpec((D, tn), lambda i, j: (0, j)),
    ]
    out_shape = [jax.ShapeDtypeStruct((T, N), BF16)]
    out_specs = [pl.BlockSpec((tm, tn), lambda i, j: (i, j))]
    args = [x, norm_w.reshape(1, D), w]
    if w_extra is not None:
        in_specs.append(pl.BlockSpec((D, LANES), lambda i, j: (0, 0)))
        out_shape.append(jax.ShapeDtypeStruct((T, LANES), F32))
        out_specs.append(pl.BlockSpec((tm, LANES), lambda i, j: (i, 0)))
        args.append(w_extra)
    out = pl.pallas_call(
        functools.partial(_norm_proj_kernel, with_extra=w_extra is not None),
        grid=(T // tm, N // tn),
        in_specs=in_specs,
        out_specs=out_specs,
        out_shape=out_shape,
        scratch_shapes=[pltpu.VMEM((tm, D), BF16)],
        compiler_params=_params("arbitrary", "arbitrary"),
        name="norm_proj",
    )(*args)
    return out if w_extra is not None else out[0]


def _proj_kernel(x_ref, w_ref, o_ref):
    o_ref[...] = _dot(x_ref[...], w_ref[...]).astype(o_ref.dtype)


def proj_only(xn, w):
    T, D = xn.shape
    N = w.shape[1]
    tm, tn = _tile(T, 1024), _tile(N, 2048)
    return pl.pallas_call(
        _proj_kernel,
        grid=(T // tm, N // tn),
        in_specs=[pl.BlockSpec((tm, D), lambda i, j: (i, 0)), pl.BlockSpec((D, tn), lambda i, j: (0, j))],
        out_specs=pl.BlockSpec((tm, tn), lambda i, j: (i, j)),
        out_shape=jax.ShapeDtypeStruct((T, N), BF16),
        compiler_params=_params("arbitrary", "arbitrary"),
        name="proj",
    )(xn, w)


def _gla_kernel(q_ref, k_ref, v_ref, g_ref, gl_ref, wgk_ref, bgk_ref, hnorm_ref, o_ref,
                st_ref, bs_ref, qt_ref, kh_ref, bref_ref, *, dk, dv, tb):
    C, SUB, GRP, H, BR = GLA_CHUNK, GLA_SUB, GLA_GROUP, GLA_HEADS, GLA_BREF_ROWS
    NS = C // SUB
    CPG = GRP // C
    kd = H * dk
    scale = dk ** -0.5

    @pl.when(pl.program_id(0) == 0)
    def _():
        st_ref[...] = jnp.zeros_like(st_ref)

    r = lax.broadcasted_iota(I32, (GRP, GRP), 0)
    c = lax.broadcasted_iota(I32, (GRP, GRP), 1)
    same = (r // SUB) == (c // SUB)
    low = jnp.where(same & (c <= r), 1.0, 0.0).astype(BF16)
    upp = jnp.where(same & (c > r), 1.0, 0.0).astype(BF16)
    rr = lax.broadcasted_iota(I32, (CPG * BR, GRP), 0)
    cc = lax.broadcasted_iota(I32, (CPG * BR, GRP), 1)
    blk = rr % BR
    mref = jnp.where(((cc // C) == (rr // BR)) & ((cc % C) < SUB * blk) & (blk <= NS),
                     1.0, 0.0).astype(BF16)

    groups = [pl.ds(gi * GRP, GRP) for gi in range(tb // GRP)]
    zs = [_dot(gl_ref[sl, :].astype(BF16), wgk_ref[...]) + bgk_ref[...] for sl in groups]
    las = [(jnp.minimum(z, 0.0) - jnp.log(1.0 + jnp.exp(-jnp.abs(z)))) * (1.0 / GATE_NORMALIZER) for z in zs]
    his = [la.astype(BF16) for la in las]
    los = [(la - hi.astype(F32)).astype(BF16) for la, hi in zip(las, his)]
    prefix = lambda m: [_dot(m, hi) + _dot(m, lo) for hi, lo in zip(his, los)]
    bss, rus, brs = prefix(low), prefix(upp), prefix(mref)
    for gi, sl in enumerate(groups):
        bs_ref[sl, :] = bss[gi]
        qt_ref[sl, :] = q_ref[sl, :].astype(F32) * jnp.exp(bss[gi]) * scale
        kh_ref[sl, :] = k_ref[sl, :].astype(F32) * jnp.exp(rus[gi])
        bref_ref[pl.ds(gi * CPG, CPG)] = brs[gi].reshape(CPG, BR, kd)

    lane = lax.broadcasted_iota(I32, (SUB, C), 1)
    subrow = lax.broadcasted_iota(I32, (SUB, dk), 0)

    def scores_pivoted(qin, kraw, bs, bn):
        neg_b = jnp.concatenate([-(bs[J * SUB:(J + 1) * SUB] + bn[J:J + 1]) for J in range(NS)], axis=0)
        s = _dot_nt(qin.astype(BF16), (kraw * jnp.exp(neg_b)).astype(BF16))
        ri = lax.broadcasted_iota(I32, (C, C), 0)
        ci = lax.broadcasted_iota(I32, (C, C), 1)
        return jnp.where(ci <= ri, s, 0.0)

    def scores_stable(qt, kh, qraw, kraw, bs, bn):
        a_rows = []
        for I in range(NS):
            blk_rows = slice(I * SUB, (I + 1) * SUB)
            q_blk, k_blk, bs_blk = qraw[blk_rows], kraw[blk_rows], bs[blk_rows]
            acc = jnp.zeros((SUB, C), F32)
            for j in range(SUB):
                e = jnp.where(subrow >= j, jnp.exp(bs_blk - bs_blk[j:j + 1]), 0.0)
                p = (q_blk * k_blk[j:j + 1]) * e
                acc = jnp.where(lane == I * SUB + j, jnp.sum(p, axis=-1, keepdims=True), acc)
            if I > 0:
                parts = []
                for J in range(NS):
                    if J < I:
                        parts.append(kh[J * SUB:(J + 1) * SUB] * jnp.exp(bn[I:I + 1] - bn[J + 1:J + 2]))
                    else:
                        parts.append(jnp.zeros((SUB, dk), F32))
                k_dec = jnp.concatenate(parts, axis=0)
                acc = acc + _dot_nt(qt[blk_rows].astype(BF16), k_dec.astype(BF16))
            a_rows.append(acc)
        return jnp.concatenate(a_rows, axis=0)

    def chunk(n, carry, *, pivoted):
        r0 = pl.multiple_of(n * C, C)
        rows = pl.ds(r0, C)
        heads = range(H)
        kls = [slice(h * dk, (h + 1) * dk) for h in heads]
        bns = [bref_ref[n, :, kl] for kl in kls]
        ebs = [jnp.exp(bn) for bn in bns]
        sts = [st_ref[h] for h in heads]
        qts = [qt_ref[rows, kl] for kl in kls]
        khs = [kh_ref[rows, kl] for kl in kls]
        vs = [v_ref[rows, h * dv:(h + 1) * dv] for h in heads]
        qins = [jnp.concatenate([qt[I * SUB:(I + 1) * SUB] * eb[I:I + 1] for I in range(NS)], axis=0)
                for qt, eb in zip(qts, ebs)]
        os = [_dot_nt(qin.astype(BF16), st.astype(BF16)) for qin, st in zip(qins, sts)]
        if pivoted:
            scores = [scores_pivoted(qins[h], k_ref[rows, kls[h]].astype(F32), bs_ref[rows, kls[h]], bns[h])
                      for h in heads]
        else:
            scores = [scores_stable(qts[h], khs[h], q_ref[rows, kls[h]].astype(F32) * scale,
                                    k_ref[rows, kls[h]].astype(F32), bs_ref[rows, kls[h]], bns[h])
                      for h in heads]
        os = [o + _dot(a.astype(BF16), v) for o, a, v in zip(os, scores, vs)]
        k_sts = [jnp.concatenate(
            [kh[J * SUB:(J + 1) * SUB] * jnp.exp(bn[NS:NS + 1] - bn[J + 1:J + 2]) for J in range(NS)], axis=0)
            for kh, bn in zip(khs, bns)]
        for h in heads:
            st_ref[h] = sts[h] * ebs[h][NS:NS + 1] + _dot_tn(vs[h], k_sts[h].astype(BF16))
        outs = [_rms(o, hnorm_ref[...]) for o in os]

        g = g_ref[rows, :].astype(F32)
        o_ref[rows, :] = (jnp.concatenate(outs, axis=1) * (g * jax.nn.sigmoid(g))).astype(o_ref.dtype)
        return carry

    span = jnp.max(-bref_ref[...])

    @pl.when(span < GLA_SAFE_DECAY)
    def _():
        lax.fori_loop(0, tb // C, functools.partial(chunk, pivoted=True), 0, unroll=2)

    @pl.when(jnp.logical_not(span < GLA_SAFE_DECAY))
    def _():
        lax.fori_loop(0, tb // C, functools.partial(chunk, pivoted=False), 0)


def gla(proj, gl, w_gk2, b_gk, head_norm, *, dk, dv):
    T = proj.shape[0]
    H = GLA_HEADS
    tb = _tile(T, 512)
    assert tb % GLA_GROUP == 0 and dk % LANES == 0 and dv % LANES == 0
    kd, vd = H * dk, H * dv
    assert (2 * kd) % vd == 0
    v_off = (2 * kd) // vd
    fixed = lambda t: (0, 0)
    return pl.pallas_call(
        functools.partial(_gla_kernel, dk=dk, dv=dv, tb=tb),
        grid=(T // tb,),
        in_specs=[
            pl.BlockSpec((tb, kd), lambda t: (t, 0)),
            pl.BlockSpec((tb, kd), lambda t: (t, 1)),
            pl.BlockSpec((tb, vd), lambda t: (t, v_off)),
            pl.BlockSpec((tb, vd), lambda t: (t, v_off + 1)),
            pl.BlockSpec((tb, LANES), lambda t: (t, 0)),
            pl.BlockSpec((LANES, kd), fixed),
            pl.BlockSpec((1, kd), fixed),
            pl.BlockSpec((1, dv), fixed),
        ],
        out_specs=pl.BlockSpec((tb, vd), lambda t: (t, 0)),
        out_shape=jax.ShapeDtypeStruct((T, vd), BF16),
        scratch_shapes=[
            pltpu.VMEM((H, dv, dk), F32),
            pltpu.VMEM((tb, kd), F32),
            pltpu.VMEM((tb, kd), F32),
            pltpu.VMEM((tb, kd), F32),
            pltpu.VMEM((tb // GLA_CHUNK, GLA_BREF_ROWS, kd), F32),
        ],
        compiler_params=_params("arbitrary"),
        name="gla",
    )(proj, proj, proj, proj, gl, w_gk2, b_gk, head_norm)


def _route(hn, rw_ref, rb_ref):
    tm = hn.shape[0]
    hi = hn.astype(BF16)
    lo = (hn - hi.astype(F32)).astype(BF16)
    both = _dot(hi, rw_ref[...])
    logits = both[:, :LANES] + both[:, LANES:] + _dot(lo, rw_ref[:, :LANES]) + rb_ref[...]
    lane = lax.broadcasted_iota(I32, (tm, LANES), 1)
    lane_f = lane.astype(F32)
    neg = -jnp.inf
    far = float(LANES)

    def first_max(vals):
        m = jnp.max(vals, axis=-1, keepdims=True)
        idx = jnp.min(jnp.where(vals == m, lane_f, far), axis=-1, keepdims=True)
        return m, idx

    lg = jnp.where((lane >= N_EXPERTS) & (lane < N_EXPERTS + N_GROUPS), logits, neg)
    mg, gidx = first_max(lg)
    top_gp = 1.0 / jnp.sum(jnp.exp(lg - mg), axis=-1, keepdims=True)
    grp = gidx.astype(I32) - N_EXPERTS
    le = jnp.where((lane < N_EXPERTS) & ((lane // EXPERTS_PER_GROUP) == grp), logits, neg)
    m1, i1 = first_max(le)
    m2, i2 = first_max(jnp.where(lane_f == i1, neg, le))
    e2 = jnp.exp(m2 - m1)
    w1 = 1.0 / (1.0 + e2)
    return i1.astype(I32), i2.astype(I32), top_gp * w1, top_gp * (e2 * w1)


def _out_router_kernel(*refs, conv):
    if conv:
        (pb_ref, pc_ref, ph_ref, hc_ref, hh_ref, cw_ref, w_ref, hin_ref, fnw_ref, rw_ref,
         rb_ref, h_ref, hn_ref, id0_ref, id1_ref, g0_ref, g1_ref, cnt_ref, hbuf0, hbuf1) = refs
    else:
        (y_ref, w_ref, hin_ref, fnw_ref, rw_ref,
         rb_ref, h_ref, hn_ref, id0_ref, id1_ref, g0_ref, g1_ref, cnt_ref, hbuf0, hbuf1) = refs
    i = pl.program_id(0)
    hbufs = (hbuf0, hbuf1)

    @pl.when(i == 0)
    def _():
        cnt_ref[...] = jnp.zeros_like(cnt_ref)
        hbuf1[...] = jnp.zeros_like(hbuf1)

    def project(dst):
        if conv:
            u = pc_ref[...].astype(F32) * ph_ref[...].astype(F32)
            prev = hc_ref[...].astype(F32) * hh_ref[...].astype(F32)
            prev = jnp.where(i > 0, prev, 0.0)
            last, last2 = prev[BF16_ROWS - 1:BF16_ROWS], prev[BF16_ROWS - 2:BF16_ROWS - 1]
            r1, r2 = pltpu.roll(u, 1, 0), pltpu.roll(u, 2, 0)
            row = lax.broadcasted_iota(I32, (SUBLANES, u.shape[1]), 0)
            top1 = jnp.where(row == 0, last, r1[:SUBLANES])
            top2 = jnp.where(row == 0, last2, jnp.where(row == 1, last, r2[:SUBLANES]))
            u1 = jnp.concatenate([top1, r1[SUBLANES:]], axis=0)
            u2 = jnp.concatenate([top2, r2[SUBLANES:]], axis=0)
            cw = cw_ref[...]
            y = (pb_ref[...].astype(F32) * (cw[0:1] * u2 + cw[1:2] * u1 + cw[2:3] * u)).astype(BF16)
        else:
            y = y_ref[...]
        h = hin_ref[...] + _dot(y, w_ref[...])
        h_ref[...] = h
        dst[...] = h

    def route(src):
        hn = _rms(src[...], fnw_ref[...])
        _store_packed(hn_ref, hn)
        i1, i2, g1, g2 = _route(hn, rw_ref, rb_ref)
        shape = id0_ref.shape
        id0_ref[...] = jnp.broadcast_to(i1, shape)
        id1_ref[...] = jnp.broadcast_to(i2, shape)
        g0_ref[...] = jnp.broadcast_to(g1, shape)
        g1_ref[...] = jnp.broadcast_to(g2, shape)
        lane = lax.broadcasted_iota(I32, shape, 1)
        hit = jnp.where(((lane == i1) | (lane == i2)) & (i > 0), 1.0, 0.0)
        cnt_ref[...] += jnp.sum(hit, axis=0, keepdims=True)

    for s in range(2):
        @pl.when(i % 2 == s)
        def _():
            route(hbufs[1 - s])
            project(hbufs[s])


def out_router(y, w_out, h_in, ffn_norm_w, r_w, r_b, conv_w=None):
    T, D = h_in.shape
    K = w_out.shape[0]
    tm = _tile(T, 256)
    n_tiles = T // tm
    conv = conv_w is not None
    tile = lambda i: jnp.minimum(i, n_tiles - 1)
    row = lambda i: (tile(i), 0)
    routed = lambda i: (jnp.maximum(i - 1, 0), 0)
    fixed = lambda i: (0, 0)
    if conv:
        hb = tm // BF16_ROWS
        halo = lambda col: (lambda i: (jnp.maximum(tile(i) * hb - 1, 0), col))
        in_specs = [
            pl.BlockSpec((tm, K), lambda i: (tile(i), 0)),
            pl.BlockSpec((tm, K), lambda i: (tile(i), 1)),
            pl.BlockSpec((tm, K), lambda i: (tile(i), 2)),
            pl.BlockSpec((BF16_ROWS, K), halo(1)),
            pl.BlockSpec((BF16_ROWS, K), halo(2)),
            pl.BlockSpec((SUBLANES, K), fixed),
        ]
        cw = jnp.zeros((SUBLANES, K), F32).at[:CONV_WIDTH].set(conv_w)
        args = [y, y, y, y, y, cw]
    else:
        in_specs = [pl.BlockSpec((tm, K), row)]
        args = [y]
    in_specs += [
        pl.BlockSpec((K, D), fixed),
        pl.BlockSpec((tm, D), row),
        pl.BlockSpec((1, D), fixed),
        pl.BlockSpec((D, 2 * LANES), fixed),
        pl.BlockSpec((1, LANES), fixed),
    ]
    args += [w_out, h_in, ffn_norm_w.reshape(1, D), r_w, r_b]
    wide = lambda dt: jax.ShapeDtypeStruct((T, LANES), dt)
    sub = _packed_sublanes(D)
    return pl.pallas_call(
        functools.partial(_out_router_kernel, conv=conv),
        grid=(n_tiles + 1,),
        in_specs=in_specs,
        out_specs=[pl.BlockSpec((tm, D), row), pl.BlockSpec((tm * sub, LANES), routed)]
        + [pl.BlockSpec((tm, LANES), routed)] * 4 + [pl.BlockSpec((1, LANES), fixed)],
        out_shape=[jax.ShapeDtypeStruct((T, D), F32), jax.ShapeDtypeStruct((T * sub, LANES), I32),
                   wide(I32), wide(I32), wide(F32), wide(F32), jax.ShapeDtypeStruct((1, LANES), F32)],
        scratch_shapes=[pltpu.VMEM((tm, D), F32)] * 2,
        compiler_params=_params("arbitrary"),
        name="out_router_conv" if conv else "out_router",
    )(*args)


def _rank_kernel(id0_ref, id1_ref, pst_ref, d0_ref, d1_ref, carry_ref):
    @pl.when(pl.program_id(0) == 0)
    def _():
        carry_ref[...] = jnp.zeros_like(carry_ref)

    tb = id0_ref.shape[0]
    lane = lax.broadcasted_iota(I32, (tb, LANES), 1)
    oh0 = lane == id0_ref[...]
    oh1 = lane == id1_ref[...]
    hit = jnp.where(oh0 | oh1, 1.0, 0.0)
    r = lax.broadcasted_iota(I32, (tb, tb), 0)
    c = lax.broadcasted_iota(I32, (tb, tb), 1)
    before = jnp.where(c < r, 1.0, 0.0).astype(BF16)
    base = _dot(before, hit.astype(BF16)) + carry_ref[...] + pst_ref[...]
    d0 = jnp.sum(jnp.where(oh0, base, 0.0), axis=-1, keepdims=True)
    d1 = jnp.sum(jnp.where(oh1, base, 0.0), axis=-1, keepdims=True)
    d0_ref[...] = jnp.broadcast_to(d0.astype(I32), (tb, LANES))
    d1_ref[...] = jnp.broadcast_to(d1.astype(I32), (tb, LANES))
    carry_ref[...] += jnp.sum(hit, axis=0, keepdims=True)


def rank(id0, id1, pstart_row):
    T = id0.shape[0]
    tb = _tile(T, 1024)
    row = lambda i: (i, 0)
    return pl.pallas_call(
        _rank_kernel,
        grid=(T // tb,),
        in_specs=[pl.BlockSpec((tb, LANES), row), pl.BlockSpec((tb, LANES), row),
                  pl.BlockSpec((1, LANES), lambda i: (0, 0))],
        out_specs=[pl.BlockSpec((tb, LANES), row)] * 2,
        out_shape=[jax.ShapeDtypeStruct((T, LANES), I32)] * 2,
        scratch_shapes=[pltpu.VMEM((1, LANES), F32)],
        compiler_params=_params("arbitrary"),
        name="rank",
    )(id0, id1, pstart_row)


def _invert_kernel(d0_ref, d1_ref, fill_ref, rt_ref):
    width = SUBLANES

    def clear(p, carry):
        for k in range(width):
            rt_ref[p * width + k] = 0
        return carry

    for e in range(fill_ref.shape[0] // 2):
        lax.fori_loop(fill_ref[2 * e] // width, fill_ref[2 * e + 1] // width, clear, 0)

    def place(t, carry):
        rt_ref[d0_ref[t]] = t
        rt_ref[d1_ref[t]] = t
        return carry

    lax.fori_loop(0, d0_ref.shape[0], place, 0, unroll=16)


def invert(d0, d1, fill_ranges, n_rows):
    smem = pl.BlockSpec(memory_space=pltpu.SMEM)
    return pl.pallas_call(
        _invert_kernel,
        in_specs=[smem, smem, smem],
        out_specs=smem,
        out_shape=jax.ShapeDtypeStruct((n_rows,), I32),
        name="invert",
    )(d0, d1, fill_ranges)


def _row_copy(src_hbm, dst_ref, sem, src_row, dst_row, sub):
    src = src_hbm.at[pl.ds(pl.multiple_of(src_row * sub, sub), sub), :]
    dst = dst_ref.at[pl.ds(pl.multiple_of(dst_row * sub, sub), sub), :]
    return pltpu.make_async_copy(src, dst, sem)


def _experts_kernel(bexp_ref, first_ref, wslot_ref, next_ref, rtok_ref, nused_ref, hn_hbm, wgu_hbm, wd_hbm, y_ref,
                    xbuf0, xbuf1, xbuf2, wgu_f32, wd_f32, wgu_bf, wd_bf, sem, wsem, *, layer, ff, rb, sub):
    i = pl.program_id(0)
    n_used = nused_ref[0]
    bufs = (xbuf0, xbuf1, xbuf2)
    n_buf = len(bufs)
    assert n_buf == GATHER_AHEAD + 1

    def weight_copies(expert, slot):
        return (pltpu.make_async_copy(wgu_hbm.at[layer, expert], wgu_f32.at[slot], wsem.at[0, slot]),
                pltpu.make_async_copy(wd_hbm.at[layer, expert], wd_f32.at[slot], wsem.at[1, slot]))

    def wait_rows(s):
        pltpu.make_async_copy(hn_hbm.at[pl.ds(0, rb * sub), :], bufs[s], sem.at[s]).wait()

    def start_rows(blk, s):
        for r in range(rb):
            _row_copy(hn_hbm, bufs[s], sem.at[s], rtok_ref[blk * rb + r], r, sub).start(priority=ROW_QUEUE)

    @pl.when(i == 0)
    def _():
        for cp in weight_copies(bexp_ref[0], 0):
            cp.start(priority=WEIGHT_QUEUE)
        for blk in range(GATHER_AHEAD):
            start_rows(blk, blk)

    @pl.when(first_ref[i] == 1)
    def _():
        slot = wslot_ref[i]
        for cp in weight_copies(bexp_ref[i], slot):
            cp.wait()
        nxt = next_ref[i]

        @pl.when(nxt >= 0)
        def _():
            for cp in weight_copies(nxt, 1 - slot):
                cp.start(priority=WEIGHT_QUEUE)

        wgu_bf[...] = wgu_f32[slot].astype(BF16)
        wd_bf[...] = wd_f32[slot].astype(BF16)

    for s in range(n_buf):
        @pl.when((i < n_used) & (i % n_buf == s))
        def _():
            wait_rows(s)
            start_rows(i + GATHER_AHEAD, (s + GATHER_AHEAD) % n_buf)
            los, his = _load_packed(bufs[s], rb, sub)
            x = jnp.concatenate([p.astype(BF16) for p in los + his], axis=1)
            gu = _dot(x, wgu_bf[...])
            gt, up = gu[:, :ff], gu[:, ff:]
            act = (gt * jax.nn.sigmoid(gt) * up).astype(BF16)
            _store_packed(y_ref, _dot(act, wd_bf[...]))

        @pl.when((i >= n_used) & (i < n_used + GATHER_AHEAD) & (i % n_buf == s))
        def _():
            wait_rows(s)

    @pl.when(i >= n_used)
    def _():
        y_ref[...] = jnp.zeros_like(y_ref)


def experts(block_exp, row_tok, n_used, hn, w_gate_up, w_down, layer):
    D, ff2 = w_gate_up.shape[-2:]
    ff = ff2 // 2
    sub = _packed_sublanes(D)
    P = row_tok.shape[0]
    rb = ROW_BLOCK
    n_blocks = P // rb

    blk = jnp.arange(n_blocks, dtype=I32)
    prev = jnp.concatenate([jnp.full((1,), -1, I32), block_exp[:-1]])
    first = (blk < n_used[0]) & (block_exp != prev)
    wslot = (jnp.cumsum(first.astype(I32)) - 1) % 2
    first_at = lax.cummin(jnp.where(first, blk, n_blocks), reverse=True)
    next_first = jnp.concatenate([first_at[1:], jnp.full((1,), n_blocks, I32)])
    next_exp = jnp.where(next_first < n_blocks, block_exp[jnp.minimum(next_first, n_blocks - 1)], -1)

    grid_spec = pltpu.PrefetchScalarGridSpec(
        num_scalar_prefetch=6,
        grid=(n_blocks,),
        in_specs=[pl.BlockSpec(memory_space=pl.ANY)] * 3,
        out_specs=pl.BlockSpec((rb * sub, LANES), lambda i, *_: (i, 0)),
        scratch_shapes=[pltpu.VMEM((rb * sub, LANES), I32)] * (GATHER_AHEAD + 1)
        + [pltpu.VMEM((2, D, ff2), F32), pltpu.VMEM((2, ff, D), F32),
           pltpu.VMEM((D, ff2), BF16), pltpu.VMEM((ff, D), BF16),
           pltpu.SemaphoreType.DMA((GATHER_AHEAD + 1,)), pltpu.SemaphoreType.DMA((2, 2))],
    )
    return pl.pallas_call(
        functools.partial(_experts_kernel, layer=layer, ff=ff, rb=rb, sub=sub),
        grid_spec=grid_spec,
        out_shape=jax.ShapeDtypeStruct((P * sub, LANES), I32),
        compiler_params=_params("arbitrary"),
        name="experts",
    )(block_exp, first.astype(I32), wslot.astype(I32), next_exp.astype(I32), row_tok, n_used,
      hn, w_gate_up, w_down)


def _combine_kernel(d0_ref, d1_ref, yb_hbm, h_ref, g0_ref, g1_ref, nw_ref, *refs, mode, sub):
    if mode == "both":
        o_ref, on_ref, b00, b01, b10, b11, sem = refs
    else:
        o_ref, b00, b01, b10, b11, sem = refs
    i = pl.program_id(0)
    n = pl.num_programs(0)
    tc = o_ref.shape[0]
    bufs = ((b00, b01), (b10, b11))
    d_refs = (d0_ref, d1_ref)

    def wait_rows(s):
        for k in range(TOP_K):
            pltpu.make_async_copy(yb_hbm.at[pl.ds(0, tc * sub), :], bufs[s][k], sem.at[s]).wait()

    def start_rows(blk, s):
        for r in range(tc):
            for k in range(TOP_K):
                _row_copy(yb_hbm, bufs[s][k], sem.at[s], d_refs[k][blk * tc + r], r, sub).start(priority=k)

    @pl.when(i == 0)
    def _():
        start_rows(0, 0)

    for s in range(2):
        @pl.when(i % 2 == s)
        def _():
            wait_rows(s)
            start_rows(i + 1, 1 - s)
            lo0, hi0 = _load_packed(bufs[s][0], tc, sub)
            lo1, hi1 = _load_packed(bufs[s][1], tc, sub)
            g0, g1 = g0_ref[...], g1_ref[...]
            pieces = [h_ref[:, j * LANES:(j + 1) * LANES] + g0 * y0 + g1 * y1
                      for j, (y0, y1) in enumerate(zip(lo0 + hi0, lo1 + hi1))]
            if mode != "norm":
                for j, piece in enumerate(pieces):
                    o_ref[:, j * LANES:(j + 1) * LANES] = piece
            if mode != "sum":
                normed = _rms(jnp.concatenate(pieces, axis=1), nw_ref[...])
                if mode == "norm":
                    o_ref[...] = normed
                else:
                    on_ref[...] = normed.astype(on_ref.dtype)

            @pl.when(i == n - 1)
            def _():
                wait_rows(1 - s)


def combine(d0, d1, yb, h, g0, g1, norm_w=None, mode="sum"):
    T, D = h.shape
    tc = _tile(T, 128)
    sub = _packed_sublanes(D)
    assert (norm_w is None) == (mode == "sum")
    nw = (jnp.ones((D,), F32) if norm_w is None else norm_w).reshape(1, D)
    spare = jnp.zeros((tc,), I32)
    row = lambda i, a, b: (i, 0)
    out_shape = [jax.ShapeDtypeStruct((T, D), F32)]
    if mode == "both":
        out_shape.append(jax.ShapeDtypeStruct((T, D), BF16))
    grid_spec = pltpu.PrefetchScalarGridSpec(
        num_scalar_prefetch=2,
        grid=(T // tc,),
        in_specs=[
            pl.BlockSpec(memory_space=pl.ANY),
            pl.BlockSpec((tc, D), row),
            pl.BlockSpec((tc, LANES), row),
            pl.BlockSpec((tc, LANES), row),
            pl.BlockSpec((1, D), lambda i, a, b: (0, 0)),
        ],
        out_specs=[pl.BlockSpec((tc, D), row)] * len(out_shape),
        scratch_shapes=[pltpu.VMEM((tc * sub, LANES), I32)] * (2 * TOP_K) + [pltpu.SemaphoreType.DMA((2,))],
    )
    out = pl.pallas_call(
        functools.partial(_combine_kernel, mode=mode, sub=sub),
        grid_spec=grid_spec,
        out_shape=out_shape,
        compiler_params=_params("arbitrary"),
        name="combine_" + mode,
    )(jnp.concatenate([d0, spare]), jnp.concatenate([d1, spare]), yb, h, g0, g1, nw)
    return out if mode == "both" else out[0]


def _pad_cols(w, n):
    return jnp.pad(w, ((0, 0), (0, n - w.shape[1])))


def moe(h, hn, id0, id1, g0, g1, counts, w_gate_up, w_down, layer, norm_w, norm_mode):
    T, D = h.shape
    rb = ROW_BLOCK
    n_blocks = -(-(T * TOP_K + N_EXPERTS * (rb - 1)) // rb) + GATHER_AHEAD
    cnt = counts[0, :N_EXPERTS].astype(I32)
    padded = ((cnt + rb - 1) // rb) * rb
    pends = jnp.cumsum(padded)
    pstart_row = _pad_cols((pends - padded).astype(F32).reshape(1, N_EXPERTS), LANES)
    d0b, d1b = rank(id0, id1, pstart_row)
    d0, d1 = d0b[:, 0], d1b[:, 0]
    fill_start = jnp.concatenate([pends - padded + cnt, pends[-1:]])
    fill_end = jnp.concatenate([pends, jnp.full((1,), n_blocks * rb, I32)])
    row_tok = invert(d0, d1, jnp.stack([fill_start, fill_end], axis=1).reshape(-1), n_blocks * rb)
    block_start = jnp.arange(n_blocks, dtype=I32) * rb
    block_exp = jnp.minimum(jnp.sum(pends[None, :] <= block_start[:, None], axis=1), N_EXPERTS - 1).astype(I32)
    n_used = (pends[-1:] // rb).astype(I32)
    yb = experts(block_exp, row_tok, n_used, hn, w_gate_up, w_down, layer)
    return combine(d0, d1, yb, h, g0, g1, norm_w, norm_mode)


def _router_weights(w_group, b_group, w_router, b_router):
    w = _pad_cols(jnp.concatenate([w_router, w_group], axis=1), LANES)
    b = _pad_cols(jnp.concatenate([b_router, b_group]).reshape(1, -1), LANES)
    hi = w.astype(BF16)
    lo = (w - hi.astype(F32)).astype(BF16)
    return jnp.concatenate([hi, lo], axis=1), b


def kernel(x, mix_norm, gla_w_in, gla_w_gk2, gla_b_gk, gla_head_norm, gla_w_out, conv_w_in, conv_w,
           conv_w_out, ffn_norm, w_group, b_group, w_router, b_router, w_gate_up, w_down, final_norm):
    B, T, D = x.shape
    assert B == 1, "the recurrence state is carried across the whole row axis"
    h = x.reshape(T, D)
    kd, vd = D // 2, D
    dk, dv = kd // GLA_HEADS, vd // GLA_HEADS

    w_in = gla_w_in[0]
    n_main = 2 * kd + 2 * vd
    proj, gl = norm_proj(h, mix_norm[0], w_in.astype(BF16),
                         _pad_cols(w_in[:, n_main:], LANES).astype(BF16), n_cols=n_main)
    w_gk2 = jnp.pad(gla_w_gk2[0], ((0, LANES - GATE_RANK), (0, 0))).astype(BF16)
    o = gla(proj, gl, w_gk2, gla_b_gk[0].reshape(1, kd), gla_head_norm[0].reshape(1, dv), dk=dk, dv=dv)
    routed = out_router(o, gla_w_out[0].astype(BF16), h, ffn_norm[0],
                        *_router_weights(w_group[0], b_group[0], w_router[0], b_router[0]))
    h, hn = moe(*routed, w_gate_up, w_down, 0, mix_norm[1], "both")

    proj = proj_only(hn, conv_w_in[0].astype(BF16))
    routed = out_router(proj, conv_w_out[0].astype(BF16), h, ffn_norm[1],
                        *_router_weights(w_group[1], b_group[1], w_router[1], b_router[1]),
                        conv_w=conv_w[0])
    out = moe(*routed, w_gate_up, w_down, 1, final_norm, "norm")
    return out.reshape(B, T, D)
```
